```python
import jax, jax.numpy as jnp
from jax import lax
import numpy as np


D_MODEL = 1024
BATCH = 1
SEQ = 16384
DEPTH = 2
DEC_BATCH = 4
DEC_SEQ = 8192
PAST_LEN = 128

N_MIXERS = 2
N_A_LAYERS = (DEPTH + N_MIXERS - 1) // N_MIXERS
N_B_LAYERS = DEPTH // N_MIXERS
MLA_HEADS = 16
Q_LORA = 256
KV_LORA = 128
NOPE_DIM = 64
ROPE_DIM = 32
QK_HEAD = NOPE_DIM + ROPE_DIM
V_HEAD = 64
ROPE_THETA = 10000.0
Q_BLOCK = 128
CHUNK = 128
SGU_HALF = D_MODEL
SGU_GROUPS = 8
SGU_GROUP_DIM = SGU_HALF // SGU_GROUPS
N_EXPERTS = 16
EXPERT_FF = 1024
CAPACITY_FACTOR = 2
EPS = 1e-6

kernel_name = 'hybrid_mla_sgu_ec_moe_encoder'


def rms_norm(x, g):
    xf = x.astype(jnp.float32)
    y = xf * lax.rsqrt(jnp.mean(xf * xf, axis=-1, keepdims=True) + EPS)
    return (y * g.astype(jnp.float32)).astype(x.dtype)


def layer_norm(x, g, b):
    xf = x.astype(jnp.float32)
    mu = jnp.mean(xf, axis=-1, keepdims=True)
    xc = xf - mu
    y = xc * lax.rsqrt(jnp.mean(xc * xc, axis=-1, keepdims=True) + EPS)
    return (y * g.astype(jnp.float32) + b.astype(jnp.float32)).astype(x.dtype)


def rotate_half(x, cos, sin):
    xf = x.astype(jnp.float32)
    x1, x2 = jnp.split(xf, 2, axis=-1)
    return jnp.concatenate([x1 * cos - x2 * sin, x2 * cos + x1 * sin], axis=-1).astype(x.dtype)


def mla_mixer(h, w_in, q_lora_g, kv_lora_g, w_uq, w_ukv, q_norm_g, k_norm_g, w_out):
    B, S, _ = h.shape
    proj = h @ w_in
    c_q = rms_norm(proj[..., :Q_LORA], q_lora_g)
    c_kv = rms_norm(proj[..., Q_LORA:Q_LORA + KV_LORA], kv_lora_g)
    k_pe = proj[..., Q_LORA + KV_LORA:]
    q = (c_q @ w_uq).reshape(B, S, MLA_HEADS, QK_HEAD)
    kv = (c_kv @ w_ukv).reshape(B, S, MLA_HEADS, NOPE_DIM + V_HEAD)
    k_nope, v = kv[..., :NOPE_DIM], kv[..., NOPE_DIM:]
    k = jnp.concatenate([k_nope, jnp.broadcast_to(k_pe[:, :, None, :], (B, S, MLA_HEADS, ROPE_DIM))], axis=-1)
    q = rms_norm(q, q_norm_g)
    k = rms_norm(k, k_norm_g)
    pos = jnp.arange(S, dtype=jnp.float32)
    inv_freq = 1.0 / (ROPE_THETA ** (jnp.arange(0, ROPE_DIM, 2, dtype=jnp.float32) / ROPE_DIM))
    ang = pos[:, None] * inv_freq[None, :]
    cos = jnp.cos(ang)[:, None, :]
    sin = jnp.sin(ang)[:, None, :]
    q = jnp.concatenate([q[..., :NOPE_DIM], rotate_half(q[..., NOPE_DIM:], cos, sin)], axis=-1)
    k = jnp.concatenate([k[..., :NOPE_DIM], rotate_half(k[..., NOPE_DIM:], cos, sin)], axis=-1)
    scale = QK_HEAD ** -0.5
    qb = q.reshape(B, S // Q_BLOCK, Q_BLOCK, MLA_HEADS, QK_HEAD).transpose(1, 0, 2, 3, 4)

    def attend(q_blk):
        s = jnp.einsum('bqhd,bkhd->bhqk', q_blk, k, preferred_element_type=jnp.float32) * scale
        p = jax.nn.softmax(s, axis=-1)
        return jnp.einsum('bhqk,bkhd->bqhd', p.astype(v.dtype), v)

    o = lax.map(attend, qb)
    o = o.transpose(1, 0, 2, 3, 4).reshape(B, S, MLA_HEADS * V_HEAD)
    return o @ w_out


def sgu_mixer(h, w_in, ln_g, ln_b, w_s, b_s, w_out):
    B, S, _ = h.shape
    z = jax.nn.gelu(h @ w_in, approximate=False)
    u, v = jnp.split(z, 2, axis=-1)
    v = layer_norm(v, ln_g, ln_b)
    vc = v.reshape(B, S // CHUNK, CHUNK, SGU_GROUPS, SGU_GROUP_DIM)
    sv = jnp.einsum('gpq,bnqgc->bnpgc', w_s, vc) + b_s.T[:, :, None]
    return (u * sv.reshape(B, S, SGU_HALF)) @ w_out


def expert_choice_moe(h, w_router, w_gate, w_up, w_down):
    B, S, D = h.shape
    n = B * S
    cap = max(1, CAPACITY_FACTOR * n // N_EXPERTS)
    xt = h.reshape(n, D)
    aff = jax.nn.softmax((xt @ w_router).astype(jnp.float32), axis=-1)
    gate, idx = lax.top_k(aff.T, cap)
    xg = xt[idx]
    hid = jax.nn.silu(jnp.einsum('ecd,edf->ecf', xg, w_gate)) * jnp.einsum('ecd,edf->ecf', xg, w_up)
    y = jnp.einsum('ecf,efd->ecd', hid, w_down) * gate[..., None].astype(h.dtype)
    out = jnp.zeros_like(xt).at[idx.reshape(-1)].add(y.reshape(-1, D))
    return out.reshape(B, S, D)


def encoder_trunk(x, norm1_g, norm2_g,
                  mla_w_in, mla_q_lora_g, mla_kv_lora_g, mla_w_uq, mla_w_ukv, mla_q_norm_g, mla_k_norm_g, mla_w_out,
                  sgu_w_in, sgu_ln_g, sgu_ln_b, sgu_w_s, sgu_b_s, sgu_w_out,
                  moe_w_router, moe_w_gate, moe_w_up, moe_w_down):
    for i in range(DEPTH):
        h = rms_norm(x, norm1_g[i])
        j = i // N_MIXERS
        if i % N_MIXERS == 0:
            mix = mla_mixer(h, mla_w_in[j], mla_q_lora_g[j], mla_kv_lora_g[j], mla_w_uq[j], mla_w_ukv[j],
                            mla_q_norm_g[j], mla_k_norm_g[j], mla_w_out[j])
        else:
            mix = sgu_mixer(h, sgu_w_in[j], sgu_ln_g[j], sgu_ln_b[j], sgu_w_s[j], sgu_b_s[j], sgu_w_out[j])
        x = x + mix
        x = x + expert_choice_moe(rms_norm(x, norm2_g[i]), moe_w_router[i], moe_w_gate[i], moe_w_up[i], moe_w_down[i])
    return x


def setup_inputs(seed: int = 0) -> dict:
    key = jax.random.key(seed)
    ks = jax.random.split(key, 24)
    f32 = jnp.float32

    def nrm(k, shape, scale):
        return jax.random.normal(k, shape, f32) * scale

    def gain(k, shape):
        return 1.0 + 0.01 * jax.random.normal(k, shape, f32)

    return {
        'x_prompt': nrm(ks[0], (BATCH, SEQ, D_MODEL), 1.0),
        'x_sample': nrm(ks[1], (DEC_BATCH, DEC_SEQ, D_MODEL), 1.0),
        'norm1_g': gain(ks[2], (DEPTH, D_MODEL)),
        'norm2_g': gain(ks[3], (DEPTH, D_MODEL)),
        'mla_w_in': nrm(ks[4], (N_A_LAYERS, D_MODEL, Q_LORA + KV_LORA + ROPE_DIM), D_MODEL ** -0.5),
        'mla_q_lora_g': gain(ks[5], (N_A_LAYERS, Q_LORA)),
        'mla_kv_lora_g': gain(ks[6], (N_A_LAYERS, KV_LORA)),
        'mla_w_uq': nrm(ks[7], (N_A_LAYERS, Q_LORA, MLA_HEADS * QK_HEAD), Q_LORA ** -0.5),
        'mla_w_ukv': nrm(ks[8], (N_A_LAYERS, KV_LORA, MLA_HEADS * (NOPE_DIM + V_HEAD)), KV_LORA ** -0.5),
        'mla_q_norm_g': gain(ks[9], (N_A_LAYERS, QK_HEAD)),
        'mla_k_norm_g': gain(ks[10], (N_A_LAYERS, QK_HEAD)),
        'mla_w_out': nrm(ks[11], (N_A_LAYERS, MLA_HEADS * V_HEAD, D_MODEL), (MLA_HEADS * V_HEAD) ** -0.5),
        'sgu_w_in': nrm(ks[12], (N_B_LAYERS, D_MODEL, 2 * SGU_HALF), D_MODEL ** -0.5),
        'sgu_ln_g': gain(ks[13], (N_B_LAYERS, SGU_HALF)),
        'sgu_ln_b': nrm(ks[14], (N_B_LAYERS, SGU_HALF), 0.02),
        'sgu_w_s': nrm(ks[15], (N_B_LAYERS, SGU_GROUPS, CHUNK, CHUNK), CHUNK ** -0.5),
        'sgu_b_s': nrm(ks[16], (N_B_LAYERS, SGU_GROUPS, CHUNK), 0.02),
        'sgu_w_out': nrm(ks[17], (N_B_LAYERS, SGU_HALF, D_MODEL), SGU_HALF ** -0.5),
        'moe_w_router': nrm(ks[18], (DEPTH, D_MODEL, N_EXPERTS), D_MODEL ** -0.5),
        'moe_w_gate': nrm(ks[19], (DEPTH, N_EXPERTS, D_MODEL, EXPERT_FF), D_MODEL ** -0.5),
        'moe_w_up': nrm(ks[20], (DEPTH, N_EXPERTS, D_MODEL, EXPERT_FF), D_MODEL ** -0.5),
        'moe_w_down': nrm(ks[21], (DEPTH, N_EXPERTS, EXPERT_FF, D_MODEL), EXPERT_FF ** -0.5),
    }


def reference(x_prompt, x_sample, norm1_g, norm2_g,
              mla_w_in, mla_q_lora_g, mla_kv_lora_g, mla_w_uq, mla_w_ukv, mla_q_norm_g, mla_k_norm_g, mla_w_out,
              sgu_w_in, sgu_ln_g, sgu_ln_b, sgu_w_s, sgu_b_s, sgu_w_out,
              moe_w_router, moe_w_gate, moe_w_up, moe_w_down):
    y_prompt = encoder_trunk(x_prompt, norm1_g, norm2_g,
                             mla_w_in, mla_q_lora_g, mla_kv_lora_g, mla_w_uq, mla_w_ukv, mla_q_norm_g, mla_k_norm_g, mla_w_out,
                             sgu_w_in, sgu_ln_g, sgu_ln_b, sgu_w_s, sgu_b_s, sgu_w_out,
                             moe_w_router, moe_w_gate, moe_w_up, moe_w_down)
    y_sample = encoder_trunk(x_sample, norm1_g, norm2_g,
                             mla_w_in, mla_q_lora_g, mla_kv_lora_g, mla_w_uq, mla_w_ukv, mla_q_norm_g, mla_k_norm_g, mla_w_out,
                             sgu_w_in, sgu_ln_g, sgu_ln_b, sgu_w_s, sgu_b_s, sgu_w_out,
                             moe_w_router, moe_w_gate, moe_w_up, moe_w_down)
    return (y_prompt, y_sample)
```

```python
import functools

import jax
import jax.numpy as jnp
import numpy as np
from jax import lax
from jax.experimental import pallas as pl
from jax.experimental.pallas import tpu as pltpu

F32 = jnp.float32
BF16 = jnp.bfloat16
I32 = jnp.int32
U32 = jnp.uint32

D_MODEL = 1024
MLA_HEADS = 16
Q_LORA = 256
KV_LORA = 128
NOPE_DIM = 64
ROPE_DIM = 32
HALF_ROPE = ROPE_DIM // 2
QK_HEAD = NOPE_DIM + ROPE_DIM
V_HEAD = 64
ROPE_THETA = 10000.0
CHUNK = 128
SGU_HALF = D_MODEL
SGU_GROUPS = 8
N_EXPERTS = 16
EXPERT_FF = 1024
CAPACITY_FACTOR = 2
EPS = 1e-6

LANES = 128
QK_PAD = 128
V_ROWS = 80
TOK_TILE = 128
SUBLANES = 8
SLAB = TOK_TILE + 16
NOT_SELECTED = -1024
HEADS_PER_STEP = 2

VMEM_LIMIT = 52 * 1024 * 1024


def _cparams(sem):
    return pltpu.CompilerParams(dimension_semantics=sem, vmem_limit_bytes=VMEM_LIMIT)


def _dot(a, b):
    return jnp.dot(a, b, preferred_element_type=F32)


def _dot_nt(a, b):
    return lax.dot_general(a, b, (((1,), (1,)), ((), ())), preferred_element_type=F32)


def _rms_rows(x, g):
    return x * lax.rsqrt(jnp.mean(x * x, axis=-1, keepdims=True) + EPS) * g


def _mla_prep_kernel(x_ref, g1_ref, w_in_ref, gq_ref, gkv_ref, w_uqT_ref, w_uk_ref, w_uvT_ref,
                     gqc_ref, kcos_ref, ksin_ref, cosT_ref, sinT_ref,
                     qT_ref, k_ref, vT_ref):
    x = x_ref[...]
    h = _rms_rows(x, g1_ref[...]).astype(BF16)
    proj = _dot(h, w_in_ref[...])
    cq = _rms_rows(proj[:, :Q_LORA], gq_ref[...]).astype(BF16)
    ckv = _rms_rows(proj[:, Q_LORA:Q_LORA + KV_LORA], gkv_ref[...]).astype(BF16)
    kpe = proj[:, 384:512]
    kpe_sw = proj[:, 512:640]
    tm = x.shape[0]

    qT = _dot_nt(w_uqT_ref[...], cq)
    cosT = cosT_ref[...]
    sinT = sinT_ref[...]
    gqc = gqc_ref[...]
    zpad = jnp.zeros((QK_PAD - QK_HEAD, tm), BF16)
    for hd in range(MLA_HEADS):
        qh = qT[hd * QK_HEAD:(hd + 1) * QK_HEAD, :]
        r = lax.rsqrt(jnp.sum(qh * qh, axis=0, keepdims=True) * (1.0 / QK_HEAD) + EPS)
        qn = qh * gqc * r
        x1 = qn[NOPE_DIM:NOPE_DIM + HALF_ROPE]
        x2 = qn[NOPE_DIM + HALF_ROPE:QK_HEAD]
        qT_ref[0, hd, 0:NOPE_DIM, :] = qn[:NOPE_DIM].astype(BF16)
        qT_ref[0, hd, NOPE_DIM:NOPE_DIM + HALF_ROPE, :] = (x1 * cosT - x2 * sinT).astype(BF16)
        qT_ref[0, hd, NOPE_DIM + HALF_ROPE:QK_HEAD, :] = (x2 * cosT + x1 * sinT).astype(BF16)
        qT_ref[0, hd, QK_HEAD:QK_PAD, :] = zpad

    knope = _dot(ckv, w_uk_ref[...])
    kcos = kcos_ref[...]
    ksin = ksin_ref[...]
    for hd in range(MLA_HEADS):
        kh = knope[:, hd * QK_PAD:(hd + 1) * QK_PAD] + kpe
        r = lax.rsqrt(jnp.sum(kh * kh, axis=-1, keepdims=True) * (1.0 / QK_HEAD) + EPS)
        k_ref[0, hd, 0] = ((kh * kcos + kpe_sw * ksin) * r).astype(BF16)

    vT = _dot_nt(w_uvT_ref[...], ckv)
    row = lax.broadcasted_iota(I32, (V_ROWS - V_HEAD, tm), 0)
    tail = jnp.where(row == 0, 1.0, 0.0).astype(BF16)
    for hd in range(MLA_HEADS):
        vT_ref[0, hd, 0, 0:V_HEAD, :] = vT[hd * V_HEAD:(hd + 1) * V_HEAD, :].astype(BF16)
        vT_ref[0, hd, 0, V_HEAD:V_ROWS, :] = tail


def _mla_prep(x3, g1, wts, tabs, tm):
    B, S, _ = x3.shape
    nb = S // tm
    full = lambda a: pl.BlockSpec(a.shape, lambda b, i: (0,) * a.ndim)
    in_specs = [pl.BlockSpec((None, tm, D_MODEL), lambda b, i: (b, i, 0)), full(g1)]
    in_specs += [full(w) for w in wts]
    in_specs += [pl.BlockSpec((tm, LANES), lambda b, i: (i, 0)),
                 pl.BlockSpec((tm, LANES), lambda b, i: (i, 0)),
                 pl.BlockSpec((HALF_ROPE, tm), lambda b, i: (0, i)),
                 pl.BlockSpec((HALF_ROPE, tm), lambda b, i: (0, i))]
    out_shape = [jax.ShapeDtypeStruct((B, MLA_HEADS, QK_PAD, S), BF16),
                 jax.ShapeDtypeStruct((B, MLA_HEADS, nb, tm, QK_PAD), BF16),
                 jax.ShapeDtypeStruct((B, MLA_HEADS, nb, V_ROWS, tm), BF16)]
    out_specs = [pl.BlockSpec((1, MLA_HEADS, QK_PAD, tm), lambda b, i: (b, 0, 0, i)),
                 pl.BlockSpec((1, MLA_HEADS, 1, tm, QK_PAD), lambda b, i: (b, 0, i, 0, 0)),
                 pl.BlockSpec((1, MLA_HEADS, 1, V_ROWS, tm), lambda b, i: (b, 0, i, 0, 0))]
    return pl.pallas_call(
        _mla_prep_kernel, grid=(B, nb), in_specs=in_specs, out_specs=out_specs, out_shape=out_shape,
        compiler_params=_cparams(("arbitrary", "arbitrary")), name="mla_prep",
    )(x3, g1, *wts, *tabs)


def _attn_kernel(qT_ref, k_ref, vT_ref, o_ref, *, n_kt):
    outs = []
    for hh in range(HEADS_PER_STEP):
        qT = qT_ref[0, hh]
        tq = qT.shape[1]

        def body(kt, carry, hh=hh, qT=qT):
            m, acc = carry
            sT = _dot(k_ref[0, hh, kt], qT)
            m_new = jnp.maximum(m, jnp.max(sT, axis=0, keepdims=True))
            alpha = jnp.exp(m - m_new)
            pT = jnp.exp(sT - m_new).astype(BF16)
            acc = alpha * acc + _dot(vT_ref[0, hh, kt], pT)
            return m_new, acc

        m0 = jnp.full((1, tq), -jnp.inf, F32)
        acc0 = jnp.zeros((V_ROWS, tq), F32)
        _, acc = lax.fori_loop(0, n_kt, body, (m0, acc0))
        outs.append(acc[:V_HEAD] / acc[V_HEAD:V_HEAD + 1])
    o_ref[0] = jnp.concatenate(outs, axis=0).T.astype(o_ref.dtype)


def _attention(qT, k, vT, tq):
    B, H, _, S = qT.shape
    n_kt, tk = k.shape[2], k.shape[3]
    hp = H // HEADS_PER_STEP
    return pl.pallas_call(
        functools.partial(_attn_kernel, n_kt=n_kt),
        grid=(B, hp, S // tq),
        in_specs=[pl.BlockSpec((1, HEADS_PER_STEP, QK_PAD, tq), lambda b, h, i: (b, h, 0, i)),
                  pl.BlockSpec((1, HEADS_PER_STEP, n_kt, tk, QK_PAD), lambda b, h, i: (b, h, 0, 0, 0)),
                  pl.BlockSpec((1, HEADS_PER_STEP, n_kt, V_ROWS, tk), lambda b, h, i: (b, h, 0, 0, 0))],
        out_specs=pl.BlockSpec((1, tq, HEADS_PER_STEP * V_HEAD), lambda b, h, i: (b, i, h)),
        out_shape=jax.ShapeDtypeStruct((B, S, H * V_HEAD), BF16),
        compiler_params=_cparams(("arbitrary", "arbitrary", "arbitrary")), name="mla_attention",
    )(qT, k, vT)


def _router_epilogue(x1, g2_ref, wr_cat_ref, wr_hi_ref, x1_ref, h2_ref, aff_ref):
    x1_ref[...] = x1
    h2 = _rms_rows(x1, g2_ref[...])
    h2_hi = h2.astype(BF16)
    h2_lo = (h2 - h2_hi.astype(F32)).astype(BF16)
    h2_ref[...] = h2_hi
    both = _dot(h2_hi, wr_cat_ref[...])
    logits = both[:, :N_EXPERTS] + both[:, N_EXPERTS:] + _dot(h2_lo, wr_hi_ref[...])
    ex = jnp.exp(logits - jnp.max(logits, axis=-1, keepdims=True))
    aff_ref[...] = ex / jnp.sum(ex, axis=-1, keepdims=True)


def _outproj_kernel(o_ref, w_ref, x_ref, g2_ref, wr_cat_ref, wr_hi_ref, x1_ref, h2_ref, aff_ref):
    x1 = x_ref[...] + _dot(o_ref[...], w_ref[...])
    _router_epilogue(x1, g2_ref, wr_cat_ref, wr_hi_ref, x1_ref, h2_ref, aff_ref)


def _row_specs(n, tm):
    out_shape = [jax.ShapeDtypeStruct((n, D_MODEL), F32),
                 jax.ShapeDtypeStruct((n, D_MODEL), BF16),
                 jax.ShapeDtypeStruct((n, N_EXPERTS), F32)]
    out_specs = [pl.BlockSpec((tm, D_MODEL), lambda i: (i, 0)),
                 pl.BlockSpec((tm, D_MODEL), lambda i: (i, 0)),
                 pl.BlockSpec((tm, N_EXPERTS), lambda i: (i, 0))]
    return out_shape, out_specs


def _outproj(o, w_out, x, g2, wr_cat, wr_hi, tm):
    n = x.shape[0]
    full = lambda a: pl.BlockSpec(a.shape, lambda i: (0,) * a.ndim)
    out_shape, out_specs = _row_specs(n, tm)
    return pl.pallas_call(
        _outproj_kernel, grid=(n // tm,),
        in_specs=[pl.BlockSpec((tm, D_MODEL), lambda i: (i, 0)), full(w_out),
                  pl.BlockSpec((tm, D_MODEL), lambda i: (i, 0)), full(g2), full(wr_cat), full(wr_hi)],
        out_specs=out_specs, out_shape=out_shape,
        compiler_params=_cparams(("arbitrary",)), name="mla_outproj_router",
    )(o, w_out, x, g2, wr_cat, wr_hi)


def _sgu_kernel(x_ref, g1_ref, w_in_ref, lng_ref, lnb_ref, ws_ref, bs_ref, w_out_ref,
                g2_ref, wr_cat_ref, wr_hi_ref, x1_ref, h2_ref, aff_ref):
    x = x_ref[...]
    tm = x.shape[0]
    h = _rms_rows(x, g1_ref[...]).astype(BF16)
    z = _dot(h, w_in_ref[...])
    z = 0.5 * z * (1.0 + lax.erf(z * np.float32(2.0 ** -0.5)))
    u = z[:, :SGU_HALF]
    v = z[:, SGU_HALF:]
    mu = jnp.mean(v, axis=-1, keepdims=True)
    vc = v - mu
    v = vc * lax.rsqrt(jnp.mean(vc * vc, axis=-1, keepdims=True) + EPS) * lng_ref[...] + lnb_ref[...]
    vb = v.astype(BF16)
    bs = bs_ref[...]
    rows = []
    for c in range(tm // CHUNK):
        cols = []
        for g in range(SGU_GROUPS):
            blk = vb[c * CHUNK:(c + 1) * CHUNK, g * LANES:(g + 1) * LANES]
            cols.append(_dot(ws_ref[g], blk))
        rows.append(jnp.concatenate(cols, axis=1) + bs)
    sv = jnp.concatenate(rows, axis=0)
    gated = (u * sv).astype(BF16)
    x1 = x + _dot(gated, w_out_ref[...])
    _router_epilogue(x1, g2_ref, wr_cat_ref, wr_hi_ref, x1_ref, h2_ref, aff_ref)


def _sgu(x, g1, w_in, lng, lnb, ws, bs_full, w_out, g2, wr_cat, wr_hi, tm):
    n = x.shape[0]
    full = lambda a: pl.BlockSpec(a.shape, lambda i: (0,) * a.ndim)
    ops = (g1, w_in, lng, lnb, ws, bs_full, w_out, g2, wr_cat, wr_hi)
    out_shape, out_specs = _row_specs(n, tm)
    return pl.pallas_call(
        _sgu_kernel, grid=(n // tm,),
        in_specs=[pl.BlockSpec((tm, D_MODEL), lambda i: (i, 0))] + [full(a) for a in ops],
        out_specs=out_specs, out_shape=out_shape,
        compiler_params=_cparams(("arbitrary",)), name="sgu_router",
    )(x, *ops)


def _select_kernel(aff_ref, lp_ref, start_ref, cnt_ref, *, nt, cap):
    a = aff_ref[...]
    bits = pltpu.bitcast(a, I32)

    def body(i, cur):
        trial = cur | jnp.left_shift(jnp.int32(1), 30 - i)
        ge = jnp.where(bits >= trial, 1.0, 0.0)
        cnt = jnp.sum(jnp.sum(ge, axis=1, keepdims=True), axis=2, keepdims=True)
        return jnp.where(cnt >= cap, trial, cur)

    thr = lax.fori_loop(0, 31, body, jnp.zeros((N_EXPERTS, 1, 1), I32))
    gtf = jnp.where(bits > thr, 1.0, 0.0)
    eqf = jnp.where(bits == thr, 1.0, 0.0)
    c_gt = jnp.sum(jnp.sum(gtf, axis=1, keepdims=True), axis=2, keepdims=True)
    need = cap - c_gt

    r = lax.broadcasted_iota(I32, (LANES, LANES), 0)
    c = lax.broadcasted_iota(I32, (LANES, LANES), 1)
    incl_u = jnp.where(r <= c, 1.0, 0.0).astype(BF16)
    ones_u = jnp.ones((LANES, LANES), BF16)
    rt = lax.broadcasted_iota(I32, (nt, nt), 0)
    ct = lax.broadcasted_iota(I32, (nt, nt), 1)
    excl_l = jnp.where(rt > ct, 1.0, 0.0).astype(BF16)

    def tile_prefix(maskf):
        m2 = maskf.reshape(N_EXPERTS * nt, LANES).astype(BF16)
        incl = _dot(m2, incl_u)
        cnt = _dot(m2, ones_u)
        cnt_b = cnt.astype(BF16)
        starts = [_dot(excl_l, cnt_b[e * nt:(e + 1) * nt]) for e in range(N_EXPERTS)]
        return incl, jnp.concatenate(starts, axis=0), cnt

    eq_incl, eq_start, _ = tile_prefix(eqf)
    tie_rank = (eq_start + eq_incl).reshape(N_EXPERTS, nt, LANES) - eqf
    selm = gtf + eqf * jnp.where(tie_rank < need, 1.0, 0.0)
    incl, start, cnt = tile_prefix(selm)
    lp = jnp.where(selm.reshape(N_EXPERTS * nt, LANES) > 0.5, incl - 1.0, float(NOT_SELECTED))
    lp_ref[...] = lp.astype(I32)
    start_ref[...] = start.astype(I32)
    cnt_ref[...] = cnt.astype(I32)


def _select(aff_t3, cap):
    e, nt, _ = aff_t3.shape
    return pl.pallas_call(
        functools.partial(_select_kernel, nt=nt, cap=cap),
        out_shape=[jax.ShapeDtypeStruct((e * nt, LANES), I32)] * 3,
        compiler_params=pltpu.CompilerParams(vmem_limit_bytes=VMEM_LIMIT), name="moe_select",
    )(aff_t3)


def _pack_rows(x_f32_bf16_valued):
    bits = pltpu.bitcast(x_f32_bf16_valued, U32)
    half = bits.shape[1] // 2
    return (bits[:, half:] & jnp.uint32(0xFFFF0000)) | (bits[:, :half] >> 16)


def _unpack_rows(w):
    lo = pltpu.bitcast(w << 16, F32)
    hi = pltpu.bitcast(w & jnp.uint32(0xFFFF0000), F32)
    return jnp.concatenate([lo, hi], axis=1).astype(BF16)


def _dispatch_kernel(start_sm, cnt_sm, x_ref, lp_ref, xg_hbm, stage, carry, sems, *, nt, cap):
    t = pl.program_id(0)
    slot = t % 2

    @pl.when(t == 0)
    def _():
        carry[...] = jnp.zeros(carry.shape, carry.dtype)
        stage[1] = jnp.zeros(stage.shape[1:], stage.dtype)
        pads = [pltpu.make_async_copy(stage.at[1, e], xg_hbm.at[e, pl.ds(cap, SLAB), :], sems.at[1, e])
                for e in range(N_EXPERTS)]
        for cp in pads:
            cp.start()
        for cp in pads:
            cp.wait()

    x = x_ref[...]
    jio = lax.broadcasted_iota(I32, (SLAB, TOK_TILE), 0)
    rio = lax.broadcasted_iota(I32, (SUBLANES, D_MODEL // 2), 0)
    base = []
    for e in range(N_EXPERTS):
        start = start_sm[e * nt + t]
        row0 = (start >> 3) << 3
        shift = start - row0
        base.append(row0)
        onehot_t = jnp.where(lp_ref[0, e:e + 1, :] + shift == jio, 1.0, 0.0).astype(BF16)
        slab = _pack_rows(_dot(onehot_t, x))
        stage[slot, e, 0:SUBLANES] = jnp.where(rio < shift, carry[e], slab[0:SUBLANES])
        stage[slot, e, SUBLANES:SLAB] = slab[SUBLANES:]
        nxt = pl.multiple_of(((shift + cnt_sm[e * nt + t]) >> 3) << 3, SUBLANES)
        carry[e] = stage[slot, e, pl.ds(nxt, SUBLANES)]

    def copy(s, e, row0):
        dst = xg_hbm.at[e, pl.ds(pl.multiple_of(row0, SUBLANES), SLAB), :]
        return pltpu.make_async_copy(stage.at[s, e], dst, sems.at[s, e])

    @pl.when(t > 0)
    def _():
        for e in range(N_EXPERTS):
            copy(1 - slot, e, 0).wait()

    for e in range(N_EXPERTS):
        copy(slot, e, base[e]).start()

    @pl.when(t == nt - 1)
    def _():
        for e in range(N_EXPERTS):
            copy(slot, e, 0).wait()


def _dispatch(start_flat, cnt_flat, h2, lp3, cap):
    n = h2.shape[0]
    nt = n // TOK_TILE
    grid_spec = pltpu.PrefetchScalarGridSpec(
        num_scalar_prefetch=2, grid=(nt,),
        in_specs=[pl.BlockSpec((TOK_TILE, D_MODEL), lambda t, s, c: (t, 0)),
                  pl.BlockSpec((1, N_EXPERTS, LANES), lambda t, s, c: (t, 0, 0))],
        out_specs=pl.BlockSpec(memory_space=pl.ANY),
        scratch_shapes=[pltpu.VMEM((2, N_EXPERTS, SLAB, D_MODEL // 2), U32),
                        pltpu.VMEM((N_EXPERTS, SUBLANES, D_MODEL // 2), U32),
                        pltpu.SemaphoreType.DMA((2, N_EXPERTS))])
    return pl.pallas_call(
        functools.partial(_dispatch_kernel, nt=nt, cap=cap), grid_spec=grid_spec,
        out_shape=jax.ShapeDtypeStruct((N_EXPERTS, cap + SLAB, D_MODEL // 2), U32),
        compiler_params=_cparams(("arbitrary",)), name="moe_dispatch",
    )(start_flat, cnt_flat, h2, lp3)


def _ffn_kernel(xg_ref, wg_ref, wu_ref, wd_ref, yg_ref):
    x = _unpack_rows(xg_ref[0])
    g = _dot(x, wg_ref[0])
    u = _dot(x, wu_ref[0])
    hid = (g * jax.nn.sigmoid(g) * u).astype(BF16)
    y = _dot(hid, wd_ref[0])
    yg_ref[0] = _pack_rows(y.astype(BF16).astype(F32))


def _ffn(xg, wg, wu, wd, cap, rt):
    wspec = pl.BlockSpec((1, D_MODEL, EXPERT_FF), lambda e, r: (e, 0, 0))
    return pl.pallas_call(
        _ffn_kernel, grid=(N_EXPERTS, cap // rt),
        in_specs=[pl.BlockSpec((1, rt, D_MODEL // 2), lambda e, r: (e, r, 0)), wspec, wspec,
                  pl.BlockSpec((1, EXPERT_FF, D_MODEL), lambda e, r: (e, 0, 0))],
        out_specs=pl.BlockSpec((1, rt, D_MODEL // 2), lambda e, r: (e, r, 0)),
        out_shape=jax.ShapeDtypeStruct((N_EXPERTS, cap, D_MODEL // 2), U32),
        compiler_params=_cparams(("arbitrary", "arbitrary")), name="moe_ffn",
    )(xg, wg, wu, wd)


def _combine_kernel(start_sm, x_ref, lpt_ref, aff_ref, yg_hbm, o_ref, slabs, sems, *, nt, cap):
    t = pl.program_id(0)
    slot = t % 2

    def row0(e, tt):
        return jnp.minimum((start_sm[e * nt + tt] >> 3) << 3, cap - SLAB)

    def copy(s, e, r0):
        src = yg_hbm.at[e, pl.ds(pl.multiple_of(r0, SUBLANES), SLAB), :]
        return pltpu.make_async_copy(src, slabs.at[s, e], sems.at[s, e])

    @pl.when(t == 0)
    def _():
        for e in range(N_EXPERTS):
            copy(0, e, row0(e, 0)).start()

    @pl.when(t + 1 < nt)
    def _():
        for e in range(N_EXPERTS):
            copy(1 - slot, e, row0(e, t + 1)).start()

    for e in range(N_EXPERTS):
        copy(slot, e, 0).wait()

    acc = x_ref[...]
    lane = lax.broadcasted_iota(I32, (TOK_TILE, SLAB), 1)
    for e in range(N_EXPERTS):
        delta = start_sm[e * nt + t] - row0(e, t)
        onehot = jnp.where(lpt_ref[:, e:e + 1] + delta == lane, 1.0, 0.0).astype(BF16)
        acc = acc + aff_ref[:, e:e + 1] * _dot(onehot, _unpack_rows(slabs[slot, e]))
    o_ref[...] = acc


def _combine(start_flat, x1, lp_t, aff, yg, cap):
    n = x1.shape[0]
    nt = n // TOK_TILE
    grid_spec = pltpu.PrefetchScalarGridSpec(
        num_scalar_prefetch=1, grid=(nt,),
        in_specs=[pl.BlockSpec((TOK_TILE, D_MODEL), lambda t, s: (t, 0)),
                  pl.BlockSpec((TOK_TILE, N_EXPERTS), lambda t, s: (t, 0)),
                  pl.BlockSpec((TOK_TILE, N_EXPERTS), lambda t, s: (t, 0)),
                  pl.BlockSpec(memory_space=pl.ANY)],
        out_specs=pl.BlockSpec((TOK_TILE, D_MODEL), lambda t, s: (t, 0)),
        scratch_shapes=[pltpu.VMEM((2, N_EXPERTS, SLAB, D_MODEL // 2), U32),
                        pltpu.SemaphoreType.DMA((2, N_EXPERTS))])
    return pl.pallas_call(
        functools.partial(_combine_kernel, nt=nt, cap=cap), grid_spec=grid_spec,
        out_shape=jax.ShapeDtypeStruct((n, D_MODEL), F32),
        compiler_params=_cparams(("arbitrary",)), name="moe_combine",
    )(start_flat, x1, lp_t, aff, yg)


def _moe(x1, h2, aff, wg, wu, wd, ffn_rows):
    n = x1.shape[0]
    nt = n // TOK_TILE
    cap = max(1, CAPACITY_FACTOR * n // N_EXPERTS)
    assert cap >= SLAB and cap % SUBLANES == 0
    lp, start, cnt = _select(aff.T.reshape(N_EXPERTS, nt, LANES), cap)
    start_flat = start[:, 0]
    lp_t = lp.reshape(N_EXPERTS, n).T
    xg = _dispatch(start_flat, cnt[:, 0], h2, lp.reshape(N_EXPERTS, nt, LANES).transpose(1, 0, 2), cap)
    yg = _ffn(xg, wg, wu, wd, cap, min(ffn_rows, cap))
    return _combine(start_flat, x1, lp_t, aff, yg, cap)


def _rope_tables(S):
    pos = jnp.arange(S, dtype=F32)
    inv_freq = 1.0 / (ROPE_THETA ** (jnp.arange(0, ROPE_DIM, 2, dtype=F32) / ROPE_DIM))
    ang = pos[:, None] * inv_freq[None, :]
    return jnp.cos(ang), jnp.sin(ang)


def _prep_mla_weights(w_in, gq, gkv, w_uq, w_ukv, qn_g, kn_g):
    pad = lambda a, lo, hi: jnp.pad(a, ((0, 0), (lo, hi)))
    pe = w_in[:, Q_LORA + KV_LORA:]
    pe_sw = jnp.concatenate([pe[:, HALF_ROPE:], pe[:, :HALF_ROPE]], axis=1)
    w_in_p = jnp.concatenate([w_in[:, :Q_LORA + KV_LORA], pad(pe, NOPE_DIM, QK_PAD - QK_HEAD),
                              pad(pe_sw, NOPE_DIM, QK_PAD - QK_HEAD)], axis=1).astype(BF16)
    w_uqT = w_uq.T.astype(BF16)
    ukv = w_ukv.reshape(KV_LORA, MLA_HEADS, NOPE_DIM + V_HEAD)
    w_uk = jnp.pad(ukv[:, :, :NOPE_DIM], ((0, 0), (0, 0), (0, QK_PAD - NOPE_DIM)))
    w_uk = w_uk.reshape(KV_LORA, MLA_HEADS * QK_PAD).astype(BF16)
    w_uvT = ukv[:, :, NOPE_DIM:].reshape(KV_LORA, MLA_HEADS * V_HEAD).T.astype(BF16)
    gqc = (qn_g * np.float32(QK_HEAD ** -0.5)).reshape(QK_HEAD, 1)
    return (w_in_p, gq.reshape(1, Q_LORA), gkv.reshape(1, KV_LORA), w_uqT, w_uk, w_uvT, gqc)


def _prep_k_tables(cos, sin, kn_g):
    S = cos.shape[0]
    z = lambda w: jnp.zeros((S, w), F32)
    cos_l = jnp.concatenate([jnp.ones((S, NOPE_DIM), F32), cos, cos, z(QK_PAD - QK_HEAD)], axis=1)
    sin_l = jnp.concatenate([z(NOPE_DIM), -sin, sin, z(QK_PAD - QK_HEAD)], axis=1)
    g = jnp.pad(kn_g, (0, QK_PAD - QK_HEAD))
    g_sw = jnp.concatenate([jnp.zeros((NOPE_DIM,), F32), kn_g[NOPE_DIM + HALF_ROPE:],
                            kn_g[NOPE_DIM:NOPE_DIM + HALF_ROPE], jnp.zeros((QK_PAD - QK_HEAD,), F32)])
    return cos_l * g[None, :], sin_l * g_sw[None, :]


def _router_weights(w_r):
    hi = w_r.astype(BF16)
    lo = (w_r - hi.astype(F32)).astype(BF16)
    return jnp.concatenate([hi, lo], axis=1), hi


def _trunk(x3, p, tiles):
    B, S, _ = x3.shape
    n = B * S
    cos, sin = _rope_tables(S)
    mw = _prep_mla_weights(p['mla_w_in'][0], p['mla_q_lora_g'][0], p['mla_kv_lora_g'][0], p['mla_w_uq'][0],
                           p['mla_w_ukv'][0], p['mla_q_norm_g'][0], p['mla_k_norm_g'][0])
    kcos, ksin = _prep_k_tables(cos, sin, p['mla_k_norm_g'][0])
    qT, k, vT = _mla_prep(x3, p['norm1_g'][0].reshape(1, D_MODEL), mw, (kcos, ksin, cos.T, sin.T), tiles['tk'])
    o = _attention(qT, k, vT, tiles['tq']).reshape(n, D_MODEL)
    x = x3.reshape(n, D_MODEL)
    wr_cat, wr_hi = _router_weights(p['moe_w_router'][0])
    x1, h2, aff = _outproj(o, p['mla_w_out'][0].astype(BF16), x, p['norm2_g'][0].reshape(1, D_MODEL),
                           wr_cat, wr_hi, tiles['tm'])
    x = _moe(x1, h2, aff, p['wg'][0], p['wu'][0], p['wd'][0], tiles['ffn_rows'])
    wr_cat, wr_hi = _router_weights(p['moe_w_router'][1])
    bs_full = jnp.repeat(p['sgu_b_s'][0].T, LANES, axis=1)
    x1, h2, aff = _sgu(x, p['norm1_g'][1].reshape(1, D_MODEL), p['sgu_w_in'][0].astype(BF16),
                       p['sgu_ln_g'][0].reshape(1, SGU_HALF), p['sgu_ln_b'][0].reshape(1, SGU_HALF),
                       p['sgu_w_s'][0].astype(BF16), bs_full, p['sgu_w_out'][0].astype(BF16),
                       p['norm2_g'][1].reshape(1, D_MODEL), wr_cat, wr_hi, tiles['tm_sgu'])
    x = _moe(x1, h2, aff, p['wg'][1], p['wu'][1], p['wd'][1], tiles['ffn_rows'])
    return x.reshape(B, S, D_MODEL)


TILES = dict(tk=512, tq=256, tm=512, tm_sgu=256, ffn_rows=512)


def kernel(x_prompt, x_sample, norm1_g, norm2_g, mla_w_in, mla_q_lora_g, mla_kv_lora_g, mla_w_uq, mla_w_ukv, mla_q_norm_g, mla_k_norm_g, mla_w_out, sgu_w_in, sgu_ln_g, sgu_ln_b, sgu_w_s, sgu_b_s, sgu_w_out, moe_w_router, moe_w_gate, moe_w_up, moe_w_down):
    p = dict(norm1_g=norm1_g, norm2_g=norm2_g, mla_w_in=mla_w_in, mla_q_lora_g=mla_q_lora_g,
             mla_kv_lora_g=mla_kv_lora_g, mla_w_uq=mla_w_uq, mla_w_ukv=mla_w_ukv, mla_q_norm_g=mla_q_norm_g,
             mla_k_norm_g=mla_k_norm_g, mla_w_out=mla_w_out, sgu_w_in=sgu_w_in, sgu_ln_g=sgu_ln_g,
             sgu_ln_b=sgu_ln_b, sgu_w_s=sgu_w_s, sgu_b_s=sgu_b_s, sgu_w_out=sgu_w_out, moe_w_router=moe_w_router,
             wg=moe_w_gate.astype(BF16), wu=moe_w_up.astype(BF16), wd=moe_w_down.astype(BF16))
    return (_trunk(x_prompt, p, TILES), _trunk(x_sample, p, TILES))
```

```python
import functools

import jax
import jax.numpy as jnp
import numpy as np
from jax import lax
from jax.experimental import pallas as pl
from jax.experimental.pallas import tpu as pltpu

F32 = jnp.float32
BF16 = jnp.bfloat16
I32 = jnp.int32
U32 = jnp.uint32

D_MODEL = 1024
MLA_HEADS = 16
Q_LORA = 256
KV_LORA = 128
NOPE_DIM = 64
ROPE_DIM = 32
HALF_ROPE = ROPE_DIM // 2
QK_HEAD = NOPE_DIM + ROPE_DIM
V_HEAD = 64
ROPE_THETA = 10000.0
CHUNK = 128
SGU_HALF = D_MODEL
SGU_GROUPS = 8
N_EXPERTS = 16
EXPERT_FF = 1024
CAPACITY_FACTOR = 2
EPS = 1e-6

LANES = 128
QK_PAD = 128
V_ROWS = 80
TOK_TILE = 128
SUBLANES = 8
SLAB = TOK_TILE + 16
NOT_SELECTED = -1024
BOUND_MARGIN = 1.0 + 2.0 ** -6
MAX_SHIFTED_SCORE_RANGE = 100.0
HEADS_PER_STEP = 2
KT_PER_TRIP = 8

VMEM_LIMIT = 52 * 1024 * 1024


def _cparams(sem):
    return pltpu.CompilerParams(dimension_semantics=sem, vmem_limit_bytes=VMEM_LIMIT)


def _dot(a, b):
    return jnp.dot(a, b, preferred_element_type=F32)


def _dot_nt(a, b):
    return lax.dot_general(a, b, (((1,), (1,)), ((), ())), preferred_element_type=F32)


def _rms_rows(x, g):
    return x * lax.rsqrt(jnp.mean(x * x, axis=-1, keepdims=True) + EPS) * g


def _mla_prep_kernel(x_ref, g1_ref, w_in_ref, gq_ref, gkv_ref, w_uqT_ref, w_uk_ref, w_uvT_ref,
                     gqc_ref, kb_ref, kcos_ref, ksin_ref, cosT_ref, sinT_ref,
                     qT_ref, k_ref, vT_ref):
    x = x_ref[...]
    h = _rms_rows(x, g1_ref[...]).astype(BF16)
    proj = _dot(h, w_in_ref[...])
    cq = _rms_rows(proj[:, :Q_LORA], gq_ref[...]).astype(BF16)
    ckv = _rms_rows(proj[:, Q_LORA:Q_LORA + KV_LORA], gkv_ref[...]).astype(BF16)
    kpe = proj[:, 384:512]
    kpe_sw = proj[:, 512:640]
    tm = x.shape[0]

    qT = _dot_nt(w_uqT_ref[...], cq)
    cosT = cosT_ref[...]
    sinT = sinT_ref[...]
    gqc = gqc_ref[...]
    kbound = kb_ref[...]
    brow = lax.broadcasted_iota(I32, (QK_PAD - QK_HEAD, tm), 0)
    for hd in range(MLA_HEADS):
        qh = qT[hd * QK_HEAD:(hd + 1) * QK_HEAD, :]
        r = lax.rsqrt(jnp.sum(qh * qh, axis=0, keepdims=True) * (1.0 / QK_HEAD) + EPS)
        qn = qh * gqc * r
        x1 = qn[NOPE_DIM:NOPE_DIM + HALF_ROPE]
        x2 = qn[NOPE_DIM + HALF_ROPE:QK_HEAD]
        qb = jnp.concatenate([qn[:NOPE_DIM], x1 * cosT - x2 * sinT, x2 * cosT + x1 * sinT], axis=0).astype(BF16)
        qT_ref[0, hd, 0:QK_HEAD, :] = qb
        qf = qb.astype(F32)
        bound = jnp.sqrt(jnp.sum(qf * qf, axis=0, keepdims=True)) * kbound
        qT_ref[0, hd, QK_HEAD:QK_PAD, :] = jnp.where(brow == 0, -bound, 0.0).astype(BF16)

    knope = _dot(ckv, w_uk_ref[...])
    kcos = kcos_ref[...]
    ksin = ksin_ref[...]
    one_lane = lax.broadcasted_iota(I32, (tm, QK_PAD), 1) == QK_HEAD
    for hd in range(MLA_HEADS):
        kh = knope[:, hd * QK_PAD:(hd + 1) * QK_PAD] + kpe
        r = lax.rsqrt(jnp.sum(kh * kh, axis=-1, keepdims=True) * (1.0 / QK_HEAD) + EPS)
        k_ref[0, hd, 0] = jnp.where(one_lane, 1.0, (kh * kcos + kpe_sw * ksin) * r).astype(BF16)

    vT = _dot_nt(w_uvT_ref[...], ckv)
    row = lax.broadcasted_iota(I32, (V_ROWS - V_HEAD, tm), 0)
    tail = jnp.where(row == 0, 1.0, 0.0).astype(BF16)
    for hd in range(MLA_HEADS):
        vT_ref[0, hd, 0, 0:V_HEAD, :] = vT[hd * V_HEAD:(hd + 1) * V_HEAD, :].astype(BF16)
        vT_ref[0, hd, 0, V_HEAD:V_ROWS, :] = tail


def _mla_prep(x3, g1, wts, tabs, tm):
    B, S, _ = x3.shape
    nb = S // tm
    full = lambda a: pl.BlockSpec(a.shape, lambda b, i: (0,) * a.ndim)
    in_specs = [pl.BlockSpec((None, tm, D_MODEL), lambda b, i: (b, i, 0)), full(g1)]
    in_specs += [full(w) for w in wts]
    in_specs += [pl.BlockSpec((tm, LANES), lambda b, i: (i, 0)),
                 pl.BlockSpec((tm, LANES), lambda b, i: (i, 0)),
                 pl.BlockSpec((HALF_ROPE, tm), lambda b, i: (0, i)),
                 pl.BlockSpec((HALF_ROPE, tm), lambda b, i: (0, i))]
    out_shape = [jax.ShapeDtypeStruct((B, MLA_HEADS, QK_PAD, S), BF16),
                 jax.ShapeDtypeStruct((B, MLA_HEADS, nb, tm, QK_PAD), BF16),
                 jax.ShapeDtypeStruct((B, MLA_HEADS, nb, V_ROWS, tm), BF16)]
    out_specs = [pl.BlockSpec((1, MLA_HEADS, QK_PAD, tm), lambda b, i: (b, 0, 0, i)),
                 pl.BlockSpec((1, MLA_HEADS, 1, tm, QK_PAD), lambda b, i: (b, 0, i, 0, 0)),
                 pl.BlockSpec((1, MLA_HEADS, 1, V_ROWS, tm), lambda b, i: (b, 0, i, 0, 0))]
    return pl.pallas_call(
        _mla_prep_kernel, grid=(B, nb), in_specs=in_specs, out_specs=out_specs, out_shape=out_shape,
        compiler_params=_cparams(("arbitrary", "arbitrary")), name="mla_prep",
    )(x3, g1, *wts, *tabs)


def _attn_kernel(qT_ref, k_ref, vT_ref, o_ref, *, n_kt):
    tq = qT_ref.shape[3]
    qTs = [qT_ref[0, hh] for hh in range(HEADS_PER_STEP)]

    def body(kt, carry):
        out = []
        for hh in range(HEADS_PER_STEP):
            m, acc = carry[hh]
            sT = _dot(k_ref[0, hh, kt], qTs[hh])
            m_new = jnp.maximum(m, jnp.max(sT, axis=0, keepdims=True))
            alpha = jnp.exp2(m - m_new)
            pT = jnp.exp2(sT - m_new).astype(BF16)
            acc = alpha * acc + _dot(vT_ref[0, hh, kt], pT)
            out.append((m_new, acc))
        return tuple(out)

    init = tuple((jnp.full((1, tq), -jnp.inf, F32), jnp.zeros((V_ROWS, tq), F32))
                 for _ in range(HEADS_PER_STEP))
    res = lax.fori_loop(0, n_kt, body, init, unroll=2)
    outs = [acc[:V_HEAD] / acc[V_HEAD:V_HEAD + 1] for _, acc in res]
    o_ref[0] = jnp.concatenate(outs, axis=0).T.astype(o_ref.dtype)


def _attn_bounded_kernel(qT_ref, k_ref, vT_ref, o_ref, *, n_kt):
    tq = qT_ref.shape[3]
    qTs = [qT_ref[0, hh] for hh in range(HEADS_PER_STEP)]

    def scores(kt):
        return [_dot(k_ref[0, hh, kt], qTs[hh]) for hh in range(HEADS_PER_STEP)]

    def accumulate(accs, kt, sTs):
        return tuple(accs[hh] + _dot(vT_ref[0, hh, kt], jnp.exp2(sTs[hh]).astype(BF16))
                     for hh in range(HEADS_PER_STEP))

    def body(j, accs):
        kt = j * KT_PER_TRIP
        sTs = scores(kt)
        for u in range(1, KT_PER_TRIP):
            nxt = scores(kt + u)
            accs = accumulate(accs, kt + u - 1, sTs)
            sTs = nxt
        return accumulate(accs, kt + KT_PER_TRIP - 1, sTs)

    assert n_kt % KT_PER_TRIP == 0
    accs = lax.fori_loop(0, n_kt // KT_PER_TRIP, body,
                         tuple(jnp.zeros((V_ROWS, tq), F32) for _ in range(HEADS_PER_STEP)))
    outs = [acc[:V_HEAD] / acc[V_HEAD:V_HEAD + 1] for acc in accs]
    o_ref[0] = jnp.concatenate(outs, axis=0).T.astype(o_ref.dtype)


def _attention(qT, k, vT, tq, bounded):
    B, H, _, S = qT.shape
    n_kt, tk = k.shape[2], k.shape[3]
    hp = H // HEADS_PER_STEP
    return pl.pallas_call(
        functools.partial(_attn_bounded_kernel if bounded else _attn_kernel, n_kt=n_kt),
        grid=(B, hp, S // tq),
        in_specs=[pl.BlockSpec((1, HEADS_PER_STEP, QK_PAD, tq), lambda b, h, i: (b, h, 0, i)),
                  pl.BlockSpec((1, HEADS_PER_STEP, n_kt, tk, QK_PAD), lambda b, h, i: (b, h, 0, 0, 0)),
                  pl.BlockSpec((1, HEADS_PER_STEP, n_kt, V_ROWS, tk), lambda b, h, i: (b, h, 0, 0, 0))],
        out_specs=pl.BlockSpec((1, tq, HEADS_PER_STEP * V_HEAD), lambda b, h, i: (b, i, h)),
        out_shape=jax.ShapeDtypeStruct((B, S, H * V_HEAD), BF16),
        compiler_params=_cparams(("arbitrary", "arbitrary", "arbitrary")),
        name="mla_attention_bounded" if bounded else "mla_attention",
    )(qT, k, vT)


def _router_epilogue(x1, g2_ref, wr_cat_ref, wr_hi_ref, x1_ref, h2_ref, aff_ref):
    x1_ref[...] = x1
    h2 = _rms_rows(x1, g2_ref[...])
    h2_hi = h2.astype(BF16)
    h2_lo = (h2 - h2_hi.astype(F32)).astype(BF16)
    h2_ref[...] = h2_hi
    both = _dot(h2_hi, wr_cat_ref[...])
    logits = both[:, :N_EXPERTS] + both[:, N_EXPERTS:] + _dot(h2_lo, wr_hi_ref[...])
    ex = jnp.exp(logits - jnp.max(logits, axis=-1, keepdims=True))
    aff_ref[...] = ex / jnp.sum(ex, axis=-1, keepdims=True)


def _outproj_kernel(o_ref, w_ref, x_ref, g2_ref, wr_cat_ref, wr_hi_ref, x1_ref, h2_ref, aff_ref):
    x1 = x_ref[...] + _dot(o_ref[...], w_ref[...])
    _router_epilogue(x1, g2_ref, wr_cat_ref, wr_hi_ref, x1_ref, h2_ref, aff_ref)


def _row_specs(n, tm):
    out_shape = [jax.ShapeDtypeStruct((n, D_MODEL), F32),
                 jax.ShapeDtypeStruct((n, D_MODEL), BF16),
                 jax.ShapeDtypeStruct((n, N_EXPERTS), F32)]
    out_specs = [pl.BlockSpec((tm, D_MODEL), lambda i: (i, 0)),
                 pl.BlockSpec((tm, D_MODEL), lambda i: (i, 0)),
                 pl.BlockSpec((tm, N_EXPERTS), lambda i: (i, 0))]
    return out_shape, out_specs


def _outproj(o, w_out, x, g2, wr_cat, wr_hi, tm):
    n = x.shape[0]
    full = lambda a: pl.BlockSpec(a.shape, lambda i: (0,) * a.ndim)
    out_shape, out_specs = _row_specs(n, tm)
    return pl.pallas_call(
        _outproj_kernel, grid=(n // tm,),
        in_specs=[pl.BlockSpec((tm, D_MODEL), lambda i: (i, 0)), full(w_out),
                  pl.BlockSpec((tm, D_MODEL), lambda i: (i, 0)), full(g2), full(wr_cat), full(wr_hi)],
        out_specs=out_specs, out_shape=out_shape,
        compiler_params=_cparams(("arbitrary",)), name="mla_outproj_router",
    )(o, w_out, x, g2, wr_cat, wr_hi)


def _sgu_kernel(x_ref, g1_ref, w_in_ref, lng_ref, lnb_ref, ws_ref, bs_ref, w_out_ref,
                g2_ref, wr_cat_ref, wr_hi_ref, x1_ref, h2_ref, aff_ref):
    x = x_ref[...]
    tm = x.shape[0]
    h = _rms_rows(x, g1_ref[...]).astype(BF16)
    z = _dot(h, w_in_ref[...])
    z = 0.5 * z * (1.0 + lax.erf(z * np.float32(2.0 ** -0.5)))
    u = z[:, :SGU_HALF]
    v = z[:, SGU_HALF:]
    mu = jnp.mean(v, axis=-1, keepdims=True)
    vc = v - mu
    v = vc * lax.rsqrt(jnp.mean(vc * vc, axis=-1, keepdims=True) + EPS) * lng_ref[...] + lnb_ref[...]
    vb = v.astype(BF16)
    bs = bs_ref[...]
    rows = []
    for c in range(tm // CHUNK):
        cols = []
        for g in range(SGU_GROUPS):
            blk = vb[c * CHUNK:(c + 1) * CHUNK, g * LANES:(g + 1) * LANES]
            cols.append(_dot(ws_ref[g], blk))
        rows.append(jnp.concatenate(cols, axis=1) + bs)
    sv = jnp.concatenate(rows, axis=0)
    gated = (u * sv).astype(BF16)
    x1 = x + _dot(gated, w_out_ref[...])
    _router_epilogue(x1, g2_ref, wr_cat_ref, wr_hi_ref, x1_ref, h2_ref, aff_ref)


def _sgu(x, g1, w_in, lng, lnb, ws, bs_full, w_out, g2, wr_cat, wr_hi, tm):
    n = x.shape[0]
    full = lambda a: pl.BlockSpec(a.shape, lambda i: (0,) * a.ndim)
    ops = (g1, w_in, lng, lnb, ws, bs_full, w_out, g2, wr_cat, wr_hi)
    out_shape, out_specs = _row_specs(n, tm)
    return pl.pallas_call(
        _sgu_kernel, grid=(n // tm,),
        in_specs=[pl.BlockSpec((tm, D_MODEL), lambda i: (i, 0))] + [full(a) for a in ops],
        out_specs=out_specs, out_shape=out_shape,
        compiler_params=_cparams(("arbitrary",)), name="sgu_router",
    )(x, *ops)


def _select_kernel(aff_ref, lp_ref, start_ref, cnt_ref, *, nt, cap):
    a = aff_ref[...]
    bits = pltpu.bitcast(a, I32)

    def body(i, cur):
        trial = cur | jnp.left_shift(jnp.int32(1), 30 - i)
        ge = jnp.where(bits >= trial, 1.0, 0.0)
        cnt = jnp.sum(jnp.sum(ge, axis=1, keepdims=True), axis=2, keepdims=True)
        return jnp.where(cnt >= cap, trial, cur)

    thr = lax.fori_loop(0, 31, body, jnp.zeros((N_EXPERTS, 1, 1), I32))
    gtf = jnp.where(bits > thr, 1.0, 0.0)
    eqf = jnp.where(bits == thr, 1.0, 0.0)
    c_gt = jnp.sum(jnp.sum(gtf, axis=1, keepdims=True), axis=2, keepdims=True)
    need = cap - c_gt

    r = lax.broadcasted_iota(I32, (LANES, LANES), 0)
    c = lax.broadcasted_iota(I32, (LANES, LANES), 1)
    incl_u = jnp.where(r <= c, 1.0, 0.0).astype(BF16)
    ones_u = jnp.ones((LANES, LANES), BF16)
    rt = lax.broadcasted_iota(I32, (nt, nt), 0)
    ct = lax.broadcasted_iota(I32, (nt, nt), 1)
    excl_l = jnp.where(rt > ct, 1.0, 0.0).astype(BF16)

    def tile_prefix(maskf):
        m2 = maskf.reshape(N_EXPERTS * nt, LANES).astype(BF16)
        incl = _dot(m2, incl_u)
        cnt = _dot(m2, ones_u)
        cnt_b = cnt.astype(BF16)
        starts = [_dot(excl_l, cnt_b[e * nt:(e + 1) * nt]) for e in range(N_EXPERTS)]
        return incl, jnp.concatenate(starts, axis=0), cnt

    eq_incl, eq_start, _ = tile_prefix(eqf)
    tie_rank = (eq_start + eq_incl).reshape(N_EXPERTS, nt, LANES) - eqf
    selm = gtf + eqf * jnp.where(tie_rank < need, 1.0, 0.0)
    incl, start, cnt = tile_prefix(selm)
    lp = jnp.where(selm.reshape(N_EXPERTS * nt, LANES) > 0.5, incl - 1.0, float(NOT_SELECTED))
    lp_ref[...] = lp.astype(I32)
    start_ref[...] = start.astype(I32)
    cnt_ref[...] = cnt.astype(I32)


def _select(aff_t3, cap):
    e, nt, _ = aff_t3.shape
    return pl.pallas_call(
        functools.partial(_select_kernel, nt=nt, cap=cap),
        out_shape=[jax.ShapeDtypeStruct((e * nt, LANES), I32)] * 3,
        compiler_params=pltpu.CompilerParams(vmem_limit_bytes=VMEM_LIMIT), name="moe_select",
    )(aff_t3)


def _pack_rows(x_f32_bf16_valued):
    bits = pltpu.bitcast(x_f32_bf16_valued, U32)
    half = bits.shape[1] // 2
    return (bits[:, half:] & jnp.uint32(0xFFFF0000)) | (bits[:, :half] >> 16)


def _unpack_rows(w):
    lo = pltpu.bitcast(w << 16, F32)
    hi = pltpu.bitcast(w & jnp.uint32(0xFFFF0000), F32)
    return jnp.concatenate([lo, hi], axis=1).astype(BF16)


def _dispatch_kernel(start_sm, cnt_sm, x_ref, lp_ref, xg_hbm, stage, carry, sems, *, nt, cap):
    t = pl.program_id(0)
    slot = t % 2

    @pl.when(t == 0)
    def _():
        carry[...] = jnp.zeros(carry.shape, carry.dtype)
        stage[1] = jnp.zeros(stage.shape[1:], stage.dtype)
        pads = [pltpu.make_async_copy(stage.at[1, e], xg_hbm.at[e, pl.ds(cap, SLAB), :], sems.at[1, e])
                for e in range(N_EXPERTS)]
        for cp in pads:
            cp.start()
        for cp in pads:
            cp.wait()

    x = x_ref[...]
    jio = lax.broadcasted_iota(I32, (SLAB, TOK_TILE), 0)
    rio = lax.broadcasted_iota(I32, (SUBLANES, D_MODEL // 2), 0)
    base = []
    for e in range(N_EXPERTS):
        start = start_sm[e * nt + t]
        row0 = (start >> 3) << 3
        shift = start - row0
        base.append(row0)
        onehot_t = jnp.where(lp_ref[0, e:e + 1, :] + shift == jio, 1.0, 0.0).astype(BF16)
        slab = _pack_rows(_dot(onehot_t, x))
        stage[slot, e, 0:SUBLANES] = jnp.where(rio < shift, carry[e], slab[0:SUBLANES])
        stage[slot, e, SUBLANES:SLAB] = slab[SUBLANES:]
        nxt = pl.multiple_of(((shift + cnt_sm[e * nt + t]) >> 3) << 3, SUBLANES)
        carry[e] = stage[slot, e, pl.ds(nxt, SUBLANES)]

    def copy(s, e, row0):
        dst = xg_hbm.at[e, pl.ds(pl.multiple_of(row0, SUBLANES), SLAB), :]
        return pltpu.make_async_copy(stage.at[s, e], dst, sems.at[s, e])

    @pl.when(t > 0)
    def _():
        for e in range(N_EXPERTS):
            copy(1 - slot, e, 0).wait()

    for e in range(N_EXPERTS):
        copy(slot, e, base[e]).start()

    @pl.when(t == nt - 1)
    def _():
        for e in range(N_EXPERTS):
            copy(slot, e, 0).wait()


def _dispatch(start_flat, cnt_flat, h2, lp3, cap):
    n = h2.shape[0]
    nt = n // TOK_TILE
    grid_spec = pltpu.PrefetchScalarGridSpec(
        num_scalar_prefetch=2, grid=(nt,),
        in_specs=[pl.BlockSpec((TOK_TILE, D_MODEL), lambda t, s, c: (t, 0)),
                  pl.BlockSpec((1, N_EXPERTS, LANES), lambda t, s, c: (t, 0, 0))],
        out_specs=pl.BlockSpec(memory_space=pl.ANY),
        scratch_shapes=[pltpu.VMEM((2, N_EXPERTS, SLAB, D_MODEL // 2), U32),
                        pltpu.VMEM((N_EXPERTS, SUBLANES, D_MODEL // 2), U32),
                        pltpu.SemaphoreType.DMA((2, N_EXPERTS))])
    return pl.pallas_call(
        functools.partial(_dispatch_kernel, nt=nt, cap=cap), grid_spec=grid_spec,
        out_shape=jax.ShapeDtypeStruct((N_EXPERTS, cap + SLAB, D_MODEL // 2), U32),
        compiler_params=_cparams(("arbitrary",)), name="moe_dispatch",
    )(start_flat, cnt_flat, h2, lp3)


def _ffn_kernel(xg_ref, wg_ref, wu_ref, wd_ref, yg_ref):
    x = _unpack_rows(xg_ref[0])
    g = _dot(x, wg_ref[0])
    u = _dot(x, wu_ref[0])
    hid = (g * jax.nn.sigmoid(g) * u).astype(BF16)
    y = _dot(hid, wd_ref[0])
    yg_ref[0] = _pack_rows(y.astype(BF16).astype(F32))


def _ffn(xg, wg, wu, wd, cap, rt):
    wspec = pl.BlockSpec((1, D_MODEL, EXPERT_FF), lambda e, r: (e, 0, 0))
    return pl.pallas_call(
        _ffn_kernel, grid=(N_EXPERTS, cap // rt),
        in_specs=[pl.BlockSpec((1, rt, D_MODEL // 2), lambda e, r: (e, r, 0)), wspec, wspec,
                  pl.BlockSpec((1, EXPERT_FF, D_MODEL), lambda e, r: (e, 0, 0))],
        out_specs=pl.BlockSpec((1, rt, D_MODEL // 2), lambda e, r: (e, r, 0)),
        out_shape=jax.ShapeDtypeStruct((N_EXPERTS, cap, D_MODEL // 2), U32),
        compiler_params=_cparams(("arbitrary", "arbitrary")), name="moe_ffn",
    )(xg, wg, wu, wd)


def _combine_kernel(start_sm, x_ref, lpt_ref, aff_ref, yg_hbm, o_ref, slabs, sems, *, nt, cap):
    t = pl.program_id(0)
    slot = t % 2

    def row0(e, tt):
        return jnp.minimum((start_sm[e * nt + tt] >> 3) << 3, cap - SLAB)

    def copy(s, e, r0):
        src = yg_hbm.at[e, pl.ds(pl.multiple_of(r0, SUBLANES), SLAB), :]
        return pltpu.make_async_copy(src, slabs.at[s, e], sems.at[s, e])

    @pl.when(t == 0)
    def _():
        for e in range(N_EXPERTS):
            copy(0, e, row0(e, 0)).start()

    @pl.when(t + 1 < nt)
    def _():
        for e in range(N_EXPERTS):
            copy(1 - slot, e, row0(e, t + 1)).start()

    for e in range(N_EXPERTS):
        copy(slot, e, 0).wait()

    acc = x_ref[...]
    lane = lax.broadcasted_iota(I32, (TOK_TILE, SLAB), 1)
    for e in range(N_EXPERTS):
        delta = start_sm[e * nt + t] - row0(e, t)
        onehot = jnp.where(lpt_ref[:, e:e + 1] + delta == lane, 1.0, 0.0).astype(BF16)
        acc = acc + aff_ref[:, e:e + 1] * _dot(onehot, _unpack_rows(slabs[slot, e]))
    o_ref[...] = acc


def _combine(start_flat, x1, lp_t, aff, yg, cap):
    n = x1.shape[0]
    nt = n // TOK_TILE
    grid_spec = pltpu.PrefetchScalarGridSpec(
        num_scalar_prefetch=1, grid=(nt,),
        in_specs=[pl.BlockSpec((TOK_TILE, D_MODEL), lambda t, s: (t, 0)),
                  pl.BlockSpec((TOK_TILE, N_EXPERTS), lambda t, s: (t, 0)),
                  pl.BlockSpec((TOK_TILE, N_EXPERTS), lambda t, s: (t, 0)),
                  pl.BlockSpec(memory_space=pl.ANY)],
        out_specs=pl.BlockSpec((TOK_TILE, D_MODEL), lambda t, s: (t, 0)),
        scratch_shapes=[pltpu.VMEM((2, N_EXPERTS, SLAB, D_MODEL // 2), U32),
                        pltpu.SemaphoreType.DMA((2, N_EXPERTS))])
    return pl.pallas_call(
        functools.partial(_combine_kernel, nt=nt, cap=cap), grid_spec=grid_spec,
        out_shape=jax.ShapeDtypeStruct((n, D_MODEL), F32),
        compiler_params=_cparams(("arbitrary",)), name="moe_combine",
    )(start_flat, x1, lp_t, aff, yg)


def _moe(x1, h2, aff, wg, wu, wd, ffn_rows):
    n = x1.shape[0]
    nt = n // TOK_TILE
    cap = max(1, CAPACITY_FACTOR * n // N_EXPERTS)
    assert cap >= SLAB and cap % SUBLANES == 0
    lp, start, cnt = _select(aff.T.reshape(N_EXPERTS, nt, LANES), cap)
    start_flat = start[:, 0]
    lp_t = lp.reshape(N_EXPERTS, n).T
    xg = _dispatch(start_flat, cnt[:, 0], h2, lp.reshape(N_EXPERTS, nt, LANES).transpose(1, 0, 2), cap)
    yg = _ffn(xg, wg, wu, wd, cap, min(ffn_rows, cap))
    return _combine(start_flat, x1, lp_t, aff, yg, cap)


def _rope_tables(S):
    pos = jnp.arange(S, dtype=F32)
    inv_freq = 1.0 / (ROPE_THETA ** (jnp.arange(0, ROPE_DIM, 2, dtype=F32) / ROPE_DIM))
    ang = pos[:, None] * inv_freq[None, :]
    return jnp.cos(ang), jnp.sin(ang)


def _prep_mla_weights(w_in, gq, gkv, w_uq, w_ukv, qn_g, kn_g):
    pad = lambda a, lo, hi: jnp.pad(a, ((0, 0), (lo, hi)))
    pe = w_in[:, Q_LORA + KV_LORA:]
    pe_sw = jnp.concatenate([pe[:, HALF_ROPE:], pe[:, :HALF_ROPE]], axis=1)
    w_in_p = jnp.concatenate([w_in[:, :Q_LORA + KV_LORA], pad(pe, NOPE_DIM, QK_PAD - QK_HEAD),
                              pad(pe_sw, NOPE_DIM, QK_PAD - QK_HEAD)], axis=1).astype(BF16)
    w_uqT = w_uq.T.astype(BF16)
    ukv = w_ukv.reshape(KV_LORA, MLA_HEADS, NOPE_DIM + V_HEAD)
    w_uk = jnp.pad(ukv[:, :, :NOPE_DIM], ((0, 0), (0, 0), (0, QK_PAD - NOPE_DIM)))
    w_uk = w_uk.reshape(KV_LORA, MLA_HEADS * QK_PAD).astype(BF16)
    w_uvT = ukv[:, :, NOPE_DIM:].reshape(KV_LORA, MLA_HEADS * V_HEAD).T.astype(BF16)
    gqc = (qn_g * np.float32(QK_HEAD ** -0.5 * np.log2(np.e))).reshape(QK_HEAD, 1)
    kb = (jnp.max(jnp.abs(kn_g)) * np.float32(QK_HEAD ** 0.5 * BOUND_MARGIN)).reshape(1, 1)
    return (w_in_p, gq.reshape(1, Q_LORA), gkv.reshape(1, KV_LORA), w_uqT, w_uk, w_uvT, gqc, kb)


def _score_bound_log2(qn_g, kn_g):
    return (jnp.max(jnp.abs(qn_g)) * jnp.max(jnp.abs(kn_g))
            * np.float32(QK_HEAD * QK_HEAD ** -0.5 * np.log2(np.e) * BOUND_MARGIN * BOUND_MARGIN))


def _prep_k_tables(cos, sin, kn_g):
    S = cos.shape[0]
    z = lambda w: jnp.zeros((S, w), F32)
    cos_l = jnp.concatenate([jnp.ones((S, NOPE_DIM), F32), cos, cos, z(QK_PAD - QK_HEAD)], axis=1)
    sin_l = jnp.concatenate([z(NOPE_DIM), -sin, sin, z(QK_PAD - QK_HEAD)], axis=1)
    g = jnp.pad(kn_g, (0, QK_PAD - QK_HEAD))
    g_sw = jnp.concatenate([jnp.zeros((NOPE_DIM,), F32), kn_g[NOPE_DIM + HALF_ROPE:],
                            kn_g[NOPE_DIM:NOPE_DIM + HALF_ROPE], jnp.zeros((QK_PAD - QK_HEAD,), F32)])
    return cos_l * g[None, :], sin_l * g_sw[None, :]


def _router_weights(w_r):
    hi = w_r.astype(BF16)
    lo = (w_r - hi.astype(F32)).astype(BF16)
    return jnp.concatenate([hi, lo], axis=1), hi


def _trunk(x3, p, tiles):
    B, S, _ = x3.shape
    n = B * S
    cos, sin = _rope_tables(S)
    mw = _prep_mla_weights(p['mla_w_in'][0], p['mla_q_lora_g'][0], p['mla_kv_lora_g'][0], p['mla_w_uq'][0],
                           p['mla_w_ukv'][0], p['mla_q_norm_g'][0], p['mla_k_norm_g'][0])
    kcos, ksin = _prep_k_tables(cos, sin, p['mla_k_norm_g'][0])
    qT, k, vT = _mla_prep(x3, p['norm1_g'][0].reshape(1, D_MODEL), mw, (kcos, ksin, cos.T, sin.T), tiles['tk'])
    fits = 2.0 * _score_bound_log2(p['mla_q_norm_g'][0], p['mla_k_norm_g'][0]) <= MAX_SHIFTED_SCORE_RANGE
    o = lax.cond(fits, lambda: _attention(qT, k, vT, tiles['tq'], True),
                 lambda: _attention(qT, k, vT, tiles['tq'], False)).reshape(n, D_MODEL)
    x = x3.reshape(n, D_MODEL)
    wr_cat, wr_hi = _router_weights(p['moe_w_router'][0])
    x1, h2, aff = _outproj(o, p['mla_w_out'][0].astype(BF16), x, p['norm2_g'][0].reshape(1, D_MODEL),
                           wr_cat, wr_hi, tiles['tm'])
    x = _moe(x1, h2, aff, p['wg'][0], p['wu'][0], p['wd'][0], tiles['ffn_rows'])
    wr_cat, wr_hi = _router_weights(p['moe_w_router'][1])
    bs_full = jnp.repeat(p['sgu_b_s'][0].T, LANES, axis=1)
    x1, h2, aff = _sgu(x, p['norm1_g'][1].reshape(1, D_MODEL), p['sgu_w_in'][0].astype(BF16),
                       p['sgu_ln_g'][0].reshape(1, SGU_HALF), p['sgu_ln_b'][0].reshape(1, SGU_HALF),
                       p['sgu_w_s'][0].astype(BF16), bs_full, p['sgu_w_out'][0].astype(BF16),
                       p['norm2_g'][1].reshape(1, D_MODEL), wr_cat, wr_hi, tiles['tm_sgu'])
    x = _moe(x1, h2, aff, p['wg'][1], p['wu'][1], p['wd'][1], tiles['ffn_rows'])
    return x.reshape(B, S, D_MODEL)


TILES = dict(tk=512, tq=256, tm=512, tm_sgu=256, ffn_rows=512)


def kernel(x_prompt, x_sample, norm1_g, norm2_g, mla_w_in, mla_q_lora_g, mla_kv_lora_g, mla_w_uq, mla_w_ukv, mla_q_norm_g, mla_k_norm_g, mla_w_out, sgu_w_in, sgu_ln_g, sgu_ln_b, sgu_w_s, sgu_b_s, sgu_w_out, moe_w_router, moe_w_gate, moe_w_up, moe_w_down):
    p = dict(norm1_g=norm1_g, norm2_g=norm2_g, mla_w_in=mla_w_in, mla_q_lora_g=mla_q_lora_g,
             mla_kv_lora_g=mla_kv_lora_g, mla_w_uq=mla_w_uq, mla_w_ukv=mla_w_ukv, mla_q_norm_g=mla_q_norm_g,
             mla_k_norm_g=mla_k_norm_g, mla_w_out=mla_w_out, sgu_w_in=sgu_w_in, sgu_ln_g=sgu_ln_g,
             sgu_ln_b=sgu_ln_b, sgu_w_s=sgu_w_s, sgu_b_s=sgu_b_s, sgu_w_out=sgu_w_out, moe_w_router=moe_w_router,
             wg=moe_w_gate.astype(BF16), wu=moe_w_up.astype(BF16), wd=moe_w_down.astype(BF16))
    return (_trunk(x_prompt, p, TILES), _trunk(x_sample, p, TILES))
```

```python
import functools

import jax
import jax.numpy as jnp
import numpy as np
from jax import lax
from jax.experimental import pallas as pl
from jax.experimental.pallas import tpu as pltpu

F32 = jnp.float32
BF16 = jnp.bfloat16
I32 = jnp.int32

D_MODEL = 1024
MLA_HEADS = 16
Q_LORA = 256
KV_LORA = 128
NOPE_DIM = 64
ROPE_DIM = 32
HALF_ROPE = ROPE_DIM // 2
QK_HEAD = NOPE_DIM + ROPE_DIM
V_HEAD = 64
ROPE_THETA = 10000.0
CHUNK = 128
SGU_HALF = D_MODEL
SGU_GROUPS = 8
N_EXPERTS = 16
EXPERT_FF = 1024
CAPACITY_FACTOR = 2
EPS = 1e-6

LANES = 128
QK_PAD = 128
V_ROWS = 80
TOK_TILE = 128
ROW_ALIGN = 16
ROW_ALIGN_SHIFT = 4
SLAB = TOK_TILE + 16
SHORT_SLAB = 64
NOT_SELECTED = -1024
BOUND_MARGIN = 1.0 + 2.0 ** -6
MAX_SHIFTED_SCORE_RANGE = 100.0
HEADS_PER_STEP = 2
KT_PER_TRIP = 8

VMEM_LIMIT = 52 * 1024 * 1024


def _cparams(sem):
    return pltpu.CompilerParams(dimension_semantics=sem, vmem_limit_bytes=VMEM_LIMIT)


def _dot(a, b):
    return jnp.dot(a, b, preferred_element_type=F32)


def _dot_nt(a, b):
    return lax.dot_general(a, b, (((1,), (1,)), ((), ())), preferred_element_type=F32)


def _rms_rows(x, g):
    return x * lax.rsqrt(jnp.mean(x * x, axis=-1, keepdims=True) + EPS) * g


def _mla_prep_kernel(x_ref, g1_ref, w_in_ref, gq_ref, gkv_ref, w_uqT_ref, w_uk_ref, w_uvT_ref,
                     gqc_ref, kb_ref, kcos_ref, ksin_ref, cosT_ref, sinT_ref,
                     qT_ref, k_ref, vT_ref):
    x = x_ref[...]
    h = _rms_rows(x, g1_ref[...]).astype(BF16)
    proj = _dot(h, w_in_ref[...])
    cq = _rms_rows(proj[:, :Q_LORA], gq_ref[...]).astype(BF16)
    ckv = _rms_rows(proj[:, Q_LORA:Q_LORA + KV_LORA], gkv_ref[...]).astype(BF16)
    kpe = proj[:, 384:512]
    kpe_sw = proj[:, 512:640]
    tm = x.shape[0]

    qT = _dot_nt(w_uqT_ref[...], cq)
    cosT = cosT_ref[...]
    sinT = sinT_ref[...]
    gqc = gqc_ref[...]
    kbound = kb_ref[...]
    brow = lax.broadcasted_iota(I32, (QK_PAD - QK_HEAD, tm), 0)
    for hd in range(MLA_HEADS):
        qh = qT[hd * QK_HEAD:(hd + 1) * QK_HEAD, :]
        r = lax.rsqrt(jnp.sum(qh * qh, axis=0, keepdims=True) * (1.0 / QK_HEAD) + EPS)
        qn = qh * gqc * r
        x1 = qn[NOPE_DIM:NOPE_DIM + HALF_ROPE]
        x2 = qn[NOPE_DIM + HALF_ROPE:QK_HEAD]
        qb = jnp.concatenate([qn[:NOPE_DIM], x1 * cosT - x2 * sinT, x2 * cosT + x1 * sinT], axis=0).astype(BF16)
        qT_ref[0, hd, 0:QK_HEAD, :] = qb
        qf = qb.astype(F32)
        bound = jnp.sqrt(jnp.sum(qf * qf, axis=0, keepdims=True)) * kbound
        qT_ref[0, hd, QK_HEAD:QK_PAD, :] = jnp.where(brow == 0, -bound, 0.0).astype(BF16)

    knope = _dot(ckv, w_uk_ref[...])
    kcos = kcos_ref[...]
    ksin = ksin_ref[...]
    one_lane = lax.broadcasted_iota(I32, (tm, QK_PAD), 1) == QK_HEAD
    for hd in range(MLA_HEADS):
        kh = knope[:, hd * QK_PAD:(hd + 1) * QK_PAD] + kpe
        r = lax.rsqrt(jnp.sum(kh * kh, axis=-1, keepdims=True) * (1.0 / QK_HEAD) + EPS)
        k_ref[0, hd, 0] = jnp.where(one_lane, 1.0, (kh * kcos + kpe_sw * ksin) * r).astype(BF16)

    vT = _dot_nt(w_uvT_ref[...], ckv)
    row = lax.broadcasted_iota(I32, (V_ROWS - V_HEAD, tm), 0)
    tail = jnp.where(row == 0, 1.0, 0.0).astype(BF16)
    for hd in range(MLA_HEADS):
        vT_ref[0, hd, 0, 0:V_HEAD, :] = vT[hd * V_HEAD:(hd + 1) * V_HEAD, :].astype(BF16)
        vT_ref[0, hd, 0, V_HEAD:V_ROWS, :] = tail


def _mla_prep(x3, g1, wts, tabs, tm):
    B, S, _ = x3.shape
    nb = S // tm
    full = lambda a: pl.BlockSpec(a.shape, lambda b, i: (0,) * a.ndim)
    in_specs = [pl.BlockSpec((None, tm, D_MODEL), lambda b, i: (b, i, 0)), full(g1)]
    in_specs += [full(w) for w in wts]
    in_specs += [pl.BlockSpec((tm, LANES), lambda b, i: (i, 0)),
                 pl.BlockSpec((tm, LANES), lambda b, i: (i, 0)),
                 pl.BlockSpec((HALF_ROPE, tm), lambda b, i: (0, i)),
                 pl.BlockSpec((HALF_ROPE, tm), lambda b, i: (0, i))]
    out_shape = [jax.ShapeDtypeStruct((B, MLA_HEADS, QK_PAD, S), BF16),
                 jax.ShapeDtypeStruct((B, MLA_HEADS, nb, tm, QK_PAD), BF16),
                 jax.ShapeDtypeStruct((B, MLA_HEADS, nb, V_ROWS, tm), BF16)]
    out_specs = [pl.BlockSpec((1, MLA_HEADS, QK_PAD, tm), lambda b, i: (b, 0, 0, i)),
                 pl.BlockSpec((1, MLA_HEADS, 1, tm, QK_PAD), lambda b, i: (b, 0, i, 0, 0)),
                 pl.BlockSpec((1, MLA_HEADS, 1, V_ROWS, tm), lambda b, i: (b, 0, i, 0, 0))]
    return pl.pallas_call(
        _mla_prep_kernel, grid=(B, nb), in_specs=in_specs, out_specs=out_specs, out_shape=out_shape,
        compiler_params=_cparams(("arbitrary", "arbitrary")), name="mla_prep",
    )(x3, g1, *wts, *tabs)


def _attn_kernel(qT_ref, k_ref, vT_ref, o_ref, *, n_kt):
    tq = qT_ref.shape[3]
    qTs = [qT_ref[0, hh] for hh in range(HEADS_PER_STEP)]

    def body(kt, carry):
        out = []
        for hh in range(HEADS_PER_STEP):
            m, acc = carry[hh]
            sT = _dot(k_ref[0, hh, kt], qTs[hh])
            m_new = jnp.maximum(m, jnp.max(sT, axis=0, keepdims=True))
            alpha = jnp.exp2(m - m_new)
            pT = jnp.exp2(sT - m_new).astype(BF16)
            acc = alpha * acc + _dot(vT_ref[0, hh, kt], pT)
            out.append((m_new, acc))
        return tuple(out)

    init = tuple((jnp.full((1, tq), -jnp.inf, F32), jnp.zeros((V_ROWS, tq), F32))
                 for _ in range(HEADS_PER_STEP))
    res = lax.fori_loop(0, n_kt, body, init, unroll=2)
    outs = [acc[:V_HEAD] / acc[V_HEAD:V_HEAD + 1] for _, acc in res]
    o_ref[0] = jnp.concatenate(outs, axis=0).T.astype(o_ref.dtype)


def _attn_bounded_kernel(qT_ref, k_ref, vT_ref, o_ref, *, n_kt):
    tq = qT_ref.shape[3]
    qTs = [qT_ref[0, hh] for hh in range(HEADS_PER_STEP)]

    def scores(kt):
        return [_dot(k_ref[0, hh, kt], qTs[hh]) for hh in range(HEADS_PER_STEP)]

    def accumulate(accs, kt, sTs):
        return tuple(accs[hh] + _dot(vT_ref[0, hh, kt], jnp.exp2(sTs[hh]).astype(BF16))
                     for hh in range(HEADS_PER_STEP))

    def body(j, accs):
        kt = j * KT_PER_TRIP
        sTs = scores(kt)
        for u in range(1, KT_PER_TRIP):
            nxt = scores(kt + u)
            accs = accumulate(accs, kt + u - 1, sTs)
            sTs = nxt
        return accumulate(accs, kt + KT_PER_TRIP - 1, sTs)

    assert n_kt % KT_PER_TRIP == 0
    accs = lax.fori_loop(0, n_kt // KT_PER_TRIP, body,
                         tuple(jnp.zeros((V_ROWS, tq), F32) for _ in range(HEADS_PER_STEP)))
    outs = [acc[:V_HEAD] / acc[V_HEAD:V_HEAD + 1] for acc in accs]
    o_ref[0] = jnp.concatenate(outs, axis=0).T.astype(o_ref.dtype)


def _attention(qT, k, vT, tq, bounded):
    B, H, _, S = qT.shape
    n_kt, tk = k.shape[2], k.shape[3]
    hp = H // HEADS_PER_STEP
    return pl.pallas_call(
        functools.partial(_attn_bounded_kernel if bounded else _attn_kernel, n_kt=n_kt),
        grid=(B, hp, S // tq),
        in_specs=[pl.BlockSpec((1, HEADS_PER_STEP, QK_PAD, tq), lambda b, h, i: (b, h, 0, i)),
                  pl.BlockSpec((1, HEADS_PER_STEP, n_kt, tk, QK_PAD), lambda b, h, i: (b, h, 0, 0, 0)),
                  pl.BlockSpec((1, HEADS_PER_STEP, n_kt, V_ROWS, tk), lambda b, h, i: (b, h, 0, 0, 0))],
        out_specs=pl.BlockSpec((1, tq, HEADS_PER_STEP * V_HEAD), lambda b, h, i: (b, i, h)),
        out_shape=jax.ShapeDtypeStruct((B, S, H * V_HEAD), BF16),
        compiler_params=_cparams(("arbitrary", "arbitrary", "arbitrary")),
        name="mla_attention_bounded" if bounded else "mla_attention",
    )(qT, k, vT)


def _router_epilogue(x1, g2_ref, wr_cat_ref, wr_hi_ref, x1_ref, h2_ref, aff_ref):
    x1_ref[...] = x1
    h2 = _rms_rows(x1, g2_ref[...])
    h2_hi = h2.astype(BF16)
    h2_lo = (h2 - h2_hi.astype(F32)).astype(BF16)
    h2_ref[...] = h2_hi
    both = _dot(h2_hi, wr_cat_ref[...])
    logits = both[:, :N_EXPERTS] + both[:, N_EXPERTS:] + _dot(h2_lo, wr_hi_ref[...])
    ex = jnp.exp(logits - jnp.max(logits, axis=-1, keepdims=True))
    aff_ref[...] = ex / jnp.sum(ex, axis=-1, keepdims=True)


def _outproj_kernel(o_ref, w_ref, x_ref, g2_ref, wr_cat_ref, wr_hi_ref, x1_ref, h2_ref, aff_ref):
    x1 = x_ref[...] + _dot(o_ref[...], w_ref[...])
    _router_epilogue(x1, g2_ref, wr_cat_ref, wr_hi_ref, x1_ref, h2_ref, aff_ref)


def _row_specs(n, tm):
    out_shape = [jax.ShapeDtypeStruct((n, D_MODEL), F32),
                 jax.ShapeDtypeStruct((n, D_MODEL), BF16),
                 jax.ShapeDtypeStruct((n, N_EXPERTS), F32)]
    out_specs = [pl.BlockSpec((tm, D_MODEL), lambda i: (i, 0)),
                 pl.BlockSpec((tm, D_MODEL), lambda i: (i, 0)),
                 pl.BlockSpec((tm, N_EXPERTS), lambda i: (i, 0))]
    return out_shape, out_specs


def _outproj(o, w_out, x, g2, wr_cat, wr_hi, tm):
    n = x.shape[0]
    full = lambda a: pl.BlockSpec(a.shape, lambda i: (0,) * a.ndim)
    out_shape, out_specs = _row_specs(n, tm)
    return pl.pallas_call(
        _outproj_kernel, grid=(n // tm,),
        in_specs=[pl.BlockSpec((tm, D_MODEL), lambda i: (i, 0)), full(w_out),
                  pl.BlockSpec((tm, D_MODEL), lambda i: (i, 0)), full(g2), full(wr_cat), full(wr_hi)],
        out_specs=out_specs, out_shape=out_shape,
        compiler_params=_cparams(("arbitrary",)), name="mla_outproj_router",
    )(o, w_out, x, g2, wr_cat, wr_hi)


def _sgu_kernel(x_ref, g1_ref, w_in_ref, lng_ref, lnb_ref, ws_ref, bs_ref, w_out_ref,
                g2_ref, wr_cat_ref, wr_hi_ref, x1_ref, h2_ref, aff_ref):
    x = x_ref[...]
    tm = x.shape[0]
    h = _rms_rows(x, g1_ref[...]).astype(BF16)
    z = _dot(h, w_in_ref[...])
    z = 0.5 * z * (1.0 + lax.erf(z * np.float32(2.0 ** -0.5)))
    u = z[:, :SGU_HALF]
    v = z[:, SGU_HALF:]
    mu = jnp.mean(v, axis=-1, keepdims=True)
    vc = v - mu
    v = vc * lax.rsqrt(jnp.mean(vc * vc, axis=-1, keepdims=True) + EPS) * lng_ref[...] + lnb_ref[...]
    vb = v.astype(BF16)
    bs = bs_ref[...]
    rows = []
    for c in range(tm // CHUNK):
        cols = []
        for g in range(SGU_GROUPS):
            blk = vb[c * CHUNK:(c + 1) * CHUNK, g * LANES:(g + 1) * LANES]
            cols.append(_dot(ws_ref[g], blk))
        rows.append(jnp.concatenate(cols, axis=1) + bs)
    sv = jnp.concatenate(rows, axis=0)
    gated = (u * sv).astype(BF16)
    x1 = x + _dot(gated, w_out_ref[...])
    _router_epilogue(x1, g2_ref, wr_cat_ref, wr_hi_ref, x1_ref, h2_ref, aff_ref)


def _sgu(x, g1, w_in, lng, lnb, ws, bs_full, w_out, g2, wr_cat, wr_hi, tm):
    n = x.shape[0]
    full = lambda a: pl.BlockSpec(a.shape, lambda i: (0,) * a.ndim)
    ops = (g1, w_in, lng, lnb, ws, bs_full, w_out, g2, wr_cat, wr_hi)
    out_shape, out_specs = _row_specs(n, tm)
    return pl.pallas_call(
        _sgu_kernel, grid=(n // tm,),
        in_specs=[pl.BlockSpec((tm, D_MODEL), lambda i: (i, 0))] + [full(a) for a in ops],
        out_specs=out_specs, out_shape=out_shape,
        compiler_params=_cparams(("arbitrary",)), name="sgu_router",
    )(x, *ops)


def _select_kernel(aff_ref, lp_ref, start_ref, cnt_ref, *, nt, cap):
    a = aff_ref[...]
    bits = pltpu.bitcast(a, I32)

    def body(i, cur):
        trial = cur | jnp.left_shift(jnp.int32(1), 30 - i)
        ge = jnp.where(bits >= trial, 1.0, 0.0)
        cnt = jnp.sum(jnp.sum(ge, axis=1, keepdims=True), axis=2, keepdims=True)
        return jnp.where(cnt >= cap, trial, cur)

    thr = lax.fori_loop(0, 31, body, jnp.zeros((N_EXPERTS, 1, 1), I32))
    gtf = jnp.where(bits > thr, 1.0, 0.0)
    eqf = jnp.where(bits == thr, 1.0, 0.0)
    c_gt = jnp.sum(jnp.sum(gtf, axis=1, keepdims=True), axis=2, keepdims=True)
    need = cap - c_gt

    r = lax.broadcasted_iota(I32, (LANES, LANES), 0)
    c = lax.broadcasted_iota(I32, (LANES, LANES), 1)
    incl_u = jnp.where(r <= c, 1.0, 0.0).astype(BF16)
    ones_u = jnp.ones((LANES, LANES), BF16)
    rt = lax.broadcasted_iota(I32, (nt, nt), 0)
    ct = lax.broadcasted_iota(I32, (nt, nt), 1)
    excl_l = jnp.where(rt > ct, 1.0, 0.0).astype(BF16)

    def tile_prefix(maskf):
        m2 = maskf.reshape(N_EXPERTS * nt, LANES).astype(BF16)
        incl = _dot(m2, incl_u)
        cnt = _dot(m2, ones_u)
        cnt_b = cnt.astype(BF16)
        starts = [_dot(excl_l, cnt_b[e * nt:(e + 1) * nt]) for e in range(N_EXPERTS)]
        return incl, jnp.concatenate(starts, axis=0), cnt

    eq_incl, eq_start, _ = tile_prefix(eqf)
    tie_rank = (eq_start + eq_incl).reshape(N_EXPERTS, nt, LANES) - eqf
    selm = gtf + eqf * jnp.where(tie_rank < need, 1.0, 0.0)
    incl, start, cnt = tile_prefix(selm)
    lp = jnp.where(selm.reshape(N_EXPERTS * nt, LANES) > 0.5, incl - 1.0, float(NOT_SELECTED))
    lp_ref[...] = lp.astype(I32)
    start_ref[...] = start.astype(I32)
    cnt_ref[...] = cnt.astype(I32)


def _select(aff_t3, cap):
    e, nt, _ = aff_t3.shape
    return pl.pallas_call(
        functools.partial(_select_kernel, nt=nt, cap=cap),
        out_shape=[jax.ShapeDtypeStruct((e * nt, LANES), I32)] * 3,
        compiler_params=pltpu.CompilerParams(vmem_limit_bytes=VMEM_LIMIT), name="moe_select",
    )(aff_t3)


def _round_down(v):
    return (v >> ROW_ALIGN_SHIFT) << ROW_ALIGN_SHIFT


def _tile_is_short(start_sm, cnt_sm, nt, t):
    worst = jnp.int32(0)
    for e in range(N_EXPERTS):
        start = start_sm[e * nt + t]
        worst = jnp.maximum(worst, start - _round_down(start) + cnt_sm[e * nt + t])
    return worst <= SHORT_SLAB - ROW_ALIGN


def _by_slab_length(short, fn):
    @pl.when(short)
    def _():
        fn(SHORT_SLAB)

    @pl.when(jnp.logical_not(short))
    def _():
        fn(SLAB)


def _dispatch_kernel(start_sm, cnt_sm, x_ref, lp_ref, xg_hbm, stage, carry, sems, *, nt, cap):
    t = pl.program_id(0)
    slot = t % 2

    @pl.when(t == 0)
    def _():
        carry[...] = jnp.zeros(carry.shape, carry.dtype)
        stage[1] = jnp.zeros(stage.shape[1:], stage.dtype)
        pads = [pltpu.make_async_copy(stage.at[1, e], xg_hbm.at[e, pl.ds(cap, SLAB), :], sems.at[1, e])
                for e in range(N_EXPERTS)]
        for cp in pads:
            cp.start()
        for cp in pads:
            cp.wait()

    x = x_ref[...]
    rio = lax.broadcasted_iota(I32, (ROW_ALIGN, D_MODEL), 0)
    short_now = _tile_is_short(start_sm, cnt_sm, nt, t)

    def fill(rows):
        jio = lax.broadcasted_iota(I32, (rows, TOK_TILE), 0)
        shifts = [start_sm[e * nt + t] - _round_down(start_sm[e * nt + t]) for e in range(N_EXPERTS)]
        onehots = jnp.concatenate(
            [jnp.where(lp_ref[0, e:e + 1, :] + shifts[e] == jio, 1.0, 0.0).astype(BF16) for e in range(N_EXPERTS)],
            axis=0)
        slabs = _dot(onehots, x)
        for e in range(N_EXPERTS):
            shift = shifts[e]
            slab = slabs[e * rows:(e + 1) * rows]
            head = jnp.where(rio < shift, carry[e].astype(F32), slab[0:ROW_ALIGN])
            stage[slot, e, 0:ROW_ALIGN] = head.astype(BF16)
            stage[slot, e, ROW_ALIGN:rows] = slab[ROW_ALIGN:].astype(BF16)
            nxt = pl.multiple_of(_round_down(shift + cnt_sm[e * nt + t]), ROW_ALIGN)
            carry[e] = stage[slot, e, pl.ds(nxt, ROW_ALIGN)]

    def copy(s, e, row0, rows):
        dst = xg_hbm.at[e, pl.ds(pl.multiple_of(row0, ROW_ALIGN), rows), :]
        return pltpu.make_async_copy(stage.at[s, e, pl.ds(0, rows)], dst, sems.at[s, e])

    by_length = _by_slab_length
    by_length(short_now, fill)

    @pl.when(t > 0)
    def _():
        by_length(_tile_is_short(start_sm, cnt_sm, nt, t - 1),
                  lambda rows: [copy(1 - slot, e, 0, rows).wait() for e in range(N_EXPERTS)])

    by_length(short_now, lambda rows: [copy(slot, e, _round_down(start_sm[e * nt + t]), rows).start()
                                       for e in range(N_EXPERTS)])

    @pl.when(t == nt - 1)
    def _():
        by_length(short_now, lambda rows: [copy(slot, e, 0, rows).wait() for e in range(N_EXPERTS)])


def _dispatch(start_flat, cnt_flat, h2, lp3, cap):
    n = h2.shape[0]
    nt = n // TOK_TILE
    grid_spec = pltpu.PrefetchScalarGridSpec(
        num_scalar_prefetch=2, grid=(nt,),
        in_specs=[pl.BlockSpec((TOK_TILE, D_MODEL), lambda t, s, c: (t, 0)),
                  pl.BlockSpec((1, N_EXPERTS, LANES), lambda t, s, c: (t, 0, 0))],
        out_specs=pl.BlockSpec(memory_space=pl.ANY),
        scratch_shapes=[pltpu.VMEM((2, N_EXPERTS, SLAB, D_MODEL), BF16),
                        pltpu.VMEM((N_EXPERTS, ROW_ALIGN, D_MODEL), BF16),
                        pltpu.SemaphoreType.DMA((2, N_EXPERTS))])
    return pl.pallas_call(
        functools.partial(_dispatch_kernel, nt=nt, cap=cap), grid_spec=grid_spec,
        out_shape=jax.ShapeDtypeStruct((N_EXPERTS, cap + SLAB, D_MODEL), BF16),
        compiler_params=_cparams(("arbitrary",)), name="moe_dispatch",
    )(start_flat, cnt_flat, h2, lp3)


def _ffn_kernel(xg_ref, wg_ref, wu_ref, wd_ref, yg_ref):
    x = xg_ref[0]
    g = _dot(x, wg_ref[0])
    u = _dot(x, wu_ref[0])
    hid = (g * jax.nn.sigmoid(g) * u).astype(BF16)
    y = _dot(hid, wd_ref[0])
    yg_ref[0] = y.astype(BF16)


def _ffn(xg, wg, wu, wd, cap, rt):
    wspec = pl.BlockSpec((1, D_MODEL, EXPERT_FF), lambda e, r: (e, 0, 0))
    return pl.pallas_call(
        _ffn_kernel, grid=(N_EXPERTS, cap // rt),
        in_specs=[pl.BlockSpec((1, rt, D_MODEL), lambda e, r: (e, r, 0)), wspec, wspec,
                  pl.BlockSpec((1, EXPERT_FF, D_MODEL), lambda e, r: (e, 0, 0))],
        out_specs=pl.BlockSpec((1, rt, D_MODEL), lambda e, r: (e, r, 0)),
        out_shape=jax.ShapeDtypeStruct((N_EXPERTS, cap, D_MODEL), BF16),
        compiler_params=_cparams(("arbitrary", "arbitrary")), name="moe_ffn",
    )(xg, wg, wu, wd)


def _combine_kernel(start_sm, cnt_sm, x_ref, lpt_ref, aff_ref, yg_hbm, o_ref, slabs, sems, *, nt, cap):
    t = pl.program_id(0)
    slot = t % 2

    def row0(e, tt, rows):
        return jnp.minimum(_round_down(start_sm[e * nt + tt]), cap - rows)

    def copy(s, e, r0, rows):
        src = yg_hbm.at[e, pl.ds(pl.multiple_of(r0, ROW_ALIGN), rows), :]
        return pltpu.make_async_copy(src, slabs.at[s, e, pl.ds(0, rows)], sems.at[s, e])

    def fetch(s, tt):
        _by_slab_length(_tile_is_short(start_sm, cnt_sm, nt, tt),
                        lambda rows: [copy(s, e, row0(e, tt, rows), rows).start() for e in range(N_EXPERTS)])

    @pl.when(t == 0)
    def _():
        fetch(0, 0)

    @pl.when(t + 1 < nt)
    def _():
        fetch(1 - slot, t + 1)

    def expand_short(rows):
        for e in range(N_EXPERTS):
            copy(slot, e, 0, rows).wait()
        stacked = jnp.concatenate([slabs[slot, e, 0:rows] for e in range(N_EXPERTS)], axis=0)
        width = N_EXPERTS * rows
        seg = lax.broadcasted_iota(I32, (N_EXPERTS, width), 1) // rows
        spread = jnp.where(seg == lax.broadcasted_iota(I32, (N_EXPERTS, width), 0), 1.0, 0.0).astype(BF16)
        g = aff_ref[...]
        g_hi = g.astype(BF16)
        g_lo = (g - g_hi.astype(F32)).astype(BF16)
        cols = jnp.concatenate([lpt_ref[...].astype(F32).astype(BF16), g_hi, g_lo], axis=0)
        wide = _dot(cols, spread)
        lane = lax.broadcasted_iota(I32, (1, width), 1)
        target = lane - (lane // rows) * rows
        for e in range(N_EXPERTS):
            delta = start_sm[e * nt + t] - row0(e, t, rows)
            target = jnp.where(lane // rows == e, target - delta, target)
        hit = wide[:TOK_TILE] == target.astype(F32)
        w = jnp.concatenate([jnp.where(hit, wide[TOK_TILE:2 * TOK_TILE], 0.0).astype(BF16),
                             jnp.where(hit, wide[2 * TOK_TILE:], 0.0).astype(BF16)], axis=0)
        both = _dot(w, stacked)
        o_ref[...] = x_ref[...] + both[:TOK_TILE] + both[TOK_TILE:]

    def expand_long(rows):
        for e in range(N_EXPERTS):
            copy(slot, e, 0, rows).wait()
        acc = x_ref[...]
        lane = lax.broadcasted_iota(I32, (TOK_TILE, rows), 1)
        for e in range(N_EXPERTS):
            delta = start_sm[e * nt + t] - row0(e, t, rows)
            onehot = jnp.where(lpt_ref[:, e:e + 1] + delta == lane, 1.0, 0.0).astype(BF16)
            acc = acc + aff_ref[:, e:e + 1] * _dot(onehot, slabs[slot, e])
        o_ref[...] = acc

    _by_slab_length(_tile_is_short(start_sm, cnt_sm, nt, t),
                    lambda rows: expand_short(rows) if rows == SHORT_SLAB else expand_long(rows))


def _combine(start_flat, cnt_flat, x1, lp_t, aff, yg, cap):
    n = x1.shape[0]
    nt = n // TOK_TILE
    grid_spec = pltpu.PrefetchScalarGridSpec(
        num_scalar_prefetch=2, grid=(nt,),
        in_specs=[pl.BlockSpec((TOK_TILE, D_MODEL), lambda t, s, c: (t, 0)),
                  pl.BlockSpec((TOK_TILE, N_EXPERTS), lambda t, s, c: (t, 0)),
                  pl.BlockSpec((TOK_TILE, N_EXPERTS), lambda t, s, c: (t, 0)),
                  pl.BlockSpec(memory_space=pl.ANY)],
        out_specs=pl.BlockSpec((TOK_TILE, D_MODEL), lambda t, s, c: (t, 0)),
        scratch_shapes=[pltpu.VMEM((2, N_EXPERTS, SLAB, D_MODEL), BF16),
                        pltpu.SemaphoreType.DMA((2, N_EXPERTS))])
    return pl.pallas_call(
        functools.partial(_combine_kernel, nt=nt, cap=cap), grid_spec=grid_spec,
        out_shape=jax.ShapeDtypeStruct((n, D_MODEL), F32),
        compiler_params=_cparams(("arbitrary",)), name="moe_combine",
    )(start_flat, cnt_flat, x1, lp_t, aff, yg)


def _moe(x1, h2, aff, wg, wu, wd, ffn_rows):
    n = x1.shape[0]
    nt = n // TOK_TILE
    cap = max(1, CAPACITY_FACTOR * n // N_EXPERTS)
    assert cap >= SLAB and cap % ROW_ALIGN == 0
    lp, start, cnt = _select(aff.T.reshape(N_EXPERTS, nt, LANES), cap)
    start_flat, cnt_flat = start[:, 0], cnt[:, 0]
    lp_t = lp.reshape(N_EXPERTS, n).T
    xg = _dispatch(start_flat, cnt_flat, h2, lp.reshape(N_EXPERTS, nt, LANES).transpose(1, 0, 2), cap)
    yg = _ffn(xg, wg, wu, wd, cap, min(ffn_rows, cap))
    return _combine(start_flat, cnt_flat, x1, lp_t, aff, yg, cap)


def _rope_tables(S):
    pos = jnp.arange(S, dtype=F32)
    inv_freq = 1.0 / (ROPE_THETA ** (jnp.arange(0, ROPE_DIM, 2, dtype=F32) / ROPE_DIM))
    ang = pos[:, None] * inv_freq[None, :]
    return jnp.cos(ang), jnp.sin(ang)


def _prep_mla_weights(w_in, gq, gkv, w_uq, w_ukv, qn_g, kn_g):
    pad = lambda a, lo, hi: jnp.pad(a, ((0, 0), (lo, hi)))
    pe = w_in[:, Q_LORA + KV_LORA:]
    pe_sw = jnp.concatenate([pe[:, HALF_ROPE:], pe[:, :HALF_ROPE]], axis=1)
    w_in_p = jnp.concatenate([w_in[:, :Q_LORA + KV_LORA], pad(pe, NOPE_DIM, QK_PAD - QK_HEAD),
                              pad(pe_sw, NOPE_DIM, QK_PAD - QK_HEAD)], axis=1).astype(BF16)
    w_uqT = w_uq.T.astype(BF16)
    ukv = w_ukv.reshape(KV_LORA, MLA_HEADS, NOPE_DIM + V_HEAD)
    w_uk = jnp.pad(ukv[:, :, :NOPE_DIM], ((0, 0), (0, 0), (0, QK_PAD - NOPE_DIM)))
    w_uk = w_uk.reshape(KV_LORA, MLA_HEADS * QK_PAD).astype(BF16)
    w_uvT = ukv[:, :, NOPE_DIM:].reshape(KV_LORA, MLA_HEADS * V_HEAD).T.astype(BF16)
    gqc = (qn_g * np.float32(QK_HEAD ** -0.5 * np.log2(np.e))).reshape(QK_HEAD, 1)
    kb = (jnp.max(jnp.abs(kn_g)) * np.float32(QK_HEAD ** 0.5 * BOUND_MARGIN)).reshape(1, 1)
    return (w_in_p, gq.reshape(1, Q_LORA), gkv.reshape(1, KV_LORA), w_uqT, w_uk, w_uvT, gqc, kb)


def _score_bound_log2(qn_g, kn_g):
    return (jnp.max(jnp.abs(qn_g)) * jnp.max(jnp.abs(kn_g))
            * np.float32(QK_HEAD * QK_HEAD ** -0.5 * np.log2(np.e) * BOUND_MARGIN * BOUND_MARGIN))


def _prep_k_tables(cos, sin, kn_g):
    S = cos.shape[0]
    z = lambda w: jnp.zeros((S, w), F32)
    cos_l = jnp.concatenate([jnp.ones((S, NOPE_DIM), F32), cos, cos, z(QK_PAD - QK_HEAD)], axis=1)
    sin_l = jnp.concatenate([z(NOPE_DIM), -sin, sin, z(QK_PAD - QK_HEAD)], axis=1)
    g = jnp.pad(kn_g, (0, QK_PAD - QK_HEAD))
    g_sw = jnp.concatenate([jnp.zeros((NOPE_DIM,), F32), kn_g[NOPE_DIM + HALF_ROPE:],
                            kn_g[NOPE_DIM:NOPE_DIM + HALF_ROPE], jnp.zeros((QK_PAD - QK_HEAD,), F32)])
    return cos_l * g[None, :], sin_l * g_sw[None, :]


def _router_weights(w_r):
    hi = w_r.astype(BF16)
    lo = (w_r - hi.astype(F32)).astype(BF16)
    return jnp.concatenate([hi, lo], axis=1), hi


def _trunk(x3, p, tiles):
    B, S, _ = x3.shape
    n = B * S
    cos, sin = _rope_tables(S)
    mw = _prep_mla_weights(p['mla_w_in'][0], p['mla_q_lora_g'][0], p['mla_kv_lora_g'][0], p['mla_w_uq'][0],
                           p['mla_w_ukv'][0], p['mla_q_norm_g'][0], p['mla_k_norm_g'][0])
    kcos, ksin = _prep_k_tables(cos, sin, p['mla_k_norm_g'][0])
    qT, k, vT = _mla_prep(x3, p['norm1_g'][0].reshape(1, D_MODEL), mw, (kcos, ksin, cos.T, sin.T), tiles['tk'])
    fits = 2.0 * _score_bound_log2(p['mla_q_norm_g'][0], p['mla_k_norm_g'][0]) <= MAX_SHIFTED_SCORE_RANGE
    o = lax.cond(fits, lambda: _attention(qT, k, vT, tiles['tq'], True),
                 lambda: _attention(qT, k, vT, tiles['tq'], False)).reshape(n, D_MODEL)
    x = x3.reshape(n, D_MODEL)
    wr_cat, wr_hi = _router_weights(p['moe_w_router'][0])
    x1, h2, aff = _outproj(o, p['mla_w_out'][0].astype(BF16), x, p['norm2_g'][0].reshape(1, D_MODEL),
                           wr_cat, wr_hi, tiles['tm'])
    x = _moe(x1, h2, aff, p['wg'][0], p['wu'][0], p['wd'][0], tiles['ffn_rows'])
    wr_cat, wr_hi = _router_weights(p['moe_w_router'][1])
    bs_full = jnp.repeat(p['sgu_b_s'][0].T, LANES, axis=1)
    x1, h2, aff = _sgu(x, p['norm1_g'][1].reshape(1, D_MODEL), p['sgu_w_in'][0].astype(BF16),
                       p['sgu_ln_g'][0].reshape(1, SGU_HALF), p['sgu_ln_b'][0].reshape(1, SGU_HALF),
                       p['sgu_w_s'][0].astype(BF16), bs_full, p['sgu_w_out'][0].astype(BF16),
                       p['norm2_g'][1].reshape(1, D_MODEL), wr_cat, wr_hi, tiles['tm_sgu'])
    x = _moe(x1, h2, aff, p['wg'][1], p['wu'][1], p['wd'][1], tiles['ffn_rows'])
    return x.reshape(B, S, D_MODEL)


TILES = dict(tk=512, tq=512, tm=512, tm_sgu=256, ffn_rows=512)


def kernel(x_prompt, x_sample, norm1_g, norm2_g, mla_w_in, mla_q_lora_g, mla_kv_lora_g, mla_w_uq, mla_w_ukv, mla_q_norm_g, mla_k_norm_g, mla_w_out, sgu_w_in, sgu_ln_g, sgu_ln_b, sgu_w_s, sgu_b_s, sgu_w_out, moe_w_router, moe_w_gate, moe_w_up, moe_w_down):
    p = dict(norm1_g=norm1_g, norm2_g=norm2_g, mla_w_in=mla_w_in, mla_q_lora_g=mla_q_lora_g,
             mla_kv_lora_g=mla_kv_lora_g, mla_w_uq=mla_w_uq, mla_w_ukv=mla_w_ukv, mla_q_norm_g=mla_q_norm_g,
             mla_k_norm_g=mla_k_norm_g, mla_w_out=mla_w_out, sgu_w_in=sgu_w_in, sgu_ln_g=sgu_ln_g,
             sgu_ln_b=sgu_ln_b, sgu_w_s=sgu_w_s, sgu_b_s=sgu_b_s, sgu_w_out=sgu_w_out, moe_w_router=moe_w_router,
             wg=moe_w_gate.astype(BF16), wu=moe_w_up.astype(BF16), wd=moe_w_down.astype(BF16))
    return (_trunk(x_prompt, p, TILES), _trunk(x_sample, p, TILES))
```

```python
import functools

import jax
import jax.numpy as jnp
import numpy as np
from jax import lax
from jax.experimental import pallas as pl
from jax.experimental.pallas import tpu as pltpu

F32 = jnp.float32
BF16 = jnp.bfloat16
I32 = jnp.int32

D_MODEL = 1024
MLA_HEADS = 16
Q_LORA = 256
KV_LORA = 128
NOPE_DIM = 64
ROPE_DIM = 32
HALF_ROPE = ROPE_DIM // 2
QK_HEAD = NOPE_DIM + ROPE_DIM
V_HEAD = 64
ROPE_THETA = 10000.0
CHUNK = 128
SGU_HALF = D_MODEL
SGU_GROUPS = 8
N_EXPERTS = 16
EXPERT_FF = 1024
CAPACITY_FACTOR = 2
EPS = 1e-6

LANES = 128
QK_PAD = 128
V_ROWS = 80
COMB_TILE = 128
DISP_TILE = 256
ROW_ALIGN = 16
ROW_ALIGN_SHIFT = 4
COMB_LONG, COMB_SHORT = COMB_TILE + ROW_ALIGN, 64
DISP_LONG, DISP_SHORT = DISP_TILE + ROW_ALIGN, 96
NOT_SELECTED = -1024
BOUND_MARGIN = 1.0 + 2.0 ** -6
MAX_SHIFTED_SCORE_RANGE = 100.0
HEADS_PER_STEP = 2
KT_PER_TRIP = 16

VMEM_LIMIT = 52 * 1024 * 1024


def _cparams(sem):
    return pltpu.CompilerParams(dimension_semantics=sem, vmem_limit_bytes=VMEM_LIMIT)


def _dot(a, b):
    return jnp.dot(a, b, preferred_element_type=F32)


def _dot_nt(a, b):
    return lax.dot_general(a, b, (((1,), (1,)), ((), ())), preferred_element_type=F32)


def _rms_rows(x, g):
    return x * lax.rsqrt(jnp.mean(x * x, axis=-1, keepdims=True) + EPS) * g


def _mla_prep_kernel(x_ref, g1_ref, w_in_ref, gq_ref, gkv_ref, w_uqT_ref, w_uk_ref, w_uvT_ref,
                     gqc_ref, kb_ref, kcos_ref, ksin_ref, cosT_ref, sinT_ref,
                     qT_ref, k_ref, vT_ref):
    x = x_ref[...]
    h = _rms_rows(x, g1_ref[...]).astype(BF16)
    proj = _dot(h, w_in_ref[...])
    cq = _rms_rows(proj[:, :Q_LORA], gq_ref[...]).astype(BF16)
    ckv = _rms_rows(proj[:, Q_LORA:Q_LORA + KV_LORA], gkv_ref[...]).astype(BF16)
    kpe = proj[:, 384:512]
    kpe_sw = proj[:, 512:640]
    tm = x.shape[0]

    qT = _dot_nt(w_uqT_ref[...], cq)
    cosT = cosT_ref[...]
    sinT = sinT_ref[...]
    gqc = gqc_ref[...]
    kbound = kb_ref[...]
    brow = lax.broadcasted_iota(I32, (QK_PAD - QK_HEAD, tm), 0)
    for hd in range(MLA_HEADS):
        qh = qT[hd * QK_HEAD:(hd + 1) * QK_HEAD, :]
        r = lax.rsqrt(jnp.sum(qh * qh, axis=0, keepdims=True) * (1.0 / QK_HEAD) + EPS)
        qn = qh * gqc * r
        x1 = qn[NOPE_DIM:NOPE_DIM + HALF_ROPE]
        x2 = qn[NOPE_DIM + HALF_ROPE:QK_HEAD]
        qb = jnp.concatenate([qn[:NOPE_DIM], x1 * cosT - x2 * sinT, x2 * cosT + x1 * sinT], axis=0).astype(BF16)
        qT_ref[0, hd, 0:QK_HEAD, :] = qb
        qf = qb.astype(F32)
        bound = jnp.sqrt(jnp.sum(qf * qf, axis=0, keepdims=True)) * kbound
        qT_ref[0, hd, QK_HEAD:QK_PAD, :] = jnp.where(brow == 0, -bound, 0.0).astype(BF16)

    knope = _dot(ckv, w_uk_ref[...])
    kcos = kcos_ref[...]
    ksin = ksin_ref[...]
    one_lane = lax.broadcasted_iota(I32, (tm, QK_PAD), 1) == QK_HEAD
    for hd in range(MLA_HEADS):
        kh = knope[:, hd * QK_PAD:(hd + 1) * QK_PAD] + kpe
        r = lax.rsqrt(jnp.sum(kh * kh, axis=-1, keepdims=True) * (1.0 / QK_HEAD) + EPS)
        k_ref[0, hd, 0] = jnp.where(one_lane, 1.0, (kh * kcos + kpe_sw * ksin) * r).astype(BF16)

    vT = _dot_nt(w_uvT_ref[...], ckv)
    row = lax.broadcasted_iota(I32, (V_ROWS - V_HEAD, tm), 0)
    tail = jnp.where(row == 0, 1.0, 0.0).astype(BF16)
    for hd in range(MLA_HEADS):
        vT_ref[0, hd, 0, 0:V_HEAD, :] = vT[hd * V_HEAD:(hd + 1) * V_HEAD, :].astype(BF16)
        vT_ref[0, hd, 0, V_HEAD:V_ROWS, :] = tail


def _mla_prep(x3, g1, wts, tabs, tm):
    B, S, _ = x3.shape
    nb = S // tm
    full = lambda a: pl.BlockSpec(a.shape, lambda b, i: (0,) * a.ndim)
    in_specs = [pl.BlockSpec((None, tm, D_MODEL), lambda b, i: (b, i, 0)), full(g1)]
    in_specs += [full(w) for w in wts]
    in_specs += [pl.BlockSpec((tm, LANES), lambda b, i: (i, 0)),
                 pl.BlockSpec((tm, LANES), lambda b, i: (i, 0)),
                 pl.BlockSpec((HALF_ROPE, tm), lambda b, i: (0, i)),
                 pl.BlockSpec((HALF_ROPE, tm), lambda b, i: (0, i))]
    out_shape = [jax.ShapeDtypeStruct((B, MLA_HEADS, QK_PAD, S), BF16),
                 jax.ShapeDtypeStruct((B, MLA_HEADS, nb, tm, QK_PAD), BF16),
                 jax.ShapeDtypeStruct((B, MLA_HEADS, nb, V_ROWS, tm), BF16)]
    out_specs = [pl.BlockSpec((1, MLA_HEADS, QK_PAD, tm), lambda b, i: (b, 0, 0, i)),
                 pl.BlockSpec((1, MLA_HEADS, 1, tm, QK_PAD), lambda b, i: (b, 0, i, 0, 0)),
                 pl.BlockSpec((1, MLA_HEADS, 1, V_ROWS, tm), lambda b, i: (b, 0, i, 0, 0))]
    return pl.pallas_call(
        _mla_prep_kernel, grid=(B, nb), in_specs=in_specs, out_specs=out_specs, out_shape=out_shape,
        compiler_params=_cparams(("arbitrary", "arbitrary")), name="mla_prep",
    )(x3, g1, *wts, *tabs)


def _attn_kernel(qT_ref, k_ref, vT_ref, o_ref, *, n_kt):
    tq = qT_ref.shape[3]
    qTs = [qT_ref[0, hh] for hh in range(HEADS_PER_STEP)]

    def body(kt, carry):
        out = []
        for hh in range(HEADS_PER_STEP):
            m, acc = carry[hh]
            sT = _dot(k_ref[0, hh, kt], qTs[hh])
            m_new = jnp.maximum(m, jnp.max(sT, axis=0, keepdims=True))
            alpha = jnp.exp2(m - m_new)
            pT = jnp.exp2(sT - m_new).astype(BF16)
            acc = alpha * acc + _dot(vT_ref[0, hh, kt], pT)
            out.append((m_new, acc))
        return tuple(out)

    init = tuple((jnp.full((1, tq), -jnp.inf, F32), jnp.zeros((V_ROWS, tq), F32))
                 for _ in range(HEADS_PER_STEP))
    res = lax.fori_loop(0, n_kt, body, init, unroll=2)
    outs = [acc[:V_HEAD] / acc[V_HEAD:V_HEAD + 1] for _, acc in res]
    o_ref[0] = jnp.concatenate(outs, axis=0).T.astype(o_ref.dtype)


def _attn_bounded_kernel(qT_ref, k_ref, vT_ref, o_ref, *, n_kt):
    tq = qT_ref.shape[3]
    qTs = [qT_ref[0, hh] for hh in range(HEADS_PER_STEP)]

    def scores(kt):
        return [_dot(k_ref[0, hh, kt], qTs[hh]) for hh in range(HEADS_PER_STEP)]

    def accumulate(accs, kt, sTs):
        return tuple(accs[hh] + _dot(vT_ref[0, hh, kt], jnp.exp2(sTs[hh]).astype(BF16))
                     for hh in range(HEADS_PER_STEP))

    def body(j, accs):
        kt = j * per_trip
        sTs = scores(kt)
        for u in range(1, per_trip):
            nxt = scores(kt + u)
            accs = accumulate(accs, kt + u - 1, sTs)
            sTs = nxt
        return accumulate(accs, kt + per_trip - 1, sTs)

    per_trip = min(KT_PER_TRIP, n_kt)
    assert n_kt % per_trip == 0
    accs = lax.fori_loop(0, n_kt // per_trip, body,
                         tuple(jnp.zeros((V_ROWS, tq), F32) for _ in range(HEADS_PER_STEP)))
    outs = [acc[:V_HEAD] / acc[V_HEAD:V_HEAD + 1] for acc in accs]
    o_ref[0] = jnp.concatenate(outs, axis=0).T.astype(o_ref.dtype)


def _attention(qT, k, vT, tq, bounded):
    B, H, _, S = qT.shape
    n_kt, tk = k.shape[2], k.shape[3]
    hp = H // HEADS_PER_STEP
    return pl.pallas_call(
        functools.partial(_attn_bounded_kernel if bounded else _attn_kernel, n_kt=n_kt),
        grid=(B, hp, S // tq),
        in_specs=[pl.BlockSpec((1, HEADS_PER_STEP, QK_PAD, tq), lambda b, h, i: (b, h, 0, i)),
                  pl.BlockSpec((1, HEADS_PER_STEP, n_kt, tk, QK_PAD), lambda b, h, i: (b, h, 0, 0, 0)),
                  pl.BlockSpec((1, HEADS_PER_STEP, n_kt, V_ROWS, tk), lambda b, h, i: (b, h, 0, 0, 0))],
        out_specs=pl.BlockSpec((1, tq, HEADS_PER_STEP * V_HEAD), lambda b, h, i: (b, i, h)),
        out_shape=jax.ShapeDtypeStruct((B, S, H * V_HEAD), BF16),
        compiler_params=_cparams(("arbitrary", "arbitrary", "arbitrary")),
        name="mla_attention_bounded" if bounded else "mla_attention",
    )(qT, k, vT)


def _router_epilogue(x1, g2_ref, wr_cat_ref, wr_hi_ref, x1_ref, h2_ref, aff_ref):
    x1_ref[...] = x1
    h2 = _rms_rows(x1, g2_ref[...])
    h2_hi = h2.astype(BF16)
    h2_lo = (h2 - h2_hi.astype(F32)).astype(BF16)
    h2_ref[...] = h2_hi
    both = _dot(h2_hi, wr_cat_ref[...])
    logits = both[:, :N_EXPERTS] + both[:, N_EXPERTS:] + _dot(h2_lo, wr_hi_ref[...])
    ex = jnp.exp(logits - jnp.max(logits, axis=-1, keepdims=True))
    aff_ref[...] = ex / jnp.sum(ex, axis=-1, keepdims=True)


def _outproj_kernel(o_ref, w_ref, x_ref, g2_ref, wr_cat_ref, wr_hi_ref, x1_ref, h2_ref, aff_ref):
    x1 = x_ref[...] + _dot(o_ref[...], w_ref[...])
    _router_epilogue(x1, g2_ref, wr_cat_ref, wr_hi_ref, x1_ref, h2_ref, aff_ref)


def _row_specs(n, tm):
    out_shape = [jax.ShapeDtypeStruct((n, D_MODEL), F32),
                 jax.ShapeDtypeStruct((n, D_MODEL), BF16),
                 jax.ShapeDtypeStruct((n, N_EXPERTS), F32)]
    out_specs = [pl.BlockSpec((tm, D_MODEL), lambda i: (i, 0)),
                 pl.BlockSpec((tm, D_MODEL), lambda i: (i, 0)),
                 pl.BlockSpec((tm, N_EXPERTS), lambda i: (i, 0))]
    return out_shape, out_specs


def _outproj(o, w_out, x, g2, wr_cat, wr_hi, tm):
    n = x.shape[0]
    full = lambda a: pl.BlockSpec(a.shape, lambda i: (0,) * a.ndim)
    out_shape, out_specs = _row_specs(n, tm)
    return pl.pallas_call(
        _outproj_kernel, grid=(n // tm,),
        in_specs=[pl.BlockSpec((tm, D_MODEL), lambda i: (i, 0)), full(w_out),
                  pl.BlockSpec((tm, D_MODEL), lambda i: (i, 0)), full(g2), full(wr_cat), full(wr_hi)],
        out_specs=out_specs, out_shape=out_shape,
        compiler_params=_cparams(("arbitrary",)), name="mla_outproj_router",
    )(o, w_out, x, g2, wr_cat, wr_hi)


def _sgu_kernel(x_ref, g1_ref, w_in_ref, lng_ref, lnb_ref, ws_ref, bs_ref, w_out_ref,
                g2_ref, wr_cat_ref, wr_hi_ref, x1_ref, h2_ref, aff_ref):
    x = x_ref[...]
    tm = x.shape[0]
    h = _rms_rows(x, g1_ref[...]).astype(BF16)
    z = _dot(h, w_in_ref[...])
    z = 0.5 * z * (1.0 + lax.erf(z * np.float32(2.0 ** -0.5)))
    u = z[:, :SGU_HALF]
    v = z[:, SGU_HALF:]
    mu = jnp.mean(v, axis=-1, keepdims=True)
    vc = v - mu
    v = vc * lax.rsqrt(jnp.mean(vc * vc, axis=-1, keepdims=True) + EPS) * lng_ref[...] + lnb_ref[...]
    vb = v.astype(BF16)
    bs = bs_ref[...]
    rows = []
    for c in range(tm // CHUNK):
        cols = []
        for g in range(SGU_GROUPS):
            blk = vb[c * CHUNK:(c + 1) * CHUNK, g * LANES:(g + 1) * LANES]
            cols.append(_dot(ws_ref[g], blk))
        rows.append(jnp.concatenate(cols, axis=1) + bs)
    sv = jnp.concatenate(rows, axis=0)
    gated = (u * sv).astype(BF16)
    x1 = x + _dot(gated, w_out_ref[...])
    _router_epilogue(x1, g2_ref, wr_cat_ref, wr_hi_ref, x1_ref, h2_ref, aff_ref)


def _sgu(x, g1, w_in, lng, lnb, ws, bs_full, w_out, g2, wr_cat, wr_hi, tm):
    n = x.shape[0]
    full = lambda a: pl.BlockSpec(a.shape, lambda i: (0,) * a.ndim)
    ops = (g1, w_in, lng, lnb, ws, bs_full, w_out, g2, wr_cat, wr_hi)
    out_shape, out_specs = _row_specs(n, tm)
    return pl.pallas_call(
        _sgu_kernel, grid=(n // tm,),
        in_specs=[pl.BlockSpec((tm, D_MODEL), lambda i: (i, 0))] + [full(a) for a in ops],
        out_specs=out_specs, out_shape=out_shape,
        compiler_params=_cparams(("arbitrary",)), name="sgu_router",
    )(x, *ops)


def _select_kernel(aff_ref, lp_ref, start_ref, cnt_ref, lp_pair_ref, *, nt, cap):
    a = aff_ref[...]
    bits = pltpu.bitcast(a, I32)

    def body(i, cur):
        trial = cur | jnp.left_shift(jnp.int32(1), 30 - i)
        ge = jnp.where(bits >= trial, 1.0, 0.0)
        cnt = jnp.sum(jnp.sum(ge, axis=1, keepdims=True), axis=2, keepdims=True)
        return jnp.where(cnt >= cap, trial, cur)

    thr = lax.fori_loop(0, 31, body, jnp.zeros((N_EXPERTS, 1, 1), I32))
    gtf = jnp.where(bits > thr, 1.0, 0.0)
    eqf = jnp.where(bits == thr, 1.0, 0.0)
    c_gt = jnp.sum(jnp.sum(gtf, axis=1, keepdims=True), axis=2, keepdims=True)
    need = cap - c_gt

    r = lax.broadcasted_iota(I32, (COMB_TILE, COMB_TILE), 0)
    c = lax.broadcasted_iota(I32, (COMB_TILE, COMB_TILE), 1)
    incl_u = jnp.where(r <= c, 1.0, 0.0).astype(BF16)
    ones_u = jnp.ones((COMB_TILE, LANES), BF16)
    rt = lax.broadcasted_iota(I32, (nt, nt), 0)
    ct = lax.broadcasted_iota(I32, (nt, nt), 1)
    excl_l = jnp.where(rt > ct, 1.0, 0.0).astype(BF16)

    def tile_prefix(maskf):
        m2 = maskf.reshape(N_EXPERTS * nt, COMB_TILE).astype(BF16)
        incl = _dot(m2, incl_u)
        cnt = _dot(m2, ones_u)
        cnt_b = cnt.astype(BF16)
        starts = [_dot(excl_l, cnt_b[e * nt:(e + 1) * nt]) for e in range(N_EXPERTS)]
        return incl, jnp.concatenate(starts, axis=0), cnt

    eq_incl, eq_start, _ = tile_prefix(eqf)
    tie_rank = (eq_start + eq_incl).reshape(N_EXPERTS, nt, COMB_TILE) - eqf
    selm = gtf + eqf * jnp.where(tie_rank < need, 1.0, 0.0)
    incl, start, cnt = tile_prefix(selm)
    sel2 = selm.reshape(N_EXPERTS * nt, COMB_TILE) > 0.5
    lp_ref[...] = jnp.where(sel2, incl - 1.0, float(NOT_SELECTED)).astype(I32)
    start_ref[...] = start.astype(I32)
    cnt_ref[...] = cnt.astype(I32)
    odd = (lax.broadcasted_iota(I32, (N_EXPERTS * nt, COMB_TILE), 0) & 1) == 1
    before = jnp.where(odd, jnp.concatenate([jnp.zeros((1, COMB_TILE), F32), cnt[:-1]], axis=0), 0.0)
    lp_pair_ref[...] = jnp.where(sel2, incl - 1.0 + before, float(NOT_SELECTED)).astype(I32)


def _select(aff_t3, cap):
    e, nt, _ = aff_t3.shape
    return pl.pallas_call(
        functools.partial(_select_kernel, nt=nt, cap=cap),
        out_shape=[jax.ShapeDtypeStruct((e * nt, COMB_TILE), I32)] * 4,
        compiler_params=pltpu.CompilerParams(vmem_limit_bytes=VMEM_LIMIT), name="moe_select",
    )(aff_t3)


def _round_down(v):
    return (v >> ROW_ALIGN_SHIFT) << ROW_ALIGN_SHIFT


def _by_slab_length(short, short_rows, long_rows, fn):
    @pl.when(short)
    def _():
        fn(short_rows)

    @pl.when(jnp.logical_not(short))
    def _():
        fn(long_rows)


def _dispatch_kernel(start_sm, cnt_sm, short_sm, x_ref, lp_ref, xg_hbm, stage, carry, sems, *, nt, cap):
    t = pl.program_id(0)
    slot = t % 2

    def copy(s, e, row0, rows):
        dst = xg_hbm.at[e, pl.ds(pl.multiple_of(row0, ROW_ALIGN), rows), :]
        return pltpu.make_async_copy(stage.at[s, e, pl.ds(0, rows)], dst, sems.at[s])

    @pl.when(t == 0)
    def _():
        carry[...] = jnp.zeros(carry.shape, carry.dtype)
        stage[1] = jnp.zeros(stage.shape[1:], stage.dtype)
        for e in range(N_EXPERTS):
            copy(1, e, cap, DISP_LONG).start()
        for e in range(N_EXPERTS):
            copy(1, e, cap, DISP_LONG).wait()

    x = x_ref[...]
    rio = lax.broadcasted_iota(I32, (ROW_ALIGN, D_MODEL), 0)
    short_now = short_sm[t] == 1
    shifts = [start_sm[e * nt + t] - _round_down(start_sm[e * nt + t]) for e in range(N_EXPERTS)]

    def onehot_t(e, rows):
        jio = lax.broadcasted_iota(I32, (rows, DISP_TILE), 0)
        return jnp.where(lp_ref[0, e:e + 1, :] + shifts[e] == jio, 1.0, 0.0).astype(BF16)

    def place(e, slab, rows):
        head = jnp.where(rio < shifts[e], carry[e].astype(F32), slab[0:ROW_ALIGN])
        stage[slot, e, 0:ROW_ALIGN] = head.astype(BF16)
        stage[slot, e, ROW_ALIGN:rows] = slab[ROW_ALIGN:].astype(BF16)
        nxt = pl.multiple_of(_round_down(shifts[e] + cnt_sm[e * nt + t]), ROW_ALIGN)
        carry[e] = stage[slot, e, pl.ds(nxt, ROW_ALIGN)]

    def fill(rows):
        if rows == DISP_SHORT:
            slabs = _dot(jnp.concatenate([onehot_t(e, rows) for e in range(N_EXPERTS)], axis=0), x)
            for e in range(N_EXPERTS):
                place(e, slabs[e * rows:(e + 1) * rows], rows)
        else:
            for e in range(N_EXPERTS):
                place(e, _dot(onehot_t(e, rows), x), rows)

    by_length = functools.partial(_by_slab_length, short_rows=DISP_SHORT, long_rows=DISP_LONG)
    by_length(short_now, fn=fill)

    @pl.when(t > 0)
    def _():
        by_length(short_sm[t - 1] == 1, fn=lambda rows: [copy(1 - slot, e, 0, rows).wait() for e in range(N_EXPERTS)])

    by_length(short_now, fn=lambda rows: [copy(slot, e, _round_down(start_sm[e * nt + t]), rows).start()
                                          for e in range(N_EXPERTS)])

    @pl.when(t == nt - 1)
    def _():
        by_length(short_now, fn=lambda rows: [copy(slot, e, 0, rows).wait() for e in range(N_EXPERTS)])


def _dispatch(start_flat, cnt_flat, short_flat, h2, lp3, cap):
    n = h2.shape[0]
    nt = n // DISP_TILE
    grid_spec = pltpu.PrefetchScalarGridSpec(
        num_scalar_prefetch=3, grid=(nt,),
        in_specs=[pl.BlockSpec((DISP_TILE, D_MODEL), lambda t, *_: (t, 0)),
                  pl.BlockSpec((1, N_EXPERTS, DISP_TILE), lambda t, *_: (t, 0, 0))],
        out_specs=pl.BlockSpec(memory_space=pl.ANY),
        scratch_shapes=[pltpu.VMEM((2, N_EXPERTS, DISP_LONG, D_MODEL), BF16),
                        pltpu.VMEM((N_EXPERTS, ROW_ALIGN, D_MODEL), BF16),
                        pltpu.SemaphoreType.DMA((2,))])
    return pl.pallas_call(
        functools.partial(_dispatch_kernel, nt=nt, cap=cap), grid_spec=grid_spec,
        out_shape=jax.ShapeDtypeStruct((N_EXPERTS, cap + DISP_LONG, D_MODEL), BF16),
        compiler_params=_cparams(("arbitrary",)), name="moe_dispatch",
    )(start_flat, cnt_flat, short_flat, h2, lp3)


def _ffn_kernel(xg_ref, wg_ref, wu_ref, wd_ref, yg_ref):
    x = xg_ref[0]
    g = _dot(x, wg_ref[0])
    u = _dot(x, wu_ref[0])
    hid = (g * jax.nn.sigmoid(g) * u).astype(BF16)
    y = _dot(hid, wd_ref[0])
    yg_ref[0] = y.astype(BF16)


def _ffn(xg, wg, wu, wd, cap, rt):
    wspec = pl.BlockSpec((1, D_MODEL, EXPERT_FF), lambda e, r: (e, 0, 0))
    return pl.pallas_call(
        _ffn_kernel, grid=(N_EXPERTS, cap // rt),
        in_specs=[pl.BlockSpec((1, rt, D_MODEL), lambda e, r: (e, r, 0)), wspec, wspec,
                  pl.BlockSpec((1, EXPERT_FF, D_MODEL), lambda e, r: (e, 0, 0))],
        out_specs=pl.BlockSpec((1, rt, D_MODEL), lambda e, r: (e, r, 0)),
        out_shape=jax.ShapeDtypeStruct((N_EXPERTS, cap, D_MODEL), BF16),
        compiler_params=_cparams(("arbitrary", "arbitrary")), name="moe_ffn",
    )(xg, wg, wu, wd)


def _combine_kernel(start_sm, short_sm, x_ref, lpt_ref, aff_ref, yg_hbm, o_ref, slabs, sems, *, nt, cap):
    t = pl.program_id(0)
    slot = t % 2

    def row0(e, tt, rows):
        return jnp.minimum(_round_down(start_sm[e * nt + tt]), cap - rows)

    def copy(s, e, r0, rows):
        src = yg_hbm.at[e, pl.ds(pl.multiple_of(r0, ROW_ALIGN), rows), :]
        return pltpu.make_async_copy(src, slabs.at[s, e, pl.ds(0, rows)], sems.at[s])

    by_length = functools.partial(_by_slab_length, short_rows=COMB_SHORT, long_rows=COMB_LONG)

    def fetch(s, tt):
        by_length(short_sm[tt] == 1,
                  fn=lambda rows: [copy(s, e, row0(e, tt, rows), rows).start() for e in range(N_EXPERTS)])

    @pl.when(t == 0)
    def _():
        fetch(0, 0)

    @pl.when(t + 1 < nt)
    def _():
        fetch(1 - slot, t + 1)

    def expand_short(rows):
        for e in range(N_EXPERTS):
            copy(slot, e, 0, rows).wait()
        stacked = jnp.concatenate([slabs[slot, e, 0:rows] for e in range(N_EXPERTS)], axis=0)
        width = N_EXPERTS * rows
        first = lax.broadcasted_iota(I32, (N_EXPERTS, width), 0) * rows
        lane2 = lax.broadcasted_iota(I32, (N_EXPERTS, width), 1)
        spread = jnp.where((lane2 >= first) & (lane2 < first + rows), 1.0, 0.0).astype(BF16)
        g = aff_ref[...]
        g_hi = g.astype(BF16)
        g_lo = (g - g_hi.astype(F32)).astype(BF16)
        cols = jnp.concatenate([lpt_ref[...].astype(F32).astype(BF16), g_hi, g_lo], axis=0)
        wide = _dot(cols, spread)
        lane = lax.broadcasted_iota(I32, (1, width), 1)
        target = lane
        for e in range(N_EXPERTS):
            delta = start_sm[e * nt + t] - row0(e, t, rows)
            target = jnp.where((lane >= e * rows) & (lane < (e + 1) * rows), lane - (e * rows + delta), target)
        hit = wide[:COMB_TILE] == target.astype(F32)
        w = jnp.concatenate([jnp.where(hit, wide[COMB_TILE:2 * COMB_TILE], 0.0).astype(BF16),
                             jnp.where(hit, wide[2 * COMB_TILE:], 0.0).astype(BF16)], axis=0)
        both = _dot(w, stacked)
        o_ref[...] = x_ref[...] + both[:COMB_TILE] + both[COMB_TILE:]

    def expand_long(rows):
        for e in range(N_EXPERTS):
            copy(slot, e, 0, rows).wait()
        acc = x_ref[...]
        lane = lax.broadcasted_iota(I32, (COMB_TILE, rows), 1)
        for e in range(N_EXPERTS):
            delta = start_sm[e * nt + t] - row0(e, t, rows)
            onehot = jnp.where(lpt_ref[:, e:e + 1] + delta == lane, 1.0, 0.0).astype(BF16)
            acc = acc + aff_ref[:, e:e + 1] * _dot(onehot, slabs[slot, e])
        o_ref[...] = acc

    by_length(short_sm[t] == 1, fn=lambda rows: expand_short(rows) if rows == COMB_SHORT else expand_long(rows))


def _combine(start_flat, short_flat, x1, lp_t, aff, yg, cap):
    n = x1.shape[0]
    nt = n // COMB_TILE
    grid_spec = pltpu.PrefetchScalarGridSpec(
        num_scalar_prefetch=2, grid=(nt,),
        in_specs=[pl.BlockSpec((COMB_TILE, D_MODEL), lambda t, s, c: (t, 0)),
                  pl.BlockSpec((COMB_TILE, N_EXPERTS), lambda t, s, c: (t, 0)),
                  pl.BlockSpec((COMB_TILE, N_EXPERTS), lambda t, s, c: (t, 0)),
                  pl.BlockSpec(memory_space=pl.ANY)],
        out_specs=pl.BlockSpec((COMB_TILE, D_MODEL), lambda t, s, c: (t, 0)),
        scratch_shapes=[pltpu.VMEM((2, N_EXPERTS, COMB_LONG, D_MODEL), BF16),
                        pltpu.SemaphoreType.DMA((2,))])
    return pl.pallas_call(
        functools.partial(_combine_kernel, nt=nt, cap=cap), grid_spec=grid_spec,
        out_shape=jax.ShapeDtypeStruct((n, D_MODEL), F32),
        compiler_params=_cparams(("arbitrary",)), name="moe_combine",
    )(start_flat, short_flat, x1, lp_t, aff, yg)


def _moe(x1, h2, aff, wg, wu, wd, ffn_rows):
    n = x1.shape[0]
    nt, ntd = n // COMB_TILE, n // DISP_TILE
    cap = max(1, CAPACITY_FACTOR * n // N_EXPERTS)
    assert DISP_TILE == 2 * COMB_TILE and cap >= DISP_LONG and cap % ROW_ALIGN == 0
    lp, start, cnt, lp_pair = _select(aff.T.reshape(N_EXPERTS, nt, COMB_TILE), cap)
    start_c, cnt_c = start[:, 0].reshape(N_EXPERTS, nt), cnt[:, 0].reshape(N_EXPERTS, nt)
    start_d, cnt_d = start_c[:, ::2], cnt_c[:, ::2] + cnt_c[:, 1::2]

    def short_flags(st, ct, short_rows):
        return (jnp.max((st & (ROW_ALIGN - 1)) + ct, axis=0) <= short_rows - ROW_ALIGN).astype(I32)

    lp_t = lp.reshape(N_EXPERTS, n).T
    lp_tiles = lp_pair.reshape(N_EXPERTS, ntd, DISP_TILE).transpose(1, 0, 2)
    xg = _dispatch(start_d.reshape(-1), cnt_d.reshape(-1), short_flags(start_d, cnt_d, DISP_SHORT), h2, lp_tiles, cap)
    yg = _ffn(xg, wg, wu, wd, cap, min(ffn_rows, cap))
    return _combine(start_c.reshape(-1), short_flags(start_c, cnt_c, COMB_SHORT), x1, lp_t, aff, yg, cap)


def _rope_tables(S):
    pos = jnp.arange(S, dtype=F32)
    inv_freq = 1.0 / (ROPE_THETA ** (jnp.arange(0, ROPE_DIM, 2, dtype=F32) / ROPE_DIM))
    ang = pos[:, None] * inv_freq[None, :]
    return jnp.cos(ang), jnp.sin(ang)


def _prep_mla_weights(w_in, gq, gkv, w_uq, w_ukv, qn_g, kn_g):
    pad = lambda a, lo, hi: jnp.pad(a, ((0, 0), (lo, hi)))
    pe = w_in[:, Q_LORA + KV_LORA:]
    pe_sw = jnp.concatenate([pe[:, HALF_ROPE:], pe[:, :HALF_ROPE]], axis=1)
    w_in_p = jnp.concatenate([w_in[:, :Q_LORA + KV_LORA], pad(pe, NOPE_DIM, QK_PAD - QK_HEAD),
                              pad(pe_sw, NOPE_DIM, QK_PAD - QK_HEAD)], axis=1).astype(BF16)
    w_uqT = w_uq.T.astype(BF16)
    ukv = w_ukv.reshape(KV_LORA, MLA_HEADS, NOPE_DIM + V_HEAD)
    w_uk = jnp.pad(ukv[:, :, :NOPE_DIM], ((0, 0), (0, 0), (0, QK_PAD - NOPE_DIM)))
    w_uk = w_uk.reshape(KV_LORA, MLA_HEADS * QK_PAD).astype(BF16)
    w_uvT = ukv[:, :, NOPE_DIM:].reshape(KV_LORA, MLA_HEADS * V_HEAD).T.astype(BF16)
    gqc = (qn_g * np.float32(QK_HEAD ** -0.5 * np.log2(np.e))).reshape(QK_HEAD, 1)
    kb = (jnp.max(jnp.abs(kn_g)) * np.float32(QK_HEAD ** 0.5 * BOUND_MARGIN)).reshape(1, 1)
    return (w_in_p, gq.reshape(1, Q_LORA), gkv.reshape(1, KV_LORA), w_uqT, w_uk, w_uvT, gqc, kb)


def _score_bound_log2(qn_g, kn_g):
    return (jnp.max(jnp.abs(qn_g)) * jnp.max(jnp.abs(kn_g))
            * np.float32(QK_HEAD * QK_HEAD ** -0.5 * np.log2(np.e) * BOUND_MARGIN * BOUND_MARGIN))


def _prep_k_tables(cos, sin, kn_g):
    S = cos.shape[0]
    z = lambda w: jnp.zeros((S, w), F32)
    cos_l = jnp.concatenate([jnp.ones((S, NOPE_DIM), F32), cos, cos, z(QK_PAD - QK_HEAD)], axis=1)
    sin_l = jnp.concatenate([z(NOPE_DIM), -sin, sin, z(QK_PAD - QK_HEAD)], axis=1)
    g = jnp.pad(kn_g, (0, QK_PAD - QK_HEAD))
    g_sw = jnp.concatenate([jnp.zeros((NOPE_DIM,), F32), kn_g[NOPE_DIM + HALF_ROPE:],
                            kn_g[NOPE_DIM:NOPE_DIM + HALF_ROPE], jnp.zeros((QK_PAD - QK_HEAD,), F32)])
    return cos_l * g[None, :], sin_l * g_sw[None, :]


def _router_weights(w_r):
    hi = w_r.astype(BF16)
    lo = (w_r - hi.astype(F32)).astype(BF16)
    return jnp.concatenate([hi, lo], axis=1), hi


def _trunk(x3, p, tiles):
    B, S, _ = x3.shape
    n = B * S
    cos, sin = _rope_tables(S)
    mw = _prep_mla_weights(p['mla_w_in'][0], p['mla_q_lora_g'][0], p['mla_kv_lora_g'][0], p['mla_w_uq'][0],
                           p['mla_w_ukv'][0], p['mla_q_norm_g'][0], p['mla_k_norm_g'][0])
    kcos, ksin = _prep_k_tables(cos, sin, p['mla_k_norm_g'][0])
    qT, k, vT = _mla_prep(x3, p['norm1_g'][0].reshape(1, D_MODEL), mw, (kcos, ksin, cos.T, sin.T), tiles['tk'])
    fits = 2.0 * _score_bound_log2(p['mla_q_norm_g'][0], p['mla_k_norm_g'][0]) <= MAX_SHIFTED_SCORE_RANGE
    o = lax.cond(fits, lambda: _attention(qT, k, vT, tiles['tq'], True),
                 lambda: _attention(qT, k, vT, tiles['tq'], False)).reshape(n, D_MODEL)
    x = x3.reshape(n, D_MODEL)
    wr_cat, wr_hi = _router_weights(p['moe_w_router'][0])
    x1, h2, aff = _outproj(o, p['mla_w_out'][0].astype(BF16), x, p['norm2_g'][0].reshape(1, D_MODEL),
                           wr_cat, wr_hi, tiles['tm'])
    x = _moe(x1, h2, aff, p['wg'][0], p['wu'][0], p['wd'][0], tiles['ffn_rows'])
    wr_cat, wr_hi = _router_weights(p['moe_w_router'][1])
    bs_full = jnp.repeat(p['sgu_b_s'][0].T, LANES, axis=1)
    x1, h2, aff = _sgu(x, p['norm1_g'][1].reshape(1, D_MODEL), p['sgu_w_in'][0].astype(BF16),
                       p['sgu_ln_g'][0].reshape(1, SGU_HALF), p['sgu_ln_b'][0].reshape(1, SGU_HALF),
                       p['sgu_w_s'][0].astype(BF16), bs_full, p['sgu_w_out'][0].astype(BF16),
                       p['norm2_g'][1].reshape(1, D_MODEL), wr_cat, wr_hi, tiles['tm_sgu'])
    x = _moe(x1, h2, aff, p['wg'][1], p['wu'][1], p['wd'][1], tiles['ffn_rows'])
    return x.reshape(B, S, D_MODEL)


TILES = dict(tk=256, tq=1024, tm=512, tm_sgu=256, ffn_rows=512)


def kernel(x_prompt, x_sample, norm1_g, norm2_g, mla_w_in, mla_q_lora_g, mla_kv_lora_g, mla_w_uq, mla_w_ukv, mla_q_norm_g, mla_k_norm_g, mla_w_out, sgu_w_in, sgu_ln_g, sgu_ln_b, sgu_w_s, sgu_b_s, sgu_w_out, moe_w_router, moe_w_gate, moe_w_up, moe_w_down):
    p = dict(norm1_g=norm1_g, norm2_g=norm2_g, mla_w_in=mla_w_in, mla_q_lora_g=mla_q_lora_g,
             mla_kv_lora_g=mla_kv_lora_g, mla_w_uq=mla_w_uq, mla_w_ukv=mla_w_ukv, mla_q_norm_g=mla_q_norm_g,
             mla_k_norm_g=mla_k_norm_g, mla_w_out=mla_w_out, sgu_w_in=sgu_w_in, sgu_ln_g=sgu_ln_g,
             sgu_ln_b=sgu_ln_b, sgu_w_s=sgu_w_s, sgu_b_s=sgu_b_s, sgu_w_out=sgu_w_out, moe_w_router=moe_w_router,
             wg=moe_w_gate.astype(BF16), wu=moe_w_up.astype(BF16), wd=moe_w_down.astype(BF16))
    return (_trunk(x_prompt, p, TILES), _trunk(x_sample, p, TILES))
```

```python
import functools

import jax
import jax.numpy as jnp
import numpy as np
from jax import lax
from jax.experimental import pallas as pl
from jax.experimental.pallas import tpu as pltpu

F32 = jnp.float32
BF16 = jnp.bfloat16
I32 = jnp.int32

D_MODEL = 1024
MLA_HEADS = 16
Q_LORA = 256
KV_LORA = 128
NOPE_DIM = 64
ROPE_DIM = 32
HALF_ROPE = ROPE_DIM // 2
QK_HEAD = NOPE_DIM + ROPE_DIM
V_HEAD = 64
ROPE_THETA = 10000.0
CHUNK = 128
SGU_HALF = D_MODEL
SGU_GROUPS = 8
N_EXPERTS = 16
EXPERT_FF = 1024
CAPACITY_FACTOR = 2
EPS = 1e-6

LANES = 128
QK_PAD = 128
V_ROWS = 80
COMB_TILE = 128
DISP_TILE = 256
ROW_ALIGN = 16
ROW_ALIGN_SHIFT = 4
COMB_LONG, COMB_SHORT = COMB_TILE + ROW_ALIGN, 64
DISP_LONG, DISP_SHORT = DISP_TILE + ROW_ALIGN, 96
NOT_SELECTED = -1024
BOUND_MARGIN = 1.0 + 2.0 ** -6
MAX_SHIFTED_SCORE_RANGE = 100.0
HEADS_PER_STEP = 2
KT_PER_TRIP = 16

VMEM_LIMIT = 52 * 1024 * 1024


def _cparams(sem):
    return pltpu.CompilerParams(dimension_semantics=sem, vmem_limit_bytes=VMEM_LIMIT)


def _dot(a, b):
    return jnp.dot(a, b, preferred_element_type=F32)


def _dot_nt(a, b):
    return lax.dot_general(a, b, (((1,), (1,)), ((), ())), preferred_element_type=F32)


def _rms_rows(x, g):
    return x * lax.rsqrt(jnp.mean(x * x, axis=-1, keepdims=True) + EPS) * g


def _mla_prep_kernel(x_ref, g1_ref, w_in_ref, gq_ref, gkv_ref, w_uqT_ref, w_uk_ref, w_uvT_ref,
                     gqc_ref, kb_ref, kcos_ref, ksin_ref, cosT_ref, sinT_ref,
                     qT_ref, k_ref, vT_ref):
    x = x_ref[...]
    h = _rms_rows(x, g1_ref[...]).astype(BF16)
    proj = _dot(h, w_in_ref[...])
    cq = _rms_rows(proj[:, :Q_LORA], gq_ref[...]).astype(BF16)
    ckv = _rms_rows(proj[:, Q_LORA:Q_LORA + KV_LORA], gkv_ref[...]).astype(BF16)
    kpe = proj[:, 384:512]
    kpe_sw = proj[:, 512:640]
    tm = x.shape[0]
    n_kt, tk = k_ref.shape[2], k_ref.shape[3]

    qT = _dot_nt(w_uqT_ref[...], cq)
    cosT = cosT_ref[...]
    sinT = sinT_ref[...]
    gqc = gqc_ref[...]
    kbound = kb_ref[...]
    brow = lax.broadcasted_iota(I32, (QK_PAD - QK_HEAD, tm), 0)
    for hd in range(MLA_HEADS):
        qh = qT[hd * QK_HEAD:(hd + 1) * QK_HEAD, :]
        r = lax.rsqrt(jnp.sum(qh * qh, axis=0, keepdims=True) * (1.0 / QK_HEAD) + EPS)
        qn = qh * gqc * r
        x1 = qn[NOPE_DIM:NOPE_DIM + HALF_ROPE]
        x2 = qn[NOPE_DIM + HALF_ROPE:QK_HEAD]
        qb = jnp.concatenate([qn[:NOPE_DIM], x1 * cosT - x2 * sinT, x2 * cosT + x1 * sinT], axis=0).astype(BF16)
        qT_ref[0, hd, 0:QK_HEAD, :] = qb
        qf = qb.astype(F32)
        bound = jnp.sqrt(jnp.sum(qf * qf, axis=0, keepdims=True)) * kbound
        qT_ref[0, hd, QK_HEAD:QK_PAD, :] = jnp.where(brow == 0, -bound, 0.0).astype(BF16)

    knope = _dot(ckv, w_uk_ref[...])
    kcos = kcos_ref[...]
    ksin = ksin_ref[...]
    one_lane = lax.broadcasted_iota(I32, (tm, QK_PAD), 1) == QK_HEAD
    for hd in range(MLA_HEADS):
        kh = knope[:, hd * QK_PAD:(hd + 1) * QK_PAD] + kpe
        r = lax.rsqrt(jnp.sum(kh * kh, axis=-1, keepdims=True) * (1.0 / QK_HEAD) + EPS)
        kb = jnp.where(one_lane, 1.0, (kh * kcos + kpe_sw * ksin) * r).astype(BF16)
        for j in range(n_kt):
            k_ref[0, hd, j] = kb[j * tk:(j + 1) * tk]

    vT = _dot_nt(w_uvT_ref[...], ckv)
    row = lax.broadcasted_iota(I32, (V_ROWS - V_HEAD, tk), 0)
    tail = jnp.where(row == 0, 1.0, 0.0).astype(BF16)
    for hd in range(MLA_HEADS):
        vb = vT[hd * V_HEAD:(hd + 1) * V_HEAD, :].astype(BF16)
        for j in range(n_kt):
            vT_ref[0, hd, j, 0:V_HEAD, :] = vb[:, j * tk:(j + 1) * tk]
            vT_ref[0, hd, j, V_HEAD:V_ROWS, :] = tail


def _mla_prep(x3, g1, wts, tabs, tm, tk):
    B, S, _ = x3.shape
    nb, per = S // tm, tm // tk
    full = lambda a: pl.BlockSpec(a.shape, lambda b, i: (0,) * a.ndim)
    in_specs = [pl.BlockSpec((None, tm, D_MODEL), lambda b, i: (b, i, 0)), full(g1)]
    in_specs += [full(w) for w in wts]
    in_specs += [pl.BlockSpec((tm, LANES), lambda b, i: (i, 0)),
                 pl.BlockSpec((tm, LANES), lambda b, i: (i, 0)),
                 pl.BlockSpec((HALF_ROPE, tm), lambda b, i: (0, i)),
                 pl.BlockSpec((HALF_ROPE, tm), lambda b, i: (0, i))]
    out_shape = [jax.ShapeDtypeStruct((B, MLA_HEADS, QK_PAD, S), BF16),
                 jax.ShapeDtypeStruct((B, MLA_HEADS, S // tk, tk, QK_PAD), BF16),
                 jax.ShapeDtypeStruct((B, MLA_HEADS, S // tk, V_ROWS, tk), BF16)]
    out_specs = [pl.BlockSpec((1, MLA_HEADS, QK_PAD, tm), lambda b, i: (b, 0, 0, i)),
                 pl.BlockSpec((1, MLA_HEADS, per, tk, QK_PAD), lambda b, i: (b, 0, i, 0, 0)),
                 pl.BlockSpec((1, MLA_HEADS, per, V_ROWS, tk), lambda b, i: (b, 0, i, 0, 0))]
    return pl.pallas_call(
        _mla_prep_kernel, grid=(B, nb), in_specs=in_specs, out_specs=out_specs, out_shape=out_shape,
        compiler_params=_cparams(("arbitrary", "arbitrary")), name="mla_prep",
    )(x3, g1, *wts, *tabs)


def _attn_kernel(qT_ref, k_ref, vT_ref, o_ref, *, n_kt):
    tq = qT_ref.shape[3]
    qTs = [qT_ref[0, hh] for hh in range(HEADS_PER_STEP)]

    def body(kt, carry):
        out = []
        for hh in range(HEADS_PER_STEP):
            m, acc = carry[hh]
            sT = _dot(k_ref[0, hh, kt], qTs[hh])
            m_new = jnp.maximum(m, jnp.max(sT, axis=0, keepdims=True))
            alpha = jnp.exp2(m - m_new)
            pT = jnp.exp2(sT - m_new).astype(BF16)
            acc = alpha * acc + _dot(vT_ref[0, hh, kt], pT)
            out.append((m_new, acc))
        return tuple(out)

    init = tuple((jnp.full((1, tq), -jnp.inf, F32), jnp.zeros((V_ROWS, tq), F32))
                 for _ in range(HEADS_PER_STEP))
    res = lax.fori_loop(0, n_kt, body, init, unroll=2)
    outs = [acc[:V_HEAD] / acc[V_HEAD:V_HEAD + 1] for _, acc in res]
    o_ref[0] = jnp.concatenate(outs, axis=0).T.astype(o_ref.dtype)


def _attn_bounded_kernel(qT_ref, k_ref, vT_ref, o_ref, *, n_kt):
    tq = qT_ref.shape[3]
    qTs = [qT_ref[0, hh] for hh in range(HEADS_PER_STEP)]

    def scores(kt):
        return [_dot(k_ref[0, hh, kt], qTs[hh]) for hh in range(HEADS_PER_STEP)]

    def accumulate(accs, kt, sTs):
        return tuple(accs[hh] + _dot(vT_ref[0, hh, kt], jnp.exp2(sTs[hh]).astype(BF16))
                     for hh in range(HEADS_PER_STEP))

    def body(j, accs):
        kt = j * per_trip
        sTs = scores(kt)
        for u in range(1, per_trip):
            nxt = scores(kt + u)
            accs = accumulate(accs, kt + u - 1, sTs)
            sTs = nxt
        return accumulate(accs, kt + per_trip - 1, sTs)

    per_trip = min(KT_PER_TRIP, n_kt)
    assert n_kt % per_trip == 0
    accs = lax.fori_loop(0, n_kt // per_trip, body,
                         tuple(jnp.zeros((V_ROWS, tq), F32) for _ in range(HEADS_PER_STEP)))
    outs = [acc[:V_HEAD] / acc[V_HEAD:V_HEAD + 1] for acc in accs]
    o_ref[0] = jnp.concatenate(outs, axis=0).T.astype(o_ref.dtype)


def _attention(qT, k, vT, tq, bounded):
    B, H, _, S = qT.shape
    n_kt, tk = k.shape[2], k.shape[3]
    hp = H // HEADS_PER_STEP
    return pl.pallas_call(
        functools.partial(_attn_bounded_kernel if bounded else _attn_kernel, n_kt=n_kt),
        grid=(B, hp, S // tq),
        in_specs=[pl.BlockSpec((1, HEADS_PER_STEP, QK_PAD, tq), lambda b, h, i: (b, h, 0, i)),
                  pl.BlockSpec((1, HEADS_PER_STEP, n_kt, tk, QK_PAD), lambda b, h, i: (b, h, 0, 0, 0)),
                  pl.BlockSpec((1, HEADS_PER_STEP, n_kt, V_ROWS, tk), lambda b, h, i: (b, h, 0, 0, 0))],
        out_specs=pl.BlockSpec((1, tq, HEADS_PER_STEP * V_HEAD), lambda b, h, i: (b, i, h)),
        out_shape=jax.ShapeDtypeStruct((B, S, H * V_HEAD), BF16),
        compiler_params=_cparams(("arbitrary", "arbitrary", "arbitrary")),
        name="mla_attention_bounded" if bounded else "mla_attention",
    )(qT, k, vT)


def _router_epilogue(x1, g2_ref, wr_cat_ref, wr_hi_ref, x1_ref, h2_ref, aff_ref):
    x1_ref[...] = x1
    h2 = _rms_rows(x1, g2_ref[...])
    h2_hi = h2.astype(BF16)
    h2_lo = (h2 - h2_hi.astype(F32)).astype(BF16)
    h2_ref[...] = h2_hi
    both = _dot(h2_hi, wr_cat_ref[...])
    logits = both[:, :N_EXPERTS] + both[:, N_EXPERTS:] + _dot(h2_lo, wr_hi_ref[...])
    ex = jnp.exp(logits - jnp.max(logits, axis=-1, keepdims=True))
    aff_ref[...] = ex / jnp.sum(ex, axis=-1, keepdims=True)


def _outproj_kernel(o_ref, w_ref, x_ref, g2_ref, wr_cat_ref, wr_hi_ref, x1_ref, h2_ref, aff_ref):
    x1 = x_ref[...] + _dot(o_ref[...], w_ref[...])
    _router_epilogue(x1, g2_ref, wr_cat_ref, wr_hi_ref, x1_ref, h2_ref, aff_ref)


def _row_specs(n, tm):
    out_shape = [jax.ShapeDtypeStruct((n, D_MODEL), F32),
                 jax.ShapeDtypeStruct((n, D_MODEL), BF16),
                 jax.ShapeDtypeStruct((n, N_EXPERTS), F32)]
    out_specs = [pl.BlockSpec((tm, D_MODEL), lambda i: (i, 0)),
                 pl.BlockSpec((tm, D_MODEL), lambda i: (i, 0)),
                 pl.BlockSpec((tm, N_EXPERTS), lambda i: (i, 0))]
    return out_shape, out_specs


def _outproj(o, w_out, x, g2, wr_cat, wr_hi, tm):
    n = x.shape[0]
    full = lambda a: pl.BlockSpec(a.shape, lambda i: (0,) * a.ndim)
    out_shape, out_specs = _row_specs(n, tm)
    return pl.pallas_call(
        _outproj_kernel, grid=(n // tm,),
        in_specs=[pl.BlockSpec((tm, D_MODEL), lambda i: (i, 0)), full(w_out),
                  pl.BlockSpec((tm, D_MODEL), lambda i: (i, 0)), full(g2), full(wr_cat), full(wr_hi)],
        out_specs=out_specs, out_shape=out_shape,
        compiler_params=_cparams(("arbitrary",)), name="mla_outproj_router",
    )(o, w_out, x, g2, wr_cat, wr_hi)


def _sgu_kernel(x_ref, g1_ref, w_in_ref, lng_ref, lnb_ref, ws_ref, bs_ref, w_out_ref,
                g2_ref, wr_cat_ref, wr_hi_ref, x1_ref, h2_ref, aff_ref):
    x = x_ref[...]
    tm = x.shape[0]
    h = _rms_rows(x, g1_ref[...]).astype(BF16)
    z = _dot(h, w_in_ref[...])
    z = 0.5 * z * (1.0 + lax.erf(z * np.float32(2.0 ** -0.5)))
    u = z[:, :SGU_HALF]
    v = z[:, SGU_HALF:]
    mu = jnp.mean(v, axis=-1, keepdims=True)
    vc = v - mu
    v = vc * lax.rsqrt(jnp.mean(vc * vc, axis=-1, keepdims=True) + EPS) * lng_ref[...] + lnb_ref[...]
    vb = v.astype(BF16)
    bs = bs_ref[...]
    n_chunks = tm // CHUNK
    mixed = []
    for g in range(SGU_GROUPS):
        blk = jnp.concatenate([vb[c * CHUNK:(c + 1) * CHUNK, g * LANES:(g + 1) * LANES]
                               for c in range(n_chunks)], axis=1)
        mixed.append(_dot(ws_ref[g], blk))
    sv = jnp.concatenate(
        [jnp.concatenate([mixed[g][:, c * LANES:(c + 1) * LANES] for g in range(SGU_GROUPS)], axis=1) + bs
         for c in range(n_chunks)], axis=0)
    gated = (u * sv).astype(BF16)
    x1 = x + _dot(gated, w_out_ref[...])
    _router_epilogue(x1, g2_ref, wr_cat_ref, wr_hi_ref, x1_ref, h2_ref, aff_ref)


def _sgu(x, g1, w_in, lng, lnb, ws, bs_full, w_out, g2, wr_cat, wr_hi, tm):
    n = x.shape[0]
    full = lambda a: pl.BlockSpec(a.shape, lambda i: (0,) * a.ndim)
    ops = (g1, w_in, lng, lnb, ws, bs_full, w_out, g2, wr_cat, wr_hi)
    out_shape, out_specs = _row_specs(n, tm)
    return pl.pallas_call(
        _sgu_kernel, grid=(n // tm,),
        in_specs=[pl.BlockSpec((tm, D_MODEL), lambda i: (i, 0))] + [full(a) for a in ops],
        out_specs=out_specs, out_shape=out_shape,
        compiler_params=_cparams(("arbitrary",)), name="sgu_router",
    )(x, *ops)


def _select_kernel(aff_ref, lp_ref, start_ref, cnt_ref, lp_pair_ref, *, nt, cap):
    a = aff_ref[...]
    bits = pltpu.bitcast(a, I32)

    def body(i, cur):
        trial = cur | jnp.left_shift(jnp.int32(1), 30 - i)
        ge = jnp.where(bits >= trial, 1.0, 0.0)
        cnt = jnp.sum(jnp.sum(ge, axis=1, keepdims=True), axis=2, keepdims=True)
        return jnp.where(cnt >= cap, trial, cur)

    thr = lax.fori_loop(0, 31, body, jnp.zeros((N_EXPERTS, 1, 1), I32))
    gtf = jnp.where(bits > thr, 1.0, 0.0)
    eqf = jnp.where(bits == thr, 1.0, 0.0)
    c_gt = jnp.sum(jnp.sum(gtf, axis=1, keepdims=True), axis=2, keepdims=True)
    need = cap - c_gt

    r = lax.broadcasted_iota(I32, (COMB_TILE, COMB_TILE), 0)
    c = lax.broadcasted_iota(I32, (COMB_TILE, COMB_TILE), 1)
    incl_u = jnp.where(r <= c, 1.0, 0.0).astype(BF16)
    ones_u = jnp.ones((COMB_TILE, LANES), BF16)
    rt = lax.broadcasted_iota(I32, (nt, nt), 0)
    ct = lax.broadcasted_iota(I32, (nt, nt), 1)
    excl_l = jnp.where(rt > ct, 1.0, 0.0).astype(BF16)

    def tile_prefix(maskf):
        m2 = maskf.reshape(N_EXPERTS * nt, COMB_TILE).astype(BF16)
        incl = _dot(m2, incl_u)
        cnt = _dot(m2, ones_u)
        cnt_b = cnt.astype(BF16)
        starts = [_dot(excl_l, cnt_b[e * nt:(e + 1) * nt]) for e in range(N_EXPERTS)]
        return incl, jnp.concatenate(starts, axis=0), cnt

    eq_incl, eq_start, _ = tile_prefix(eqf)
    tie_rank = (eq_start + eq_incl).reshape(N_EXPERTS, nt, COMB_TILE) - eqf
    selm = gtf + eqf * jnp.where(tie_rank < need, 1.0, 0.0)
    incl, start, cnt = tile_prefix(selm)
    sel2 = selm.reshape(N_EXPERTS * nt, COMB_TILE) > 0.5
    lp_ref[...] = jnp.where(sel2, incl - 1.0, float(NOT_SELECTED)).astype(I32)
    start_ref[...] = start.astype(I32)
    cnt_ref[...] = cnt.astype(I32)
    odd = (lax.broadcasted_iota(I32, (N_EXPERTS * nt, COMB_TILE), 0) & 1) == 1
    before = jnp.where(odd, jnp.concatenate([jnp.zeros((1, COMB_TILE), F32), cnt[:-1]], axis=0), 0.0)
    lp_pair_ref[...] = jnp.where(sel2, incl - 1.0 + before, float(NOT_SELECTED)).astype(I32)


def _select(aff_t3, cap):
    e, nt, _ = aff_t3.shape
    return pl.pallas_call(
        functools.partial(_select_kernel, nt=nt, cap=cap),
        out_shape=[jax.ShapeDtypeStruct((e * nt, COMB_TILE), I32)] * 4,
        compiler_params=pltpu.CompilerParams(vmem_limit_bytes=VMEM_LIMIT), name="moe_select",
    )(aff_t3)


def _round_down(v):
    return (v >> ROW_ALIGN_SHIFT) << ROW_ALIGN_SHIFT


def _by_slab_length(short, short_rows, long_rows, fn):
    @pl.when(short)
    def _():
        fn(short_rows)

    @pl.when(jnp.logical_not(short))
    def _():
        fn(long_rows)


def _dispatch_kernel(start_sm, cnt_sm, short_sm, x_ref, lp_ref, xg_hbm, stage, carry, sems, *, nt, cap):
    t = pl.program_id(0)
    slot = t % 2

    def copy(s, e, row0, rows):
        dst = xg_hbm.at[e, pl.ds(pl.multiple_of(row0, ROW_ALIGN), rows), :]
        return pltpu.make_async_copy(stage.at[s, e, pl.ds(0, rows)], dst, sems.at[s])

    @pl.when(t == 0)
    def _():
        carry[...] = jnp.zeros(carry.shape, carry.dtype)
        stage[1] = jnp.zeros(stage.shape[1:], stage.dtype)
        for e in range(N_EXPERTS):
            copy(1, e, cap, DISP_LONG).start()
        for e in range(N_EXPERTS):
            copy(1, e, cap, DISP_LONG).wait()

    x = x_ref[...]
    rio = lax.broadcasted_iota(I32, (ROW_ALIGN, D_MODEL), 0)
    short_now = short_sm[t] == 1
    shifts = [start_sm[e * nt + t] - _round_down(start_sm[e * nt + t]) for e in range(N_EXPERTS)]

    def onehot_t(e, rows):
        jio = lax.broadcasted_iota(I32, (rows, DISP_TILE), 0)
        return jnp.where(lp_ref[0, e:e + 1, :] + shifts[e] == jio, 1.0, 0.0).astype(BF16)

    def place(e, slab, rows):
        head = jnp.where(rio < shifts[e], carry[e].astype(F32), slab[0:ROW_ALIGN])
        stage[slot, e, 0:ROW_ALIGN] = head.astype(BF16)
        stage[slot, e, ROW_ALIGN:rows] = slab[ROW_ALIGN:].astype(BF16)
        nxt = pl.multiple_of(_round_down(shifts[e] + cnt_sm[e * nt + t]), ROW_ALIGN)
        carry[e] = stage[slot, e, pl.ds(nxt, ROW_ALIGN)]

    def fill(rows):
        if rows == DISP_SHORT:
            slabs = _dot(jnp.concatenate([onehot_t(e, rows) for e in range(N_EXPERTS)], axis=0), x)
            for e in range(N_EXPERTS):
                place(e, slabs[e * rows:(e + 1) * rows], rows)
        else:
            for e in range(N_EXPERTS):
                place(e, _dot(onehot_t(e, rows), x), rows)

    by_length = functools.partial(_by_slab_length, short_rows=DISP_SHORT, long_rows=DISP_LONG)
    by_length(short_now, fn=fill)

    @pl.when(t > 0)
    def _():
        by_length(short_sm[t - 1] == 1, fn=lambda rows: [copy(1 - slot, e, 0, rows).wait() for e in range(N_EXPERTS)])

    by_length(short_now, fn=lambda rows: [copy(slot, e, _round_down(start_sm[e * nt + t]), rows).start()
                                          for e in range(N_EXPERTS)])

    @pl.when(t == nt - 1)
    def _():
        by_length(short_now, fn=lambda rows: [copy(slot, e, 0, rows).wait() for e in range(N_EXPERTS)])


def _dispatch(start_flat, cnt_flat, short_flat, h2, lp3, cap):
    n = h2.shape[0]
    nt = n // DISP_TILE
    grid_spec = pltpu.PrefetchScalarGridSpec(
        num_scalar_prefetch=3, grid=(nt,),
        in_specs=[pl.BlockSpec((DISP_TILE, D_MODEL), lambda t, *_: (t, 0)),
                  pl.BlockSpec((1, N_EXPERTS, DISP_TILE), lambda t, *_: (t, 0, 0))],
        out_specs=pl.BlockSpec(memory_space=pl.ANY),
        scratch_shapes=[pltpu.VMEM((2, N_EXPERTS, DISP_LONG, D_MODEL), BF16),
                        pltpu.VMEM((N_EXPERTS, ROW_ALIGN, D_MODEL), BF16),
                        pltpu.SemaphoreType.DMA((2,))])
    return pl.pallas_call(
        functools.partial(_dispatch_kernel, nt=nt, cap=cap), grid_spec=grid_spec,
        out_shape=jax.ShapeDtypeStruct((N_EXPERTS, cap + DISP_LONG, D_MODEL), BF16),
        compiler_params=_cparams(("arbitrary",)), name="moe_dispatch",
    )(start_flat, cnt_flat, short_flat, h2, lp3)


def _ffn_kernel(xg_ref, wg_ref, wu_ref, wd_ref, yg_ref, wg_b, wu_b, wd_b):
    @pl.when(pl.program_id(1) == 0)
    def _():
        wg_b[...] = wg_ref[0, 0].astype(BF16)
        wu_b[...] = wu_ref[0, 0].astype(BF16)
        wd_b[...] = wd_ref[0, 0].astype(BF16)

    x = xg_ref[0]
    g = _dot(x, wg_b[...])
    u = _dot(x, wu_b[...])
    hid = (g * jax.nn.sigmoid(g) * u).astype(BF16)
    y = _dot(hid, wd_b[...])
    yg_ref[0] = y.astype(BF16)


def _ffn(xg, wg, wu, wd, layer, cap, rt):
    wspec = pl.BlockSpec((1, 1, D_MODEL, EXPERT_FF), lambda e, r: (layer, e, 0, 0))
    return pl.pallas_call(
        _ffn_kernel, grid=(N_EXPERTS, cap // rt),
        in_specs=[pl.BlockSpec((1, rt, D_MODEL), lambda e, r: (e, r, 0)), wspec, wspec,
                  pl.BlockSpec((1, 1, EXPERT_FF, D_MODEL), lambda e, r: (layer, e, 0, 0))],
        out_specs=pl.BlockSpec((1, rt, D_MODEL), lambda e, r: (e, r, 0)),
        out_shape=jax.ShapeDtypeStruct((N_EXPERTS, cap, D_MODEL), BF16),
        scratch_shapes=[pltpu.VMEM((D_MODEL, EXPERT_FF), BF16), pltpu.VMEM((D_MODEL, EXPERT_FF), BF16),
                        pltpu.VMEM((EXPERT_FF, D_MODEL), BF16)],
        compiler_params=_cparams(("arbitrary", "arbitrary")), name="moe_ffn",
    )(xg, wg, wu, wd)


def _combine_kernel(start_sm, short_sm, x_ref, lpt_ref, aff_ref, yg_hbm, o_ref, slabs, sems, *, nt, cap):
    t = pl.program_id(0)
    slot = t % 2

    def row0(e, tt, rows):
        return jnp.minimum(_round_down(start_sm[e * nt + tt]), cap - rows)

    def copy(s, e, r0, rows):
        src = yg_hbm.at[e, pl.ds(pl.multiple_of(r0, ROW_ALIGN), rows), :]
        return pltpu.make_async_copy(src, slabs.at[s, e, pl.ds(0, rows)], sems.at[s])

    by_length = functools.partial(_by_slab_length, short_rows=COMB_SHORT, long_rows=COMB_LONG)

    def fetch(s, tt):
        by_length(short_sm[tt] == 1,
                  fn=lambda rows: [copy(s, e, row0(e, tt, rows), rows).start() for e in range(N_EXPERTS)])

    @pl.when(t == 0)
    def _():
        fetch(0, 0)

    @pl.when(t + 1 < nt)
    def _():
        fetch(1 - slot, t + 1)

    def expand_short(rows):
        for e in range(N_EXPERTS):
            copy(slot, e, 0, rows).wait()
        stacked = jnp.concatenate([slabs[slot, e, 0:rows] for e in range(N_EXPERTS)], axis=0)
        width = N_EXPERTS * rows
        first = lax.broadcasted_iota(I32, (N_EXPERTS, width), 0) * rows
        lane2 = lax.broadcasted_iota(I32, (N_EXPERTS, width), 1)
        spread = jnp.where((lane2 >= first) & (lane2 < first + rows), 1.0, 0.0).astype(BF16)
        g = aff_ref[...]
        g_hi = g.astype(BF16)
        g_lo = (g - g_hi.astype(F32)).astype(BF16)
        cols = jnp.concatenate([lpt_ref[...].astype(F32).astype(BF16), g_hi, g_lo], axis=0)
        wide = _dot(cols, spread)
        lane = lax.broadcasted_iota(I32, (1, width), 1)
        target = lane
        for e in range(N_EXPERTS):
            delta = start_sm[e * nt + t] - row0(e, t, rows)
            target = jnp.where((lane >= e * rows) & (lane < (e + 1) * rows), lane - (e * rows + delta), target)
        hit = wide[:COMB_TILE] == target.astype(F32)
        w = jnp.concatenate([jnp.where(hit, wide[COMB_TILE:2 * COMB_TILE], 0.0).astype(BF16),
                             jnp.where(hit, wide[2 * COMB_TILE:], 0.0).astype(BF16)], axis=0)
        both = _dot(w, stacked)
        o_ref[...] = x_ref[...] + both[:COMB_TILE] + both[COMB_TILE:]

    def expand_long(rows):
        for e in range(N_EXPERTS):
            copy(slot, e, 0, rows).wait()
        acc = x_ref[...]
        lane = lax.broadcasted_iota(I32, (COMB_TILE, rows), 1)
        for e in range(N_EXPERTS):
            delta = start_sm[e * nt + t] - row0(e, t, rows)
            onehot = jnp.where(lpt_ref[:, e:e + 1] + delta == lane, 1.0, 0.0).astype(BF16)
            acc = acc + aff_ref[:, e:e + 1] * _dot(onehot, slabs[slot, e])
        o_ref[...] = acc

    by_length(short_sm[t] == 1, fn=lambda rows: expand_short(rows) if rows == COMB_SHORT else expand_long(rows))


def _combine(start_flat, short_flat, x1, lp_t, aff, yg, cap):
    n = x1.shape[0]
    nt = n // COMB_TILE
    grid_spec = pltpu.PrefetchScalarGridSpec(
        num_scalar_prefetch=2, grid=(nt,),
        in_specs=[pl.BlockSpec((COMB_TILE, D_MODEL), lambda t, s, c: (t, 0)),
                  pl.BlockSpec((COMB_TILE, N_EXPERTS), lambda t, s, c: (t, 0)),
                  pl.BlockSpec((COMB_TILE, N_EXPERTS), lambda t, s, c: (t, 0)),
                  pl.BlockSpec(memory_space=pl.ANY)],
        out_specs=pl.BlockSpec((COMB_TILE, D_MODEL), lambda t, s, c: (t, 0)),
        scratch_shapes=[pltpu.VMEM((2, N_EXPERTS, COMB_LONG, D_MODEL), BF16),
                        pltpu.SemaphoreType.DMA((2,))])
    return pl.pallas_call(
        functools.partial(_combine_kernel, nt=nt, cap=cap), grid_spec=grid_spec,
        out_shape=jax.ShapeDtypeStruct((n, D_MODEL), F32),
        compiler_params=_cparams(("arbitrary",)), name="moe_combine",
    )(start_flat, short_flat, x1, lp_t, aff, yg)


def _moe(x1, h2, aff, wg, wu, wd, layer, ffn_rows):
    n = x1.shape[0]
    nt, ntd = n // COMB_TILE, n // DISP_TILE
    cap = max(1, CAPACITY_FACTOR * n // N_EXPERTS)
    assert DISP_TILE == 2 * COMB_TILE and cap >= DISP_LONG and cap % ROW_ALIGN == 0
    lp, start, cnt, lp_pair = _select(aff.T.reshape(N_EXPERTS, nt, COMB_TILE), cap)
    start_c, cnt_c = start[:, 0].reshape(N_EXPERTS, nt), cnt[:, 0].reshape(N_EXPERTS, nt)
    start_d, cnt_d = start_c[:, ::2], cnt_c[:, ::2] + cnt_c[:, 1::2]

    def short_flags(st, ct, short_rows):
        return (jnp.max((st & (ROW_ALIGN - 1)) + ct, axis=0) <= short_rows - ROW_ALIGN).astype(I32)

    lp_t = lp.reshape(N_EXPERTS, n).T
    lp_tiles = lp_pair.reshape(N_EXPERTS, ntd, DISP_TILE).transpose(1, 0, 2)
    xg = _dispatch(start_d.reshape(-1), cnt_d.reshape(-1), short_flags(start_d, cnt_d, DISP_SHORT), h2, lp_tiles, cap)
    yg = _ffn(xg, wg, wu, wd, layer, cap, min(ffn_rows, cap))
    return _combine(start_c.reshape(-1), short_flags(start_c, cnt_c, COMB_SHORT), x1, lp_t, aff, yg, cap)


def _rope_tables(S):
    pos = jnp.arange(S, dtype=F32)
    inv_freq = 1.0 / (ROPE_THETA ** (jnp.arange(0, ROPE_DIM, 2, dtype=F32) / ROPE_DIM))
    ang = pos[:, None] * inv_freq[None, :]
    return jnp.cos(ang), jnp.sin(ang)


def _prep_mla_weights(w_in, gq, gkv, w_uq, w_ukv, qn_g, kn_g):
    pad = lambda a, lo, hi: jnp.pad(a, ((0, 0), (lo, hi)))
    pe = w_in[:, Q_LORA + KV_LORA:]
    pe_sw = jnp.concatenate([pe[:, HALF_ROPE:], pe[:, :HALF_ROPE]], axis=1)
    w_in_p = jnp.concatenate([w_in[:, :Q_LORA + KV_LORA], pad(pe, NOPE_DIM, QK_PAD - QK_HEAD),
                              pad(pe_sw, NOPE_DIM, QK_PAD - QK_HEAD)], axis=1).astype(BF16)
    w_uqT = w_uq.T.astype(BF16)
    ukv = w_ukv.reshape(KV_LORA, MLA_HEADS, NOPE_DIM + V_HEAD)
    w_uk = jnp.pad(ukv[:, :, :NOPE_DIM], ((0, 0), (0, 0), (0, QK_PAD - NOPE_DIM)))
    w_uk = w_uk.reshape(KV_LORA, MLA_HEADS * QK_PAD).astype(BF16)
    w_uvT = ukv[:, :, NOPE_DIM:].reshape(KV_LORA, MLA_HEADS * V_HEAD).T.astype(BF16)
    gqc = (qn_g * np.float32(QK_HEAD ** -0.5 * np.log2(np.e))).reshape(QK_HEAD, 1)
    kb = (jnp.max(jnp.abs(kn_g)) * np.float32(QK_HEAD ** 0.5 * BOUND_MARGIN)).reshape(1, 1)
    return (w_in_p, gq.reshape(1, Q_LORA), gkv.reshape(1, KV_LORA), w_uqT, w_uk, w_uvT, gqc, kb)


def _score_bound_log2(qn_g, kn_g):
    return (jnp.max(jnp.abs(qn_g)) * jnp.max(jnp.abs(kn_g))
            * np.float32(QK_HEAD * QK_HEAD ** -0.5 * np.log2(np.e) * BOUND_MARGIN * BOUND_MARGIN))


def _prep_k_tables(cos, sin, kn_g):
    S = cos.shape[0]
    z = lambda w: jnp.zeros((S, w), F32)
    cos_l = jnp.concatenate([jnp.ones((S, NOPE_DIM), F32), cos, cos, z(QK_PAD - QK_HEAD)], axis=1)
    sin_l = jnp.concatenate([z(NOPE_DIM), -sin, sin, z(QK_PAD - QK_HEAD)], axis=1)
    g = jnp.pad(kn_g, (0, QK_PAD - QK_HEAD))
    g_sw = jnp.concatenate([jnp.zeros((NOPE_DIM,), F32), kn_g[NOPE_DIM + HALF_ROPE:],
                            kn_g[NOPE_DIM:NOPE_DIM + HALF_ROPE], jnp.zeros((QK_PAD - QK_HEAD,), F32)])
    return cos_l * g[None, :], sin_l * g_sw[None, :]


def _router_weights(w_r):
    hi = w_r.astype(BF16)
    lo = (w_r - hi.astype(F32)).astype(BF16)
    return jnp.concatenate([hi, lo], axis=1), hi


def _trunk(x3, p, tiles):
    B, S, _ = x3.shape
    n = B * S
    cos, sin = _rope_tables(S)
    mw = _prep_mla_weights(p['mla_w_in'][0], p['mla_q_lora_g'][0], p['mla_kv_lora_g'][0], p['mla_w_uq'][0],
                           p['mla_w_ukv'][0], p['mla_q_norm_g'][0], p['mla_k_norm_g'][0])
    kcos, ksin = _prep_k_tables(cos, sin, p['mla_k_norm_g'][0])
    qT, k, vT = _mla_prep(x3, p['norm1_g'][0].reshape(1, D_MODEL), mw, (kcos, ksin, cos.T, sin.T), tiles['tm'], tiles['tk'])
    fits = 2.0 * _score_bound_log2(p['mla_q_norm_g'][0], p['mla_k_norm_g'][0]) <= MAX_SHIFTED_SCORE_RANGE
    o = lax.cond(fits, lambda: _attention(qT, k, vT, tiles['tq'], True),
                 lambda: _attention(qT, k, vT, tiles['tq'], False)).reshape(n, D_MODEL)
    x = x3.reshape(n, D_MODEL)
    wr_cat, wr_hi = _router_weights(p['moe_w_router'][0])
    x1, h2, aff = _outproj(o, p['mla_w_out'][0].astype(BF16), x, p['norm2_g'][0].reshape(1, D_MODEL),
                           wr_cat, wr_hi, tiles['tm_out'])
    x = _moe(x1, h2, aff, p['moe_w_gate'], p['moe_w_up'], p['moe_w_down'], 0, tiles['ffn_rows'])
    wr_cat, wr_hi = _router_weights(p['moe_w_router'][1])
    bs_full = jnp.repeat(p['sgu_b_s'][0].T, LANES, axis=1)
    x1, h2, aff = _sgu(x, p['norm1_g'][1].reshape(1, D_MODEL), p['sgu_w_in'][0].astype(BF16),
                       p['sgu_ln_g'][0].reshape(1, SGU_HALF), p['sgu_ln_b'][0].reshape(1, SGU_HALF),
                       p['sgu_w_s'][0].astype(BF16), bs_full, p['sgu_w_out'][0].astype(BF16),
                       p['norm2_g'][1].reshape(1, D_MODEL), wr_cat, wr_hi, tiles['tm_sgu'])
    x = _moe(x1, h2, aff, p['moe_w_gate'], p['moe_w_up'], p['moe_w_down'], 1, tiles['ffn_rows'])
    return x.reshape(B, S, D_MODEL)


TILES = dict(tk=256, tq=1024, tm=512, tm_out=1024, tm_sgu=1024, ffn_rows=1024)


def kernel(x_prompt, x_sample, norm1_g, norm2_g, mla_w_in, mla_q_lora_g, mla_kv_lora_g, mla_w_uq, mla_w_ukv, mla_q_norm_g, mla_k_norm_g, mla_w_out, sgu_w_in, sgu_ln_g, sgu_ln_b, sgu_w_s, sgu_b_s, sgu_w_out, moe_w_router, moe_w_gate, moe_w_up, moe_w_down):
    p = dict(norm1_g=norm1_g, norm2_g=norm2_g, mla_w_in=mla_w_in, mla_q_lora_g=mla_q_lora_g,
             mla_kv_lora_g=mla_kv_lora_g, mla_w_uq=mla_w_uq, mla_w_ukv=mla_w_ukv, mla_q_norm_g=mla_q_norm_g,
             mla_k_norm_g=mla_k_norm_g, mla_w_out=mla_w_out, sgu_w_in=sgu_w_in, sgu_ln_g=sgu_ln_g,
             sgu_ln_b=sgu_ln_b, sgu_w_s=sgu_w_s, sgu_b_s=sgu_b_s, sgu_w_out=sgu_w_out, moe_w_router=moe_w_router,
             moe_w_gate=moe_w_gate, moe_w_up=moe_w_up, moe_w_down=moe_w_down)
    return (_trunk(x_prompt, p, TILES), _trunk(x_sample, p, TILES))
```

```python
import functools

import jax
import jax.numpy as jnp
import numpy as np
from jax import lax
from jax.experimental import pallas as pl
from jax.experimental.pallas import tpu as pltpu

F32 = jnp.float32
BF16 = jnp.bfloat16
I32 = jnp.int32

D_MODEL = 1024
MLA_HEADS = 16
Q_LORA = 256
KV_LORA = 128
NOPE_DIM = 64
ROPE_DIM = 32
HALF_ROPE = ROPE_DIM // 2
QK_HEAD = NOPE_DIM + ROPE_DIM
V_HEAD = 64
ROPE_THETA = 10000.0
CHUNK = 128
SGU_HALF = D_MODEL
SGU_GROUPS = 8
N_EXPERTS = 16
EXPERT_FF = 1024
CAPACITY_FACTOR = 2
EPS = 1e-6

LANES = 128
QK_PAD = 128
V_ROWS = 80
COMB_TILE = 128
DISP_TILE = 256
COMB_PER_STEP = 2
ROW_ALIGN = 16
ROW_ALIGN_SHIFT = 4
COMB_LONG, COMB_SHORT = COMB_TILE + ROW_ALIGN, 64
DISP_LONG, DISP_SHORT = DISP_TILE + ROW_ALIGN, 96
NOT_SELECTED = -1024
BOUND_MARGIN = 1.0 + 2.0 ** -6
MAX_SHIFTED_SCORE_RANGE = 100.0
HEADS_PER_STEP = 2
KT_PER_TRIP = 32

VMEM_LIMIT = 52 * 1024 * 1024


def _cparams(sem):
    return pltpu.CompilerParams(dimension_semantics=sem, vmem_limit_bytes=VMEM_LIMIT)


def _dot(a, b):
    return jnp.dot(a, b, preferred_element_type=F32)


def _dot_nt(a, b):
    return lax.dot_general(a, b, (((1,), (1,)), ((), ())), preferred_element_type=F32)


def _rms_rows(x, g):
    return x * lax.rsqrt(jnp.mean(x * x, axis=-1, keepdims=True) + EPS) * g


def _mla_prep_kernel(x_ref, g1_ref, w_in_ref, gq_ref, gkv_ref, w_uqT_ref, w_uk_ref, w_uvT_ref,
                     gqc_ref, kb_ref, kcos_ref, ksin_ref, cosT_ref, sinT_ref,
                     qT_ref, k_ref, vT_ref):
    x = x_ref[...]
    h = _rms_rows(x, g1_ref[...]).astype(BF16)
    proj = _dot(h, w_in_ref[...])
    cq = _rms_rows(proj[:, :Q_LORA], gq_ref[...]).astype(BF16)
    ckv = _rms_rows(proj[:, Q_LORA:Q_LORA + KV_LORA], gkv_ref[...]).astype(BF16)
    kpe = proj[:, 384:512]
    kpe_sw = proj[:, 512:640]
    tm = x.shape[0]
    n_kt, tk = k_ref.shape[2], k_ref.shape[3]

    qT = _dot_nt(w_uqT_ref[...], cq)
    cosT = cosT_ref[...]
    sinT = sinT_ref[...]
    gqc = gqc_ref[...]
    kbound = kb_ref[...]
    brow = lax.broadcasted_iota(I32, (QK_PAD - QK_HEAD, tm), 0)
    for hd in range(MLA_HEADS):
        qh = qT[hd * QK_HEAD:(hd + 1) * QK_HEAD, :]
        r = lax.rsqrt(jnp.sum(qh * qh, axis=0, keepdims=True) * (1.0 / QK_HEAD) + EPS)
        qn = qh * gqc * r
        x1 = qn[NOPE_DIM:NOPE_DIM + HALF_ROPE]
        x2 = qn[NOPE_DIM + HALF_ROPE:QK_HEAD]
        qb = jnp.concatenate([qn[:NOPE_DIM], x1 * cosT - x2 * sinT, x2 * cosT + x1 * sinT], axis=0).astype(BF16)
        qT_ref[0, hd, 0:QK_HEAD, :] = qb
        qf = qb.astype(F32)
        bound = jnp.sqrt(jnp.sum(qf * qf, axis=0, keepdims=True)) * kbound
        qT_ref[0, hd, QK_HEAD:QK_PAD, :] = jnp.where(brow == 0, -bound, 0.0).astype(BF16)

    knope = _dot(ckv, w_uk_ref[...])
    kcos = kcos_ref[...]
    ksin = ksin_ref[...]
    one_lane = lax.broadcasted_iota(I32, (tm, QK_PAD), 1) == QK_HEAD
    for hd in range(MLA_HEADS):
        kh = knope[:, hd * QK_PAD:(hd + 1) * QK_PAD] + kpe
        r = lax.rsqrt(jnp.sum(kh * kh, axis=-1, keepdims=True) * (1.0 / QK_HEAD) + EPS)
        kb = jnp.where(one_lane, 1.0, (kh * kcos + kpe_sw * ksin) * r).astype(BF16)
        for j in range(n_kt):
            k_ref[0, hd, j] = kb[j * tk:(j + 1) * tk]

    vT = _dot_nt(w_uvT_ref[...], ckv)
    row = lax.broadcasted_iota(I32, (V_ROWS - V_HEAD, tk), 0)
    tail = jnp.where(row == 0, 1.0, 0.0).astype(BF16)
    for hd in range(MLA_HEADS):
        vb = vT[hd * V_HEAD:(hd + 1) * V_HEAD, :].astype(BF16)
        for j in range(n_kt):
            vT_ref[0, hd, j, 0:V_HEAD, :] = vb[:, j * tk:(j + 1) * tk]
            vT_ref[0, hd, j, V_HEAD:V_ROWS, :] = tail


def _mla_prep(x3, g1, wts, tabs, tm, tk):
    B, S, _ = x3.shape
    nb, per = S // tm, tm // tk
    full = lambda a: pl.BlockSpec(a.shape, lambda b, i: (0,) * a.ndim)
    in_specs = [pl.BlockSpec((None, tm, D_MODEL), lambda b, i: (b, i, 0)), full(g1)]
    in_specs += [full(w) for w in wts]
    in_specs += [pl.BlockSpec((tm, LANES), lambda b, i: (i, 0)),
                 pl.BlockSpec((tm, LANES), lambda b, i: (i, 0)),
                 pl.BlockSpec((HALF_ROPE, tm), lambda b, i: (0, i)),
                 pl.BlockSpec((HALF_ROPE, tm), lambda b, i: (0, i))]
    out_shape = [jax.ShapeDtypeStruct((B, MLA_HEADS, QK_PAD, S), BF16),
                 jax.ShapeDtypeStruct((B, MLA_HEADS, S // tk, tk, QK_PAD), BF16),
                 jax.ShapeDtypeStruct((B, MLA_HEADS, S // tk, V_ROWS, tk), BF16)]
    out_specs = [pl.BlockSpec((1, MLA_HEADS, QK_PAD, tm), lambda b, i: (b, 0, 0, i)),
                 pl.BlockSpec((1, MLA_HEADS, per, tk, QK_PAD), lambda b, i: (b, 0, i, 0, 0)),
                 pl.BlockSpec((1, MLA_HEADS, per, V_ROWS, tk), lambda b, i: (b, 0, i, 0, 0))]
    return pl.pallas_call(
        _mla_prep_kernel, grid=(B, nb), in_specs=in_specs, out_specs=out_specs, out_shape=out_shape,
        compiler_params=_cparams(("arbitrary", "arbitrary")), name="mla_prep",
    )(x3, g1, *wts, *tabs)


def _attn_kernel(qT_ref, k_ref, vT_ref, o_ref, *, n_kt):
    tq = qT_ref.shape[3]
    qTs = [qT_ref[0, hh] for hh in range(HEADS_PER_STEP)]

    def body(kt, carry):
        out = []
        for hh in range(HEADS_PER_STEP):
            m, acc = carry[hh]
            sT = _dot(k_ref[0, hh, kt], qTs[hh])
            m_new = jnp.maximum(m, jnp.max(sT, axis=0, keepdims=True))
            alpha = jnp.exp2(m - m_new)
            pT = jnp.exp2(sT - m_new).astype(BF16)
            acc = alpha * acc + _dot(vT_ref[0, hh, kt], pT)
            out.append((m_new, acc))
        return tuple(out)

    init = tuple((jnp.full((1, tq), -jnp.inf, F32), jnp.zeros((V_ROWS, tq), F32))
                 for _ in range(HEADS_PER_STEP))
    res = lax.fori_loop(0, n_kt, body, init, unroll=2)
    outs = [acc[:V_HEAD] / acc[V_HEAD:V_HEAD + 1] for _, acc in res]
    o_ref[0] = jnp.concatenate(outs, axis=0).T.astype(o_ref.dtype)


def _attn_bounded_kernel(qT_ref, k_ref, vT_ref, o_ref, *, n_kt):
    tq = qT_ref.shape[3]
    qTs = [qT_ref[0, hh] for hh in range(HEADS_PER_STEP)]

    def scores(kt):
        return [_dot(k_ref[0, hh, kt], qTs[hh]) for hh in range(HEADS_PER_STEP)]

    def accumulate(accs, kt, sTs):
        return tuple(accs[hh] + _dot(vT_ref[0, hh, kt], jnp.exp2(sTs[hh]).astype(BF16))
                     for hh in range(HEADS_PER_STEP))

    def body(j, accs):
        kt = j * per_trip
        sTs = scores(kt)
        for u in range(1, per_trip):
            nxt = scores(kt + u)
            accs = accumulate(accs, kt + u - 1, sTs)
            sTs = nxt
        return accumulate(accs, kt + per_trip - 1, sTs)

    per_trip = min(KT_PER_TRIP, n_kt)
    assert n_kt % per_trip == 0
    accs = lax.fori_loop(0, n_kt // per_trip, body,
                         tuple(jnp.zeros((V_ROWS, tq), F32) for _ in range(HEADS_PER_STEP)))
    outs = [acc[:V_HEAD] / acc[V_HEAD:V_HEAD + 1] for acc in accs]
    o_ref[0] = jnp.concatenate(outs, axis=0).T.astype(o_ref.dtype)


def _attention(qT, k, vT, tq, bounded):
    B, H, _, S = qT.shape
    n_kt, tk = k.shape[2], k.shape[3]
    hp = H // HEADS_PER_STEP
    return pl.pallas_call(
        functools.partial(_attn_bounded_kernel if bounded else _attn_kernel, n_kt=n_kt),
        grid=(B, hp, S // tq),
        in_specs=[pl.BlockSpec((1, HEADS_PER_STEP, QK_PAD, tq), lambda b, h, i: (b, h, 0, i)),
                  pl.BlockSpec((1, HEADS_PER_STEP, n_kt, tk, QK_PAD), lambda b, h, i: (b, h, 0, 0, 0)),
                  pl.BlockSpec((1, HEADS_PER_STEP, n_kt, V_ROWS, tk), lambda b, h, i: (b, h, 0, 0, 0))],
        out_specs=pl.BlockSpec((1, tq, HEADS_PER_STEP * V_HEAD), lambda b, h, i: (b, i, h)),
        out_shape=jax.ShapeDtypeStruct((B, S, H * V_HEAD), BF16),
        compiler_params=_cparams(("arbitrary", "arbitrary", "arbitrary")),
        name="mla_attention_bounded" if bounded else "mla_attention",
    )(qT, k, vT)


def _router_epilogue(x1, g2_ref, wr_cat_ref, wr_hi_ref, x1_ref, h2_ref, aff_ref):
    x1_ref[...] = x1
    h2 = _rms_rows(x1, g2_ref[...])
    h2_hi = h2.astype(BF16)
    h2_lo = (h2 - h2_hi.astype(F32)).astype(BF16)
    h2_ref[...] = h2_hi
    both = _dot(h2_hi, wr_cat_ref[...])
    logits = both[:, :N_EXPERTS] + both[:, N_EXPERTS:] + _dot(h2_lo, wr_hi_ref[...])
    ex = jnp.exp(logits - jnp.max(logits, axis=-1, keepdims=True))
    aff_ref[...] = ex / jnp.sum(ex, axis=-1, keepdims=True)


def _outproj_kernel(o_ref, w_ref, x_ref, g2_ref, wr_cat_ref, wr_hi_ref, x1_ref, h2_ref, aff_ref):
    x1 = x_ref[...] + _dot(o_ref[...], w_ref[...])
    _router_epilogue(x1, g2_ref, wr_cat_ref, wr_hi_ref, x1_ref, h2_ref, aff_ref)


def _row_specs(n, tm):
    out_shape = [jax.ShapeDtypeStruct((n, D_MODEL), F32),
                 jax.ShapeDtypeStruct((n, D_MODEL), BF16),
                 jax.ShapeDtypeStruct((n, N_EXPERTS), F32)]
    out_specs = [pl.BlockSpec((tm, D_MODEL), lambda i: (i, 0)),
                 pl.BlockSpec((tm, D_MODEL), lambda i: (i, 0)),
                 pl.BlockSpec((tm, N_EXPERTS), lambda i: (i, 0))]
    return out_shape, out_specs


def _outproj(o, w_out, x, g2, wr_cat, wr_hi, tm):
    n = x.shape[0]
    full = lambda a: pl.BlockSpec(a.shape, lambda i: (0,) * a.ndim)
    out_shape, out_specs = _row_specs(n, tm)
    return pl.pallas_call(
        _outproj_kernel, grid=(n // tm,),
        in_specs=[pl.BlockSpec((tm, D_MODEL), lambda i: (i, 0)), full(w_out),
                  pl.BlockSpec((tm, D_MODEL), lambda i: (i, 0)), full(g2), full(wr_cat), full(wr_hi)],
        out_specs=out_specs, out_shape=out_shape,
        compiler_params=_cparams(("arbitrary",)), name="mla_outproj_router",
    )(o, w_out, x, g2, wr_cat, wr_hi)


def _sgu_kernel(x_ref, g1_ref, w_in_ref, lng_ref, lnb_ref, ws_ref, bs_ref, w_out_ref,
                g2_ref, wr_cat_ref, wr_hi_ref, x1_ref, h2_ref, aff_ref):
    x = x_ref[...]
    tm = x.shape[0]
    h = _rms_rows(x, g1_ref[...]).astype(BF16)
    z = _dot(h, w_in_ref[...])
    z = 0.5 * z * (1.0 + lax.erf(z * np.float32(2.0 ** -0.5)))
    u = z[:, :SGU_HALF]
    v = z[:, SGU_HALF:]
    mu = jnp.mean(v, axis=-1, keepdims=True)
    vc = v - mu
    v = vc * lax.rsqrt(jnp.mean(vc * vc, axis=-1, keepdims=True) + EPS) * lng_ref[...] + lnb_ref[...]
    vb = v.astype(BF16)
    bs = bs_ref[...]
    n_chunks = tm // CHUNK
    mixed = []
    for g in range(SGU_GROUPS):
        blk = jnp.concatenate([vb[c * CHUNK:(c + 1) * CHUNK, g * LANES:(g + 1) * LANES]
                               for c in range(n_chunks)], axis=1)
        mixed.append(_dot(ws_ref[g], blk))
    sv = jnp.concatenate(
        [jnp.concatenate([mixed[g][:, c * LANES:(c + 1) * LANES] for g in range(SGU_GROUPS)], axis=1) + bs
         for c in range(n_chunks)], axis=0)
    gated = (u * sv).astype(BF16)
    x1 = x + _dot(gated, w_out_ref[...])
    _router_epilogue(x1, g2_ref, wr_cat_ref, wr_hi_ref, x1_ref, h2_ref, aff_ref)


def _sgu(x, g1, w_in, lng, lnb, ws, bs_full, w_out, g2, wr_cat, wr_hi, tm):
    n = x.shape[0]
    full = lambda a: pl.BlockSpec(a.shape, lambda i: (0,) * a.ndim)
    ops = (g1, w_in, lng, lnb, ws, bs_full, w_out, g2, wr_cat, wr_hi)
    out_shape, out_specs = _row_specs(n, tm)
    return pl.pallas_call(
        _sgu_kernel, grid=(n // tm,),
        in_specs=[pl.BlockSpec((tm, D_MODEL), lambda i: (i, 0))] + [full(a) for a in ops],
        out_specs=out_specs, out_shape=out_shape,
        compiler_params=_cparams(("arbitrary",)), name="sgu_router",
    )(x, *ops)


def _select_kernel(aff_ref, lp_ref, start_ref, cnt_ref, lp_pair_ref, *, nt, cap):
    a = aff_ref[...]
    bits = pltpu.bitcast(a, I32)

    def body(i, cur):
        trial = cur | jnp.left_shift(jnp.int32(1), 30 - i)
        ge = jnp.where(bits >= trial, 1.0, 0.0)
        cnt = jnp.sum(jnp.sum(ge, axis=1, keepdims=True), axis=2, keepdims=True)
        return jnp.where(cnt >= cap, trial, cur)

    thr = lax.fori_loop(0, 31, body, jnp.zeros((N_EXPERTS, 1, 1), I32))
    gtf = jnp.where(bits > thr, 1.0, 0.0)
    eqf = jnp.where(bits == thr, 1.0, 0.0)
    c_gt = jnp.sum(jnp.sum(gtf, axis=1, keepdims=True), axis=2, keepdims=True)
    need = cap - c_gt

    r = lax.broadcasted_iota(I32, (COMB_TILE, COMB_TILE), 0)
    c = lax.broadcasted_iota(I32, (COMB_TILE, COMB_TILE), 1)
    incl_u = jnp.where(r <= c, 1.0, 0.0).astype(BF16)
    ones_u = jnp.ones((COMB_TILE, LANES), BF16)
    rt = lax.broadcasted_iota(I32, (nt, nt), 0)
    ct = lax.broadcasted_iota(I32, (nt, nt), 1)
    excl_l = jnp.where(rt > ct, 1.0, 0.0).astype(BF16)

    def tile_prefix(maskf):
        m2 = maskf.reshape(N_EXPERTS * nt, COMB_TILE).astype(BF16)
        incl = _dot(m2, incl_u)
        cnt = _dot(m2, ones_u)
        cnt_b = cnt.astype(BF16)
        starts = [_dot(excl_l, cnt_b[e * nt:(e + 1) * nt]) for e in range(N_EXPERTS)]
        return incl, jnp.concatenate(starts, axis=0), cnt

    eq_incl, eq_start, _ = tile_prefix(eqf)
    tie_rank = (eq_start + eq_incl).reshape(N_EXPERTS, nt, COMB_TILE) - eqf
    selm = gtf + eqf * jnp.where(tie_rank < need, 1.0, 0.0)
    incl, start, cnt = tile_prefix(selm)
    sel2 = selm.reshape(N_EXPERTS * nt, COMB_TILE) > 0.5
    lp_ref[...] = jnp.where(sel2, incl - 1.0, float(NOT_SELECTED)).astype(I32)
    start_ref[...] = start.astype(I32)
    cnt_ref[...] = cnt.astype(I32)
    odd = (lax.broadcasted_iota(I32, (N_EXPERTS * nt, COMB_TILE), 0) & 1) == 1
    before = jnp.where(odd, jnp.concatenate([jnp.zeros((1, COMB_TILE), F32), cnt[:-1]], axis=0), 0.0)
    lp_pair_ref[...] = jnp.where(sel2, incl - 1.0 + before, float(NOT_SELECTED)).astype(I32)


def _select(aff_t3, cap):
    e, nt, _ = aff_t3.shape
    return pl.pallas_call(
        functools.partial(_select_kernel, nt=nt, cap=cap),
        out_shape=[jax.ShapeDtypeStruct((e * nt, COMB_TILE), I32)] * 4,
        compiler_params=pltpu.CompilerParams(vmem_limit_bytes=VMEM_LIMIT), name="moe_select",
    )(aff_t3)


def _round_down(v):
    return (v >> ROW_ALIGN_SHIFT) << ROW_ALIGN_SHIFT


def _by_slab_length(short, short_rows, long_rows, fn):
    @pl.when(short)
    def _():
        fn(short_rows)

    @pl.when(jnp.logical_not(short))
    def _():
        fn(long_rows)


def _dispatch_kernel(start_sm, cnt_sm, short_sm, x_ref, lp_ref, xg_hbm, stage, carry, sems, *, nt, cap):
    t = pl.program_id(0)
    slot = t % 2

    def copy(s, e, row0, rows):
        dst = xg_hbm.at[e, pl.ds(pl.multiple_of(row0, ROW_ALIGN), rows), :]
        return pltpu.make_async_copy(stage.at[s, e, pl.ds(0, rows)], dst, sems.at[s])

    @pl.when(t == 0)
    def _():
        carry[...] = jnp.zeros(carry.shape, carry.dtype)
        stage[1] = jnp.zeros(stage.shape[1:], stage.dtype)
        for e in range(N_EXPERTS):
            copy(1, e, cap, DISP_LONG).start()
        for e in range(N_EXPERTS):
            copy(1, e, cap, DISP_LONG).wait()

    x = x_ref[...]
    rio = lax.broadcasted_iota(I32, (ROW_ALIGN, D_MODEL), 0)
    short_now = short_sm[t] == 1
    shifts = [start_sm[e * nt + t] - _round_down(start_sm[e * nt + t]) for e in range(N_EXPERTS)]

    def onehot_t(e, rows):
        jio = lax.broadcasted_iota(I32, (rows, DISP_TILE), 0)
        return jnp.where(lp_ref[0, e:e + 1, :] + shifts[e] == jio, 1.0, 0.0).astype(BF16)

    def place(e, slab, rows):
        head = jnp.where(rio < shifts[e], carry[e].astype(F32), slab[0:ROW_ALIGN])
        stage[slot, e, 0:ROW_ALIGN] = head.astype(BF16)
        stage[slot, e, ROW_ALIGN:rows] = slab[ROW_ALIGN:].astype(BF16)
        nxt = pl.multiple_of(_round_down(shifts[e] + cnt_sm[e * nt + t]), ROW_ALIGN)
        carry[e] = stage[slot, e, pl.ds(nxt, ROW_ALIGN)]

    def fill(rows):
        if rows == DISP_SHORT:
            slabs = _dot(jnp.concatenate([onehot_t(e, rows) for e in range(N_EXPERTS)], axis=0), x)
            for e in range(N_EXPERTS):
                place(e, slabs[e * rows:(e + 1) * rows], rows)
        else:
            for e in range(N_EXPERTS):
                place(e, _dot(onehot_t(e, rows), x), rows)

    by_length = functools.partial(_by_slab_length, short_rows=DISP_SHORT, long_rows=DISP_LONG)
    by_length(short_now, fn=fill)

    @pl.when(t > 0)
    def _():
        by_length(short_sm[t - 1] == 1, fn=lambda rows: [copy(1 - slot, e, 0, rows).wait() for e in range(N_EXPERTS)])

    by_length(short_now, fn=lambda rows: [copy(slot, e, _round_down(start_sm[e * nt + t]), rows).start()
                                          for e in range(N_EXPERTS)])

    @pl.when(t == nt - 1)
    def _():
        by_length(short_now, fn=lambda rows: [copy(slot, e, 0, rows).wait() for e in range(N_EXPERTS)])


def _dispatch(start_flat, cnt_flat, short_flat, h2, lp3, cap):
    n = h2.shape[0]
    nt = n // DISP_TILE
    grid_spec = pltpu.PrefetchScalarGridSpec(
        num_scalar_prefetch=3, grid=(nt,),
        in_specs=[pl.BlockSpec((DISP_TILE, D_MODEL), lambda t, *_: (t, 0)),
                  pl.BlockSpec((1, N_EXPERTS, DISP_TILE), lambda t, *_: (t, 0, 0))],
        out_specs=pl.BlockSpec(memory_space=pl.ANY),
        scratch_shapes=[pltpu.VMEM((2, N_EXPERTS, DISP_LONG, D_MODEL), BF16),
                        pltpu.VMEM((N_EXPERTS, ROW_ALIGN, D_MODEL), BF16),
                        pltpu.SemaphoreType.DMA((2,))])
    return pl.pallas_call(
        functools.partial(_dispatch_kernel, nt=nt, cap=cap), grid_spec=grid_spec,
        out_shape=jax.ShapeDtypeStruct((N_EXPERTS, cap + DISP_LONG, D_MODEL), BF16),
        compiler_params=_cparams(("arbitrary",)), name="moe_dispatch",
    )(start_flat, cnt_flat, short_flat, h2, lp3)


def _ffn_kernel(xg_ref, wg_ref, wu_ref, wd_ref, yg_ref, wg_b, wu_b, wd_b):
    @pl.when(pl.program_id(1) == 0)
    def _():
        wg_b[...] = wg_ref[0, 0].astype(BF16)
        wu_b[...] = wu_ref[0, 0].astype(BF16)
        wd_b[...] = wd_ref[0, 0].astype(BF16)

    x = xg_ref[0]
    g = _dot(x, wg_b[...])
    u = _dot(x, wu_b[...])
    hid = (g * jax.nn.sigmoid(g) * u).astype(BF16)
    y = _dot(hid, wd_b[...])
    yg_ref[0] = y.astype(BF16)


def _ffn(xg, wg, wu, wd, layer, cap, rt):
    wspec = pl.BlockSpec((1, 1, D_MODEL, EXPERT_FF), lambda e, r: (layer, e, 0, 0))
    return pl.pallas_call(
        _ffn_kernel, grid=(N_EXPERTS, cap // rt),
        in_specs=[pl.BlockSpec((1, rt, D_MODEL), lambda e, r: (e, r, 0)), wspec, wspec,
                  pl.BlockSpec((1, 1, EXPERT_FF, D_MODEL), lambda e, r: (layer, e, 0, 0))],
        out_specs=pl.BlockSpec((1, rt, D_MODEL), lambda e, r: (e, r, 0)),
        out_shape=jax.ShapeDtypeStruct((N_EXPERTS, cap, D_MODEL), BF16),
        scratch_shapes=[pltpu.VMEM((D_MODEL, EXPERT_FF), BF16), pltpu.VMEM((D_MODEL, EXPERT_FF), BF16),
                        pltpu.VMEM((EXPERT_FF, D_MODEL), BF16)],
        compiler_params=_cparams(("arbitrary", "arbitrary")), name="moe_ffn",
    )(xg, wg, wu, wd)


def _combine_kernel(start_sm, short_sm, x_ref, lpt_ref, aff_ref, yg_hbm, o_ref, slabs, sems, *, nt, cap):
    t = pl.program_id(0)
    slot = t % 2
    n_steps = nt // COMB_PER_STEP

    def row0(e, tt, rows):
        return jnp.minimum(_round_down(start_sm[e * nt + tt]), cap - rows)

    def copy(s, j, e, r0, rows):
        src = yg_hbm.at[e, pl.ds(pl.multiple_of(r0, ROW_ALIGN), rows), :]
        return pltpu.make_async_copy(src, slabs.at[s, j, e, pl.ds(0, rows)], sems.at[s, j])

    by_length = functools.partial(_by_slab_length, short_rows=COMB_SHORT, long_rows=COMB_LONG)

    def fetch(s, step):
        for j in range(COMB_PER_STEP):
            tt = step * COMB_PER_STEP + j
            by_length(short_sm[tt] == 1,
                      fn=lambda rows, j=j, tt=tt: [copy(s, j, e, row0(e, tt, rows), rows).start()
                                                   for e in range(N_EXPERTS)])

    @pl.when(t == 0)
    def _():
        fetch(0, 0)

    @pl.when(t + 1 < n_steps)
    def _():
        fetch(1 - slot, t + 1)

    def expand_short(j, tt, rows):
        tok = pl.ds(j * COMB_TILE, COMB_TILE)
        for e in range(N_EXPERTS):
            copy(slot, j, e, 0, rows).wait()
        stacked = jnp.concatenate([slabs[slot, j, e, 0:rows] for e in range(N_EXPERTS)], axis=0)
        width = N_EXPERTS * rows
        first = lax.broadcasted_iota(I32, (N_EXPERTS, width), 0) * rows
        lane2 = lax.broadcasted_iota(I32, (N_EXPERTS, width), 1)
        spread = jnp.where((lane2 >= first) & (lane2 < first + rows), 1.0, 0.0).astype(BF16)
        g = aff_ref[tok, :]
        g_hi = g.astype(BF16)
        g_lo = (g - g_hi.astype(F32)).astype(BF16)
        cols = jnp.concatenate([lpt_ref[tok, :].astype(F32).astype(BF16), g_hi, g_lo], axis=0)
        wide = _dot(cols, spread)
        lane = lax.broadcasted_iota(I32, (1, width), 1)
        target = lane
        for e in range(N_EXPERTS):
            delta = start_sm[e * nt + tt] - row0(e, tt, rows)
            target = jnp.where((lane >= e * rows) & (lane < (e + 1) * rows), lane - (e * rows + delta), target)
        hit = wide[:COMB_TILE] == target.astype(F32)
        w = jnp.concatenate([jnp.where(hit, wide[COMB_TILE:2 * COMB_TILE], 0.0).astype(BF16),
                             jnp.where(hit, wide[2 * COMB_TILE:], 0.0).astype(BF16)], axis=0)
        both = _dot(w, stacked)
        o_ref[tok, :] = x_ref[tok, :] + both[:COMB_TILE] + both[COMB_TILE:]

    def expand_long(j, tt, rows):
        tok = pl.ds(j * COMB_TILE, COMB_TILE)
        for e in range(N_EXPERTS):
            copy(slot, j, e, 0, rows).wait()
        acc = x_ref[tok, :]
        lane = lax.broadcasted_iota(I32, (COMB_TILE, rows), 1)
        for e in range(N_EXPERTS):
            delta = start_sm[e * nt + tt] - row0(e, tt, rows)
            onehot = jnp.where(lpt_ref[tok, e:e + 1] + delta == lane, 1.0, 0.0).astype(BF16)
            acc = acc + aff_ref[tok, e:e + 1] * _dot(onehot, slabs[slot, j, e])
        o_ref[tok, :] = acc

    for j in range(COMB_PER_STEP):
        tt = t * COMB_PER_STEP + j
        by_length(short_sm[tt] == 1,
                  fn=lambda rows, j=j, tt=tt: expand_short(j, tt, rows) if rows == COMB_SHORT
                  else expand_long(j, tt, rows))


def _combine(start_flat, short_flat, x1, lp_t, aff, yg, cap):
    n = x1.shape[0]
    nt = n // COMB_TILE
    rows = COMB_TILE * COMB_PER_STEP
    grid_spec = pltpu.PrefetchScalarGridSpec(
        num_scalar_prefetch=2, grid=(nt // COMB_PER_STEP,),
        in_specs=[pl.BlockSpec((rows, D_MODEL), lambda t, s, c: (t, 0)),
                  pl.BlockSpec((rows, N_EXPERTS), lambda t, s, c: (t, 0)),
                  pl.BlockSpec((rows, N_EXPERTS), lambda t, s, c: (t, 0)),
                  pl.BlockSpec(memory_space=pl.ANY)],
        out_specs=pl.BlockSpec((rows, D_MODEL), lambda t, s, c: (t, 0)),
        scratch_shapes=[pltpu.VMEM((2, COMB_PER_STEP, N_EXPERTS, COMB_LONG, D_MODEL), BF16),
                        pltpu.SemaphoreType.DMA((2, COMB_PER_STEP))])
    return pl.pallas_call(
        functools.partial(_combine_kernel, nt=nt, cap=cap), grid_spec=grid_spec,
        out_shape=jax.ShapeDtypeStruct((n, D_MODEL), F32),
        compiler_params=_cparams(("arbitrary",)), name="moe_combine",
    )(start_flat, short_flat, x1, lp_t, aff, yg)


def _moe(x1, h2, aff, wg, wu, wd, layer, ffn_rows):
    n = x1.shape[0]
    nt, ntd = n // COMB_TILE, n // DISP_TILE
    cap = max(1, CAPACITY_FACTOR * n // N_EXPERTS)
    assert DISP_TILE == 2 * COMB_TILE and cap >= DISP_LONG and cap % ROW_ALIGN == 0
    lp, start, cnt, lp_pair = _select(aff.T.reshape(N_EXPERTS, nt, COMB_TILE), cap)
    start_c, cnt_c = start[:, 0].reshape(N_EXPERTS, nt), cnt[:, 0].reshape(N_EXPERTS, nt)
    start_d, cnt_d = start_c[:, ::2], cnt_c[:, ::2] + cnt_c[:, 1::2]

    def short_flags(st, ct, short_rows):
        return (jnp.max((st & (ROW_ALIGN - 1)) + ct, axis=0) <= short_rows - ROW_ALIGN).astype(I32)

    lp_t = lp.reshape(N_EXPERTS, n).T
    lp_tiles = lp_pair.reshape(N_EXPERTS, ntd, DISP_TILE).transpose(1, 0, 2)
    xg = _dispatch(start_d.reshape(-1), cnt_d.reshape(-1), short_flags(start_d, cnt_d, DISP_SHORT), h2, lp_tiles, cap)
    yg = _ffn(xg, wg, wu, wd, layer, cap, min(ffn_rows, cap))
    return _combine(start_c.reshape(-1), short_flags(start_c, cnt_c, COMB_SHORT), x1, lp_t, aff, yg, cap)


def _rope_tables(S):
    pos = jnp.arange(S, dtype=F32)
    inv_freq = 1.0 / (ROPE_THETA ** (jnp.arange(0, ROPE_DIM, 2, dtype=F32) / ROPE_DIM))
    ang = pos[:, None] * inv_freq[None, :]
    return jnp.cos(ang), jnp.sin(ang)


def _prep_mla_weights(w_in, gq, gkv, w_uq, w_ukv, qn_g, kn_g):
    pad = lambda a, lo, hi: jnp.pad(a, ((0, 0), (lo, hi)))
    pe = w_in[:, Q_LORA + KV_LORA:]
    pe_sw = jnp.concatenate([pe[:, HALF_ROPE:], pe[:, :HALF_ROPE]], axis=1)
    w_in_p = jnp.concatenate([w_in[:, :Q_LORA + KV_LORA], pad(pe, NOPE_DIM, QK_PAD - QK_HEAD),
                              pad(pe_sw, NOPE_DIM, QK_PAD - QK_HEAD)], axis=1).astype(BF16)
    w_uqT = w_uq.T.astype(BF16)
    ukv = w_ukv.reshape(KV_LORA, MLA_HEADS, NOPE_DIM + V_HEAD)
    w_uk = jnp.pad(ukv[:, :, :NOPE_DIM], ((0, 0), (0, 0), (0, QK_PAD - NOPE_DIM)))
    w_uk = w_uk.reshape(KV_LORA, MLA_HEADS * QK_PAD).astype(BF16)
    w_uvT = ukv[:, :, NOPE_DIM:].reshape(KV_LORA, MLA_HEADS * V_HEAD).T.astype(BF16)
    gqc = (qn_g * np.float32(QK_HEAD ** -0.5 * np.log2(np.e))).reshape(QK_HEAD, 1)
    kb = (jnp.max(jnp.abs(kn_g)) * np.float32(QK_HEAD ** 0.5 * BOUND_MARGIN)).reshape(1, 1)
    return (w_in_p, gq.reshape(1, Q_LORA), gkv.reshape(1, KV_LORA), w_uqT, w_uk, w_uvT, gqc, kb)


def _score_bound_log2(qn_g, kn_g):
    return (jnp.max(jnp.abs(qn_g)) * jnp.max(jnp.abs(kn_g))
            * np.float32(QK_HEAD * QK_HEAD ** -0.5 * np.log2(np.e) * BOUND_MARGIN * BOUND_MARGIN))


def _prep_k_tables(cos, sin, kn_g):
    S = cos.shape[0]
    z = lambda w: jnp.zeros((S, w), F32)
    cos_l = jnp.concatenate([jnp.ones((S, NOPE_DIM), F32), cos, cos, z(QK_PAD - QK_HEAD)], axis=1)
    sin_l = jnp.concatenate([z(NOPE_DIM), -sin, sin, z(QK_PAD - QK_HEAD)], axis=1)
    g = jnp.pad(kn_g, (0, QK_PAD - QK_HEAD))
    g_sw = jnp.concatenate([jnp.zeros((NOPE_DIM,), F32), kn_g[NOPE_DIM + HALF_ROPE:],
                            kn_g[NOPE_DIM:NOPE_DIM + HALF_ROPE], jnp.zeros((QK_PAD - QK_HEAD,), F32)])
    return cos_l * g[None, :], sin_l * g_sw[None, :]


def _router_weights(w_r):
    hi = w_r.astype(BF16)
    lo = (w_r - hi.astype(F32)).astype(BF16)
    return jnp.concatenate([hi, lo], axis=1), hi


def _trunk(x3, p, tiles):
    B, S, _ = x3.shape
    n = B * S
    cos, sin = _rope_tables(S)
    mw = _prep_mla_weights(p['mla_w_in'][0], p['mla_q_lora_g'][0], p['mla_kv_lora_g'][0], p['mla_w_uq'][0],
                           p['mla_w_ukv'][0], p['mla_q_norm_g'][0], p['mla_k_norm_g'][0])
    kcos, ksin = _prep_k_tables(cos, sin, p['mla_k_norm_g'][0])
    qT, k, vT = _mla_prep(x3, p['norm1_g'][0].reshape(1, D_MODEL), mw, (kcos, ksin, cos.T, sin.T), tiles['tm'], tiles['tk'])
    fits = 2.0 * _score_bound_log2(p['mla_q_norm_g'][0], p['mla_k_norm_g'][0]) <= MAX_SHIFTED_SCORE_RANGE
    o = lax.cond(fits, lambda: _attention(qT, k, vT, tiles['tq'], True),
                 lambda: _attention(qT, k, vT, tiles['tq'], False)).reshape(n, D_MODEL)
    x = x3.reshape(n, D_MODEL)
    wr_cat, wr_hi = _router_weights(p['moe_w_router'][0])
    x1, h2, aff = _outproj(o, p['mla_w_out'][0].astype(BF16), x, p['norm2_g'][0].reshape(1, D_MODEL),
                           wr_cat, wr_hi, tiles['tm_out'])
    x = _moe(x1, h2, aff, p['moe_w_gate'], p['moe_w_up'], p['moe_w_down'], 0, tiles['ffn_rows'])
    wr_cat, wr_hi = _router_weights(p['moe_w_router'][1])
    bs_full = jnp.repeat(p['sgu_b_s'][0].T, LANES, axis=1)
    x1, h2, aff = _sgu(x, p['norm1_g'][1].reshape(1, D_MODEL), p['sgu_w_in'][0].astype(BF16),
                       p['sgu_ln_g'][0].reshape(1, SGU_HALF), p['sgu_ln_b'][0].reshape(1, SGU_HALF),
                       p['sgu_w_s'][0].astype(BF16), bs_full, p['sgu_w_out'][0].astype(BF16),
                       p['norm2_g'][1].reshape(1, D_MODEL), wr_cat, wr_hi, tiles['tm_sgu'])
    x = _moe(x1, h2, aff, p['moe_w_gate'], p['moe_w_up'], p['moe_w_down'], 1, tiles['ffn_rows'])
    return x.reshape(B, S, D_MODEL)


TILES = dict(tk=256, tq=1024, tm=512, tm_out=1024, tm_sgu=1024, ffn_rows=1024)


def kernel(x_prompt, x_sample, norm1_g, norm2_g, mla_w_in, mla_q_lora_g, mla_kv_lora_g, mla_w_uq, mla_w_ukv, mla_q_norm_g, mla_k_norm_g, mla_w_out, sgu_w_in, sgu_ln_g, sgu_ln_b, sgu_w_s, sgu_b_s, sgu_w_out, moe_w_router, moe_w_gate, moe_w_up, moe_w_down):
    p = dict(norm1_g=norm1_g, norm2_g=norm2_g, mla_w_in=mla_w_in, mla_q_lora_g=mla_q_lora_g,
             mla_kv_lora_g=mla_kv_lora_g, mla_w_uq=mla_w_uq, mla_w_ukv=mla_w_ukv, mla_q_norm_g=mla_q_norm_g,
             mla_k_norm_g=mla_k_norm_g, mla_w_out=mla_w_out, sgu_w_in=sgu_w_in, sgu_ln_g=sgu_ln_g,
             sgu_ln_b=sgu_ln_b, sgu_w_s=sgu_w_s, sgu_b_s=sgu_b_s, sgu_w_out=sgu_w_out, moe_w_router=moe_w_router,
             moe_w_gate=moe_w_gate, moe_w_up=moe_w_up, moe_w_down=moe_w_down)
    return (_trunk(x_prompt, p, TILES), _trunk(x_sample, p, TILES))
```

```python
import functools

import jax
import jax.numpy as jnp
import numpy as np
from jax import lax
from jax.experimental import pallas as pl
from jax.experimental.pallas import tpu as pltpu

F32 = jnp.float32
BF16 = jnp.bfloat16
I32 = jnp.int32

D_MODEL = 1024
MLA_HEADS = 16
Q_LORA = 256
KV_LORA = 128
NOPE_DIM = 64
ROPE_DIM = 32
HALF_ROPE = ROPE_DIM // 2
QK_HEAD = NOPE_DIM + ROPE_DIM
V_HEAD = 64
ROPE_THETA = 10000.0
CHUNK = 128
SGU_HALF = D_MODEL
SGU_GROUPS = 8
N_EXPERTS = 16
EXPERT_FF = 1024
CAPACITY_FACTOR = 2
EPS = 1e-6

LANES = 128
QK_PAD = 128
V_ROWS = 80
COMB_TILE = 128
DISP_TILE = 256
COMB_PER_STEP = 2
ROW_ALIGN = 16
ROW_ALIGN_SHIFT = 4
COMB_LONG, COMB_SHORT = COMB_TILE + ROW_ALIGN, 64
DISP_LONG, DISP_SHORT = DISP_TILE + ROW_ALIGN, 80
NOT_SELECTED = -1024
BOUND_MARGIN = 1.0 + 2.0 ** -6
MAX_SHIFTED_SCORE_RANGE = 100.0
HEADS_PER_STEP = 2
KT_PER_TRIP = 32

VMEM_LIMIT = 52 * 1024 * 1024


def _cparams(sem):
    return pltpu.CompilerParams(dimension_semantics=sem, vmem_limit_bytes=VMEM_LIMIT)


def _dot(a, b):
    return jnp.dot(a, b, preferred_element_type=F32)


def _dot_nt(a, b):
    return lax.dot_general(a, b, (((1,), (1,)), ((), ())), preferred_element_type=F32)


def _rms_rows(x, g):
    return x * lax.rsqrt(jnp.mean(x * x, axis=-1, keepdims=True) + EPS) * g


def _mla_prep_kernel(x_ref, g1_ref, w_in_ref, gq_ref, gkv_ref, w_uqT_ref, w_uk_ref, w_uvT_ref,
                     gqc_ref, kb_ref, kcos_ref, ksin_ref, cosT_ref, sinT_ref,
                     qT_ref, k_ref, vT_ref):
    x = x_ref[...]
    h = _rms_rows(x, g1_ref[...]).astype(BF16)
    proj = _dot(h, w_in_ref[...])
    cq = _rms_rows(proj[:, :Q_LORA], gq_ref[...]).astype(BF16)
    ckv = _rms_rows(proj[:, Q_LORA:Q_LORA + KV_LORA], gkv_ref[...]).astype(BF16)
    kpe = proj[:, 384:512]
    kpe_sw = proj[:, 512:640]
    tm = x.shape[0]
    n_kt, tk = k_ref.shape[2], k_ref.shape[3]

    qT = _dot_nt(w_uqT_ref[...], cq)
    cosT = cosT_ref[...]
    sinT = sinT_ref[...]
    gqc = gqc_ref[...]
    kbound = kb_ref[...]
    brow = lax.broadcasted_iota(I32, (QK_PAD - QK_HEAD, tm), 0)
    for hd in range(MLA_HEADS):
        qh = qT[hd * QK_HEAD:(hd + 1) * QK_HEAD, :]
        r = lax.rsqrt(jnp.sum(qh * qh, axis=0, keepdims=True) * (1.0 / QK_HEAD) + EPS)
        qn = qh * gqc * r
        x1 = qn[NOPE_DIM:NOPE_DIM + HALF_ROPE]
        x2 = qn[NOPE_DIM + HALF_ROPE:QK_HEAD]
        qb = jnp.concatenate([qn[:NOPE_DIM], x1 * cosT - x2 * sinT, x2 * cosT + x1 * sinT], axis=0).astype(BF16)
        qT_ref[0, hd, 0:QK_HEAD, :] = qb
        qf = qb.astype(F32)
        bound = jnp.sqrt(jnp.sum(qf * qf, axis=0, keepdims=True)) * kbound
        qT_ref[0, hd, QK_HEAD:QK_PAD, :] = jnp.where(brow == 0, -bound, 0.0).astype(BF16)

    knope = _dot(ckv, w_uk_ref[...])
    kcos = kcos_ref[...]
    ksin = ksin_ref[...]
    one_lane = lax.broadcasted_iota(I32, (tm, QK_PAD), 1) == QK_HEAD
    for hd in range(MLA_HEADS):
        kh = knope[:, hd * QK_PAD:(hd + 1) * QK_PAD] + kpe
        r = lax.rsqrt(jnp.sum(kh * kh, axis=-1, keepdims=True) * (1.0 / QK_HEAD) + EPS)
        kb = jnp.where(one_lane, 1.0, (kh * kcos + kpe_sw * ksin) * r).astype(BF16)
        for j in range(n_kt):
            k_ref[0, hd, j] = kb[j * tk:(j + 1) * tk]

    vT = _dot_nt(w_uvT_ref[...], ckv)
    row = lax.broadcasted_iota(I32, (V_ROWS - V_HEAD, tk), 0)
    tail = jnp.where(row == 0, 1.0, 0.0).astype(BF16)
    for hd in range(MLA_HEADS):
        vb = vT[hd * V_HEAD:(hd + 1) * V_HEAD, :].astype(BF16)
        for j in range(n_kt):
            vT_ref[0, hd, j, 0:V_HEAD, :] = vb[:, j * tk:(j + 1) * tk]
            vT_ref[0, hd, j, V_HEAD:V_ROWS, :] = tail


def _mla_prep(x3, g1, wts, tabs, tm, tk):
    B, S, _ = x3.shape
    nb, per = S // tm, tm // tk
    full = lambda a: pl.BlockSpec(a.shape, lambda b, i: (0,) * a.ndim)
    in_specs = [pl.BlockSpec((None, tm, D_MODEL), lambda b, i: (b, i, 0)), full(g1)]
    in_specs += [full(w) for w in wts]
    in_specs += [pl.BlockSpec((tm, LANES), lambda b, i: (i, 0)),
                 pl.BlockSpec((tm, LANES), lambda b, i: (i, 0)),
                 pl.BlockSpec((HALF_ROPE, tm), lambda b, i: (0, i)),
                 pl.BlockSpec((HALF_ROPE, tm), lambda b, i: (0, i))]
    out_shape = [jax.ShapeDtypeStruct((B, MLA_HEADS, QK_PAD, S), BF16),
                 jax.ShapeDtypeStruct((B, MLA_HEADS, S // tk, tk, QK_PAD), BF16),
                 jax.ShapeDtypeStruct((B, MLA_HEADS, S // tk, V_ROWS, tk), BF16)]
    out_specs = [pl.BlockSpec((1, MLA_HEADS, QK_PAD, tm), lambda b, i: (b, 0, 0, i)),
                 pl.BlockSpec((1, MLA_HEADS, per, tk, QK_PAD), lambda b, i: (b, 0, i, 0, 0)),
                 pl.BlockSpec((1, MLA_HEADS, per, V_ROWS, tk), lambda b, i: (b, 0, i, 0, 0))]
    return pl.pallas_call(
        _mla_prep_kernel, grid=(B, nb), in_specs=in_specs, out_specs=out_specs, out_shape=out_shape,
        compiler_params=_cparams(("arbitrary", "arbitrary")), name="mla_prep",
    )(x3, g1, *wts, *tabs)


def _attn_kernel(qT_ref, k_ref, vT_ref, o_ref, *, n_kt):
    tq = qT_ref.shape[3]
    qTs = [qT_ref[0, hh] for hh in range(HEADS_PER_STEP)]

    def body(kt, carry):
        out = []
        for hh in range(HEADS_PER_STEP):
            m, acc = carry[hh]
            sT = _dot(k_ref[0, hh, kt], qTs[hh])
            m_new = jnp.maximum(m, jnp.max(sT, axis=0, keepdims=True))
            alpha = jnp.exp2(m - m_new)
            pT = jnp.exp2(sT - m_new).astype(BF16)
            acc = alpha * acc + _dot(vT_ref[0, hh, kt], pT)
            out.append((m_new, acc))
        return tuple(out)

    init = tuple((jnp.full((1, tq), -jnp.inf, F32), jnp.zeros((V_ROWS, tq), F32))
                 for _ in range(HEADS_PER_STEP))
    res = lax.fori_loop(0, n_kt, body, init, unroll=2)
    outs = [acc[:V_HEAD] / acc[V_HEAD:V_HEAD + 1] for _, acc in res]
    o_ref[0] = jnp.concatenate(outs, axis=0).T.astype(o_ref.dtype)


def _attn_bounded_kernel(qT_ref, k_ref, vT_ref, o_ref, *, n_kt):
    tq = qT_ref.shape[3]
    qTs = [qT_ref[0, hh] for hh in range(HEADS_PER_STEP)]

    def scores(kt):
        return [_dot(k_ref[0, hh, kt], qTs[hh]) for hh in range(HEADS_PER_STEP)]

    def accumulate(accs, kt, sTs):
        return tuple(accs[hh] + _dot(vT_ref[0, hh, kt], jnp.exp2(sTs[hh]).astype(BF16))
                     for hh in range(HEADS_PER_STEP))

    def body(j, accs):
        kt = j * per_trip
        sTs = scores(kt)
        for u in range(1, per_trip):
            nxt = scores(kt + u)
            accs = accumulate(accs, kt + u - 1, sTs)
            sTs = nxt
        return accumulate(accs, kt + per_trip - 1, sTs)

    per_trip = min(KT_PER_TRIP, n_kt)
    assert n_kt % per_trip == 0
    accs = lax.fori_loop(0, n_kt // per_trip, body,
                         tuple(jnp.zeros((V_ROWS, tq), F32) for _ in range(HEADS_PER_STEP)))
    outs = [acc[:V_HEAD] / acc[V_HEAD:V_HEAD + 1] for acc in accs]
    o_ref[0] = jnp.concatenate(outs, axis=0).T.astype(o_ref.dtype)


def _attention(qT, k, vT, tq, bounded):
    B, H, _, S = qT.shape
    n_kt, tk = k.shape[2], k.shape[3]
    hp = H // HEADS_PER_STEP
    return pl.pallas_call(
        functools.partial(_attn_bounded_kernel if bounded else _attn_kernel, n_kt=n_kt),
        grid=(B, hp, S // tq),
        in_specs=[pl.BlockSpec((1, HEADS_PER_STEP, QK_PAD, tq), lambda b, h, i: (b, h, 0, i)),
                  pl.BlockSpec((1, HEADS_PER_STEP, n_kt, tk, QK_PAD), lambda b, h, i: (b, h, 0, 0, 0)),
                  pl.BlockSpec((1, HEADS_PER_STEP, n_kt, V_ROWS, tk), lambda b, h, i: (b, h, 0, 0, 0))],
        out_specs=pl.BlockSpec((1, tq, HEADS_PER_STEP * V_HEAD), lambda b, h, i: (b, i, h)),
        out_shape=jax.ShapeDtypeStruct((B, S, H * V_HEAD), BF16),
        compiler_params=_cparams(("arbitrary", "arbitrary", "arbitrary")),
        name="mla_attention_bounded" if bounded else "mla_attention",
    )(qT, k, vT)


def _router_epilogue(x1, g2_ref, wr_cat_ref, wr_hi_ref, x1_ref, h2_ref, aff_ref):
    x1_ref[...] = x1
    h2 = _rms_rows(x1, g2_ref[...])
    h2_hi = h2.astype(BF16)
    h2_lo = (h2 - h2_hi.astype(F32)).astype(BF16)
    h2_ref[...] = h2_hi
    both = _dot(h2_hi, wr_cat_ref[...])
    logits = both[:, :N_EXPERTS] + both[:, N_EXPERTS:] + _dot(h2_lo, wr_hi_ref[...])
    ex = jnp.exp(logits - jnp.max(logits, axis=-1, keepdims=True))
    aff_ref[...] = ex / jnp.sum(ex, axis=-1, keepdims=True)


def _outproj_kernel(o_ref, w_ref, x_ref, g2_ref, wr_cat_ref, wr_hi_ref, x1_ref, h2_ref, aff_ref):
    x1 = x_ref[...] + _dot(o_ref[...], w_ref[...])
    _router_epilogue(x1, g2_ref, wr_cat_ref, wr_hi_ref, x1_ref, h2_ref, aff_ref)


def _row_specs(n, tm):
    out_shape = [jax.ShapeDtypeStruct((n, D_MODEL), F32),
                 jax.ShapeDtypeStruct((n, D_MODEL), BF16),
                 jax.ShapeDtypeStruct((n, N_EXPERTS), F32)]
    out_specs = [pl.BlockSpec((tm, D_MODEL), lambda i: (i, 0)),
                 pl.BlockSpec((tm, D_MODEL), lambda i: (i, 0)),
                 pl.BlockSpec((tm, N_EXPERTS), lambda i: (i, 0))]
    return out_shape, out_specs


def _outproj(o, w_out, x, g2, wr_cat, wr_hi, tm):
    n = x.shape[0]
    full = lambda a: pl.BlockSpec(a.shape, lambda i: (0,) * a.ndim)
    out_shape, out_specs = _row_specs(n, tm)
    return pl.pallas_call(
        _outproj_kernel, grid=(n // tm,),
        in_specs=[pl.BlockSpec((tm, D_MODEL), lambda i: (i, 0)), full(w_out),
                  pl.BlockSpec((tm, D_MODEL), lambda i: (i, 0)), full(g2), full(wr_cat), full(wr_hi)],
        out_specs=out_specs, out_shape=out_shape,
        compiler_params=_cparams(("arbitrary",)), name="mla_outproj_router",
    )(o, w_out, x, g2, wr_cat, wr_hi)


def _sgu_kernel(x_ref, g1_ref, w_in_ref, lng_ref, lnb_ref, ws_ref, bs_ref, w_out_ref,
                g2_ref, wr_cat_ref, wr_hi_ref, x1_ref, h2_ref, aff_ref):
    x = x_ref[...]
    tm = x.shape[0]
    h = _rms_rows(x, g1_ref[...]).astype(BF16)
    z = _dot(h, w_in_ref[...])
    z = 0.5 * z * (1.0 + lax.erf(z * np.float32(2.0 ** -0.5)))
    u = z[:, :SGU_HALF]
    v = z[:, SGU_HALF:]
    mu = jnp.mean(v, axis=-1, keepdims=True)
    vc = v - mu
    v = vc * lax.rsqrt(jnp.mean(vc * vc, axis=-1, keepdims=True) + EPS) * lng_ref[...] + lnb_ref[...]
    vb = v.astype(BF16)
    bs = bs_ref[...]
    n_chunks = tm // CHUNK
    mixed = []
    for g in range(SGU_GROUPS):
        blk = jnp.concatenate([vb[c * CHUNK:(c + 1) * CHUNK, g * LANES:(g + 1) * LANES]
                               for c in range(n_chunks)], axis=1)
        mixed.append(_dot(ws_ref[g], blk))
    sv = jnp.concatenate(
        [jnp.concatenate([mixed[g][:, c * LANES:(c + 1) * LANES] for g in range(SGU_GROUPS)], axis=1) + bs
         for c in range(n_chunks)], axis=0)
    gated = (u * sv).astype(BF16)
    x1 = x + _dot(gated, w_out_ref[...])
    _router_epilogue(x1, g2_ref, wr_cat_ref, wr_hi_ref, x1_ref, h2_ref, aff_ref)


def _sgu(x, g1, w_in, lng, lnb, ws, bs_full, w_out, g2, wr_cat, wr_hi, tm):
    n = x.shape[0]
    full = lambda a: pl.BlockSpec(a.shape, lambda i: (0,) * a.ndim)
    ops = (g1, w_in, lng, lnb, ws, bs_full, w_out, g2, wr_cat, wr_hi)
    out_shape, out_specs = _row_specs(n, tm)
    return pl.pallas_call(
        _sgu_kernel, grid=(n // tm,),
        in_specs=[pl.BlockSpec((tm, D_MODEL), lambda i: (i, 0))] + [full(a) for a in ops],
        out_specs=out_specs, out_shape=out_shape,
        compiler_params=_cparams(("arbitrary",)), name="sgu_router",
    )(x, *ops)


def _select_kernel(aff_ref, lp_ref, start_ref, cnt_ref, lp_pair_ref, *, nt, cap):
    a = aff_ref[...]
    bits = pltpu.bitcast(a, I32)

    def body(i, cur):
        trial = cur | jnp.left_shift(jnp.int32(1), 30 - i)
        ge = jnp.where(bits >= trial, 1.0, 0.0)
        cnt = jnp.sum(jnp.sum(ge, axis=1, keepdims=True), axis=2, keepdims=True)
        return jnp.where(cnt >= cap, trial, cur)

    thr = lax.fori_loop(0, 31, body, jnp.zeros((N_EXPERTS, 1, 1), I32))
    gtf = jnp.where(bits > thr, 1.0, 0.0)
    eqf = jnp.where(bits == thr, 1.0, 0.0)
    c_gt = jnp.sum(jnp.sum(gtf, axis=1, keepdims=True), axis=2, keepdims=True)
    need = cap - c_gt

    r = lax.broadcasted_iota(I32, (COMB_TILE, COMB_TILE), 0)
    c = lax.broadcasted_iota(I32, (COMB_TILE, COMB_TILE), 1)
    incl_u = jnp.where(r <= c, 1.0, 0.0).astype(BF16)
    ones_u = jnp.ones((COMB_TILE, LANES), BF16)
    rt = lax.broadcasted_iota(I32, (nt, nt), 0)
    ct = lax.broadcasted_iota(I32, (nt, nt), 1)
    excl_l = jnp.where(rt > ct, 1.0, 0.0).astype(BF16)

    def tile_prefix(maskf):
        m2 = maskf.reshape(N_EXPERTS * nt, COMB_TILE).astype(BF16)
        incl = _dot(m2, incl_u)
        cnt = _dot(m2, ones_u)
        cnt_b = cnt.astype(BF16)
        starts = [_dot(excl_l, cnt_b[e * nt:(e + 1) * nt]) for e in range(N_EXPERTS)]
        return incl, jnp.concatenate(starts, axis=0), cnt

    eq_incl, eq_start, _ = tile_prefix(eqf)
    tie_rank = (eq_start + eq_incl).reshape(N_EXPERTS, nt, COMB_TILE) - eqf
    selm = gtf + eqf * jnp.where(tie_rank < need, 1.0, 0.0)
    incl, start, cnt = tile_prefix(selm)
    sel2 = selm.reshape(N_EXPERTS * nt, COMB_TILE) > 0.5
    lp_ref[...] = jnp.where(sel2, incl - 1.0, float(NOT_SELECTED)).astype(I32)
    start_ref[...] = start.astype(I32)
    cnt_ref[...] = cnt.astype(I32)
    odd = (lax.broadcasted_iota(I32, (N_EXPERTS * nt, COMB_TILE), 0) & 1) == 1
    before = jnp.where(odd, jnp.concatenate([jnp.zeros((1, COMB_TILE), F32), cnt[:-1]], axis=0), 0.0)
    lp_pair_ref[...] = jnp.where(sel2, incl - 1.0 + before, float(NOT_SELECTED)).astype(I32)


def _select(aff_t3, cap):
    e, nt, _ = aff_t3.shape
    return pl.pallas_call(
        functools.partial(_select_kernel, nt=nt, cap=cap),
        out_shape=[jax.ShapeDtypeStruct((e * nt, COMB_TILE), I32)] * 4,
        compiler_params=pltpu.CompilerParams(vmem_limit_bytes=VMEM_LIMIT), name="moe_select",
    )(aff_t3)


def _round_down(v):
    return (v >> ROW_ALIGN_SHIFT) << ROW_ALIGN_SHIFT


def _by_slab_length(short, short_rows, long_rows, fn):
    @pl.when(short)
    def _():
        fn(short_rows)

    @pl.when(jnp.logical_not(short))
    def _():
        fn(long_rows)


def _dispatch_kernel(start_sm, cnt_sm, short_sm, x_ref, lp_ref, xg_hbm, stage, carry, sems, *, nt, cap):
    step = pl.program_id(0)

    def copy(s, e, row0, rows):
        dst = xg_hbm.at[e, pl.ds(pl.multiple_of(row0, ROW_ALIGN), rows), :]
        return pltpu.make_async_copy(stage.at[s, e, pl.ds(0, rows)], dst, sems.at[s])

    @pl.when(step == 0)
    def _():
        carry[...] = jnp.zeros(carry.shape, carry.dtype)
        stage[1] = jnp.zeros(stage.shape[1:], stage.dtype)
        for e in range(N_EXPERTS):
            copy(1, e, cap, DISP_LONG).start()
        for e in range(N_EXPERTS):
            copy(1, e, cap, DISP_LONG).wait()

    rio = lax.broadcasted_iota(I32, (ROW_ALIGN, D_MODEL), 0)
    by_length = functools.partial(_by_slab_length, short_rows=DISP_SHORT, long_rows=DISP_LONG)

    for slot in range(2):
        t = step * 2 + slot
        x = x_ref[pl.ds(slot * DISP_TILE, DISP_TILE), :]
        shifts = [start_sm[e * nt + t] - _round_down(start_sm[e * nt + t]) for e in range(N_EXPERTS)]

        def onehot_t(e, rows, slot=slot, shifts=shifts):
            jio = lax.broadcasted_iota(I32, (rows, DISP_TILE), 0)
            return jnp.where(lp_ref[slot, e:e + 1, :] + shifts[e] == jio, 1.0, 0.0).astype(BF16)

        def place(e, slab, rows, slot=slot, shifts=shifts, t=t):
            head = jnp.where(rio < shifts[e], carry[e].astype(F32), slab[0:ROW_ALIGN])
            stage[slot, e, 0:ROW_ALIGN] = head.astype(BF16)
            stage[slot, e, ROW_ALIGN:rows] = slab[ROW_ALIGN:].astype(BF16)
            nxt = pl.multiple_of(_round_down(shifts[e] + cnt_sm[e * nt + t]), ROW_ALIGN)
            carry[e] = stage[slot, e, pl.ds(nxt, ROW_ALIGN)]

        def fill(rows, x=x, onehot_t=onehot_t, place=place):
            if rows == DISP_SHORT:
                slabs = _dot(jnp.concatenate([onehot_t(e, rows) for e in range(N_EXPERTS)], axis=0), x)
                for e in range(N_EXPERTS):
                    place(e, slabs[e * rows:(e + 1) * rows], rows)
            else:
                for e in range(N_EXPERTS):
                    place(e, _dot(onehot_t(e, rows), x), rows)

        short_now = short_sm[t] == 1
        by_length(short_now, fn=fill)

        def wait_previous(t=t, slot=slot):
            by_length(short_sm[t - 1] == 1,
                      fn=lambda rows: [copy(1 - slot, e, 0, rows).wait() for e in range(N_EXPERTS)])

        if slot == 0:
            pl.when(step > 0)(wait_previous)
        else:
            wait_previous()

        by_length(short_now, fn=lambda rows, t=t, slot=slot: [
            copy(slot, e, _round_down(start_sm[e * nt + t]), rows).start() for e in range(N_EXPERTS)])

    @pl.when(step == nt // 2 - 1)
    def _():
        by_length(short_sm[nt - 1] == 1, fn=lambda rows: [copy(1, e, 0, rows).wait() for e in range(N_EXPERTS)])


def _dispatch(start_flat, cnt_flat, short_flat, h2, lp3, cap):
    n = h2.shape[0]
    nt = n // DISP_TILE
    assert nt % 2 == 0
    grid_spec = pltpu.PrefetchScalarGridSpec(
        num_scalar_prefetch=3, grid=(nt // 2,),
        in_specs=[pl.BlockSpec((2 * DISP_TILE, D_MODEL), lambda t, *_: (t, 0)),
                  pl.BlockSpec((2, N_EXPERTS, DISP_TILE), lambda t, *_: (t, 0, 0))],
        out_specs=pl.BlockSpec(memory_space=pl.ANY),
        scratch_shapes=[pltpu.VMEM((2, N_EXPERTS, DISP_LONG, D_MODEL), BF16),
                        pltpu.VMEM((N_EXPERTS, ROW_ALIGN, D_MODEL), BF16),
                        pltpu.SemaphoreType.DMA((2,))])
    return pl.pallas_call(
        functools.partial(_dispatch_kernel, nt=nt, cap=cap), grid_spec=grid_spec,
        out_shape=jax.ShapeDtypeStruct((N_EXPERTS, cap + DISP_LONG, D_MODEL), BF16),
        compiler_params=_cparams(("arbitrary",)), name="moe_dispatch",
    )(start_flat, cnt_flat, short_flat, h2, lp3)


def _ffn_kernel(xg_ref, wg_ref, wu_ref, wd_ref, yg_ref, wg_b, wu_b, wd_b):
    @pl.when(pl.program_id(1) == 0)
    def _():
        wg_b[...] = wg_ref[0, 0].astype(BF16)
        wu_b[...] = wu_ref[0, 0].astype(BF16)
        wd_b[...] = wd_ref[0, 0].astype(BF16)

    x = xg_ref[0]
    g = _dot(x, wg_b[...])
    u = _dot(x, wu_b[...])
    hid = (g * jax.nn.sigmoid(g) * u).astype(BF16)
    y = _dot(hid, wd_b[...])
    yg_ref[0] = y.astype(BF16)


def _ffn(xg, wg, wu, wd, layer, cap, rt):
    wspec = pl.BlockSpec((1, 1, D_MODEL, EXPERT_FF), lambda e, r: (layer, e, 0, 0))
    return pl.pallas_call(
        _ffn_kernel, grid=(N_EXPERTS, cap // rt),
        in_specs=[pl.BlockSpec((1, rt, D_MODEL), lambda e, r: (e, r, 0)), wspec, wspec,
                  pl.BlockSpec((1, 1, EXPERT_FF, D_MODEL), lambda e, r: (layer, e, 0, 0))],
        out_specs=pl.BlockSpec((1, rt, D_MODEL), lambda e, r: (e, r, 0)),
        out_shape=jax.ShapeDtypeStruct((N_EXPERTS, cap, D_MODEL), BF16),
        scratch_shapes=[pltpu.VMEM((D_MODEL, EXPERT_FF), BF16), pltpu.VMEM((D_MODEL, EXPERT_FF), BF16),
                        pltpu.VMEM((EXPERT_FF, D_MODEL), BF16)],
        compiler_params=_cparams(("arbitrary", "arbitrary")), name="moe_ffn",
    )(xg, wg, wu, wd)


def _combine_kernel(start_sm, short_sm, x_ref, lpt_ref, aff_ref, yg_hbm, o_ref, slabs, sems, *, nt, cap):
    t = pl.program_id(0)
    slot = t % 2
    n_steps = nt // COMB_PER_STEP

    def row0(e, tt, rows):
        return jnp.minimum(_round_down(start_sm[e * nt + tt]), cap - rows)

    def copy(s, j, e, r0, rows):
        src = yg_hbm.at[e, pl.ds(pl.multiple_of(r0, ROW_ALIGN), rows), :]
        return pltpu.make_async_copy(src, slabs.at[s, j, e, pl.ds(0, rows)], sems.at[s, j])

    by_length = functools.partial(_by_slab_length, short_rows=COMB_SHORT, long_rows=COMB_LONG)

    def fetch(s, step):
        for j in range(COMB_PER_STEP):
            tt = step * COMB_PER_STEP + j
            by_length(short_sm[tt] == 1,
                      fn=lambda rows, j=j, tt=tt: [copy(s, j, e, row0(e, tt, rows), rows).start()
                                                   for e in range(N_EXPERTS)])

    @pl.when(t == 0)
    def _():
        fetch(0, 0)

    @pl.when(t + 1 < n_steps)
    def _():
        fetch(1 - slot, t + 1)

    def expand_short(j, tt, rows):
        tok = pl.ds(j * COMB_TILE, COMB_TILE)
        for e in range(N_EXPERTS):
            copy(slot, j, e, 0, rows).wait()
        stacked = jnp.concatenate([slabs[slot, j, e, 0:rows] for e in range(N_EXPERTS)], axis=0)
        width = N_EXPERTS * rows
        first = lax.broadcasted_iota(I32, (N_EXPERTS, width), 0) * rows
        lane2 = lax.broadcasted_iota(I32, (N_EXPERTS, width), 1)
        spread = jnp.where((lane2 >= first) & (lane2 < first + rows), 1.0, 0.0).astype(BF16)
        g = aff_ref[tok, :]
        g_hi = g.astype(BF16)
        g_lo = (g - g_hi.astype(F32)).astype(BF16)
        cols = jnp.concatenate([lpt_ref[tok, :].astype(F32).astype(BF16), g_hi, g_lo], axis=0)
        wide = _dot(cols, spread)
        lane = lax.broadcasted_iota(I32, (1, width), 1)
        target = lane
        for e in range(N_EXPERTS):
            delta = start_sm[e * nt + tt] - row0(e, tt, rows)
            target = jnp.where((lane >= e * rows) & (lane < (e + 1) * rows), lane - (e * rows + delta), target)
        hit = wide[:COMB_TILE] == target.astype(F32)
        w = jnp.concatenate([jnp.where(hit, wide[COMB_TILE:2 * COMB_TILE], 0.0).astype(BF16),
                             jnp.where(hit, wide[2 * COMB_TILE:], 0.0).astype(BF16)], axis=0)
        both = _dot(w, stacked)
        o_ref[tok, :] = x_ref[tok, :] + both[:COMB_TILE] + both[COMB_TILE:]

    def expand_long(j, tt, rows):
        tok = pl.ds(j * COMB_TILE, COMB_TILE)
        for e in range(N_EXPERTS):
            copy(slot, j, e, 0, rows).wait()
        acc = x_ref[tok, :]
        lane = lax.broadcasted_iota(I32, (COMB_TILE, rows), 1)
        for e in range(N_EXPERTS):
            delta = start_sm[e * nt + tt] - row0(e, tt, rows)
            onehot = jnp.where(lpt_ref[tok, e:e + 1] + delta == lane, 1.0, 0.0).astype(BF16)
            acc = acc + aff_ref[tok, e:e + 1] * _dot(onehot, slabs[slot, j, e])
        o_ref[tok, :] = acc

    for j in range(COMB_PER_STEP):
        tt = t * COMB_PER_STEP + j
        by_length(short_sm[tt] == 1,
                  fn=lambda rows, j=j, tt=tt: expand_short(j, tt, rows) if rows == COMB_SHORT
                  else expand_long(j, tt, rows))


def _combine(start_flat, short_flat, x1, lp_t, aff, yg, cap):
    n = x1.shape[0]
    nt = n // COMB_TILE
    rows = COMB_TILE * COMB_PER_STEP
    grid_spec = pltpu.PrefetchScalarGridSpec(
        num_scalar_prefetch=2, grid=(nt // COMB_PER_STEP,),
        in_specs=[pl.BlockSpec((rows, D_MODEL), lambda t, s, c: (t, 0)),
                  pl.BlockSpec((rows, N_EXPERTS), lambda t, s, c: (t, 0)),
                  pl.BlockSpec((rows, N_EXPERTS), lambda t, s, c: (t, 0)),
                  pl.BlockSpec(memory_space=pl.ANY)],
        out_specs=pl.BlockSpec((rows, D_MODEL), lambda t, s, c: (t, 0)),
        scratch_shapes=[pltpu.VMEM((2, COMB_PER_STEP, N_EXPERTS, COMB_LONG, D_MODEL), BF16),
                        pltpu.SemaphoreType.DMA((2, COMB_PER_STEP))])
    return pl.pallas_call(
        functools.partial(_combine_kernel, nt=nt, cap=cap), grid_spec=grid_spec,
        out_shape=jax.ShapeDtypeStruct((n, D_MODEL), F32),
        compiler_params=_cparams(("arbitrary",)), name="moe_combine",
    )(start_flat, short_flat, x1, lp_t, aff, yg)


def _moe(x1, h2, aff, wg, wu, wd, layer, ffn_rows):
    n = x1.shape[0]
    nt, ntd = n // COMB_TILE, n // DISP_TILE
    cap = max(1, CAPACITY_FACTOR * n // N_EXPERTS)
    assert DISP_TILE == 2 * COMB_TILE and cap >= DISP_LONG and cap % ROW_ALIGN == 0
    lp, start, cnt, lp_pair = _select(aff.T.reshape(N_EXPERTS, nt, COMB_TILE), cap)
    start_c, cnt_c = start[:, 0].reshape(N_EXPERTS, nt), cnt[:, 0].reshape(N_EXPERTS, nt)
    start_d, cnt_d = start_c[:, ::2], cnt_c[:, ::2] + cnt_c[:, 1::2]

    def short_flags(st, ct, short_rows):
        return (jnp.max((st & (ROW_ALIGN - 1)) + ct, axis=0) <= short_rows - ROW_ALIGN).astype(I32)

    lp_t = lp.reshape(N_EXPERTS, n).T
    lp_tiles = lp_pair.reshape(N_EXPERTS, ntd, DISP_TILE).transpose(1, 0, 2)
    xg = _dispatch(start_d.reshape(-1), cnt_d.reshape(-1), short_flags(start_d, cnt_d, DISP_SHORT), h2, lp_tiles, cap)
    yg = _ffn(xg, wg, wu, wd, layer, cap, min(ffn_rows, cap))
    return _combine(start_c.reshape(-1), short_flags(start_c, cnt_c, COMB_SHORT), x1, lp_t, aff, yg, cap)


def _rope_tables(S):
    pos = jnp.arange(S, dtype=F32)
    inv_freq = 1.0 / (ROPE_THETA ** (jnp.arange(0, ROPE_DIM, 2, dtype=F32) / ROPE_DIM))
    ang = pos[:, None] * inv_freq[None, :]
    return jnp.cos(ang), jnp.sin(ang)


def _prep_mla_weights(w_in, gq, gkv, w_uq, w_ukv, qn_g, kn_g):
    pad = lambda a, lo, hi: jnp.pad(a, ((0, 0), (lo, hi)))
    pe = w_in[:, Q_LORA + KV_LORA:]
    pe_sw = jnp.concatenate([pe[:, HALF_ROPE:], pe[:, :HALF_ROPE]], axis=1)
    w_in_p = jnp.concatenate([w_in[:, :Q_LORA + KV_LORA], pad(pe, NOPE_DIM, QK_PAD - QK_HEAD),
                              pad(pe_sw, NOPE_DIM, QK_PAD - QK_HEAD)], axis=1).astype(BF16)
    w_uqT = w_uq.T.astype(BF16)
    ukv = w_ukv.reshape(KV_LORA, MLA_HEADS, NOPE_DIM + V_HEAD)
    w_uk = jnp.pad(ukv[:, :, :NOPE_DIM], ((0, 0), (0, 0), (0, QK_PAD - NOPE_DIM)))
    w_uk = w_uk.reshape(KV_LORA, MLA_HEADS * QK_PAD).astype(BF16)
    w_uvT = ukv[:, :, NOPE_DIM:].reshape(KV_LORA, MLA_HEADS * V_HEAD).T.astype(BF16)
    gqc = (qn_g * np.float32(QK_HEAD ** -0.5 * np.log2(np.e))).reshape(QK_HEAD, 1)
    kb = (jnp.max(jnp.abs(kn_g)) * np.float32(QK_HEAD ** 0.5 * BOUND_MARGIN)).reshape(1, 1)
    return (w_in_p, gq.reshape(1, Q_LORA), gkv.reshape(1, KV_LORA), w_uqT, w_uk, w_uvT, gqc, kb)


def _score_bound_log2(qn_g, kn_g):
    return (jnp.max(jnp.abs(qn_g)) * jnp.max(jnp.abs(kn_g))
            * np.float32(QK_HEAD * QK_HEAD ** -0.5 * np.log2(np.e) * BOUND_MARGIN * BOUND_MARGIN))


def _prep_k_tables(cos, sin, kn_g):
    S = cos.shape[0]
    z = lambda w: jnp.zeros((S, w), F32)
    cos_l = jnp.concatenate([jnp.ones((S, NOPE_DIM), F32), cos, cos, z(QK_PAD - QK_HEAD)], axis=1)
    sin_l = jnp.concatenate([z(NOPE_DIM), -sin, sin, z(QK_PAD - QK_HEAD)], axis=1)
    g = jnp.pad(kn_g, (0, QK_PAD - QK_HEAD))
    g_sw = jnp.concatenate([jnp.zeros((NOPE_DIM,), F32), kn_g[NOPE_DIM + HALF_ROPE:],
                            kn_g[NOPE_DIM:NOPE_DIM + HALF_ROPE], jnp.zeros((QK_PAD - QK_HEAD,), F32)])
    return cos_l * g[None, :], sin_l * g_sw[None, :]


def _router_weights(w_r):
    hi = w_r.astype(BF16)
    lo = (w_r - hi.astype(F32)).astype(BF16)
    return jnp.concatenate([hi, lo], axis=1), hi


def _trunk(x3, p, tiles):
    B, S, _ = x3.shape
    n = B * S
    cos, sin = _rope_tables(S)
    mw = _prep_mla_weights(p['mla_w_in'][0], p['mla_q_lora_g'][0], p['mla_kv_lora_g'][0], p['mla_w_uq'][0],
                           p['mla_w_ukv'][0], p['mla_q_norm_g'][0], p['mla_k_norm_g'][0])
    kcos, ksin = _prep_k_tables(cos, sin, p['mla_k_norm_g'][0])
    qT, k, vT = _mla_prep(x3, p['norm1_g'][0].reshape(1, D_MODEL), mw, (kcos, ksin, cos.T, sin.T), tiles['tm'], tiles['tk'])
    fits = 2.0 * _score_bound_log2(p['mla_q_norm_g'][0], p['mla_k_norm_g'][0]) <= MAX_SHIFTED_SCORE_RANGE
    o = lax.cond(fits, lambda: _attention(qT, k, vT, tiles['tq'], True),
                 lambda: _attention(qT, k, vT, tiles['tq'], False)).reshape(n, D_MODEL)
    x = x3.reshape(n, D_MODEL)
    wr_cat, wr_hi = _router_weights(p['moe_w_router'][0])
    x1, h2, aff = _outproj(o, p['mla_w_out'][0].astype(BF16), x, p['norm2_g'][0].reshape(1, D_MODEL),
                           wr_cat, wr_hi, tiles['tm_out'])
    x = _moe(x1, h2, aff, p['moe_w_gate'], p['moe_w_up'], p['moe_w_down'], 0, tiles['ffn_rows'])
    wr_cat, wr_hi = _router_weights(p['moe_w_router'][1])
    bs_full = jnp.repeat(p['sgu_b_s'][0].T, LANES, axis=1)
    x1, h2, aff = _sgu(x, p['norm1_g'][1].reshape(1, D_MODEL), p['sgu_w_in'][0].astype(BF16),
                       p['sgu_ln_g'][0].reshape(1, SGU_HALF), p['sgu_ln_b'][0].reshape(1, SGU_HALF),
                       p['sgu_w_s'][0].astype(BF16), bs_full, p['sgu_w_out'][0].astype(BF16),
                       p['norm2_g'][1].reshape(1, D_MODEL), wr_cat, wr_hi, tiles['tm_sgu'])
    x = _moe(x1, h2, aff, p['moe_w_gate'], p['moe_w_up'], p['moe_w_down'], 1, tiles['ffn_rows'])
    return x.reshape(B, S, D_MODEL)


TILES = dict(tk=256, tq=1024, tm=512, tm_out=1024, tm_sgu=1024, ffn_rows=1024)


def kernel(x_prompt, x_sample, norm1_g, norm2_g, mla_w_in, mla_q_lora_g, mla_kv_lora_g, mla_w_uq, mla_w_ukv, mla_q_norm_g, mla_k_norm_g, mla_w_out, sgu_w_in, sgu_ln_g, sgu_ln_b, sgu_w_s, sgu_b_s, sgu_w_out, moe_w_router, moe_w_gate, moe_w_up, moe_w_down):
    p = dict(norm1_g=norm1_g, norm2_g=norm2_g, mla_w_in=mla_w_in, mla_q_lora_g=mla_q_lora_g,
             mla_kv_lora_g=mla_kv_lora_g, mla_w_uq=mla_w_uq, mla_w_ukv=mla_w_ukv, mla_q_norm_g=mla_q_norm_g,
             mla_k_norm_g=mla_k_norm_g, mla_w_out=mla_w_out, sgu_w_in=sgu_w_in, sgu_ln_g=sgu_ln_g,
             sgu_ln_b=sgu_ln_b, sgu_w_s=sgu_w_s, sgu_b_s=sgu_b_s, sgu_w_out=sgu_w_out, moe_w_router=moe_w_router,
             moe_w_gate=moe_w_gate, moe_w_up=moe_w_up, moe_w_down=moe_w_down)
    return (_trunk(x_prompt, p, TILES), _trunk(x_sample, p, TILES))
```

```python
import functools

import jax
import jax.numpy as jnp
import numpy as np
from jax import lax
from jax.experimental import pallas as pl
from jax.experimental.pallas import tpu as pltpu

F32 = jnp.float32
BF16 = jnp.bfloat16
I32 = jnp.int32

D_MODEL = 1024
MLA_HEADS = 16
Q_LORA = 256
KV_LORA = 128
NOPE_DIM = 64
ROPE_DIM = 32
HALF_ROPE = ROPE_DIM // 2
QK_HEAD = NOPE_DIM + ROPE_DIM
V_HEAD = 64
ROPE_THETA = 10000.0
CHUNK = 128
SGU_HALF = D_MODEL
SGU_GROUPS = 8
N_EXPERTS = 16
EXPERT_FF = 1024
CAPACITY_FACTOR = 2
EPS = 1e-6

LANES = 128
QK_PAD = 128
V_ROWS = 80
COMB_TILE = 128
DISP_TILE = 256
COMB_PER_STEP = 2
ROW_ALIGN = 16
ROW_ALIGN_SHIFT = 4
COMB_LONG, COMB_SHORT = COMB_TILE + ROW_ALIGN, 64
DISP_LONG, DISP_SHORT = DISP_TILE + ROW_ALIGN, 80
NOT_SELECTED = -1024
BOUND_MARGIN = 1.0 + 2.0 ** -6
MAX_SHIFTED_SCORE_RANGE = 100.0
HEADS_PER_STEP = 2
KT_PER_TRIP = 32
QCHUNK = 256

VMEM_LIMIT = 52 * 1024 * 1024


def _cparams(sem):
    return pltpu.CompilerParams(dimension_semantics=sem, vmem_limit_bytes=VMEM_LIMIT)


def _dot(a, b):
    return jnp.dot(a, b, preferred_element_type=F32)


def _dot_nt(a, b):
    return lax.dot_general(a, b, (((1,), (1,)), ((), ())), preferred_element_type=F32)


def _rms_rows(x, g):
    return x * lax.rsqrt(jnp.mean(x * x, axis=-1, keepdims=True) + EPS) * g


def _mla_prep_kernel(x_ref, g1_ref, w_in_ref, gq_ref, gkv_ref, w_uqT_ref, w_uk_ref, w_uvT_ref,
                     gqc_ref, kb_ref, kcos_ref, ksin_ref, cosT_ref, sinT_ref,
                     qT_ref, k_ref, vT_ref):
    x = x_ref[...]
    h = _rms_rows(x, g1_ref[...]).astype(BF16)
    proj = _dot(h, w_in_ref[...])
    cq = _rms_rows(proj[:, :Q_LORA], gq_ref[...]).astype(BF16)
    ckv = _rms_rows(proj[:, Q_LORA:Q_LORA + KV_LORA], gkv_ref[...]).astype(BF16)
    kpe = proj[:, 384:512]
    kpe_sw = proj[:, 512:640]
    tm = x.shape[0]
    n_kt, tk = k_ref.shape[2], k_ref.shape[3]

    qT = _dot_nt(w_uqT_ref[...], cq)
    cosT = cosT_ref[...]
    sinT = sinT_ref[...]
    gqc = gqc_ref[...]
    kbound = kb_ref[...]
    brow = lax.broadcasted_iota(I32, (QK_PAD - QK_HEAD, tm), 0)
    for hd in range(MLA_HEADS):
        qh = qT[hd * QK_HEAD:(hd + 1) * QK_HEAD, :]
        r = lax.rsqrt(jnp.sum(qh * qh, axis=0, keepdims=True) * (1.0 / QK_HEAD) + EPS)
        qn = qh * gqc * r
        x1 = qn[NOPE_DIM:NOPE_DIM + HALF_ROPE]
        x2 = qn[NOPE_DIM + HALF_ROPE:QK_HEAD]
        qb = jnp.concatenate([qn[:NOPE_DIM], x1 * cosT - x2 * sinT, x2 * cosT + x1 * sinT], axis=0).astype(BF16)
        qT_ref[0, hd, 0:QK_HEAD, :] = qb
        qf = qb.astype(F32)
        bound = jnp.sqrt(jnp.sum(qf * qf, axis=0, keepdims=True)) * kbound
        qT_ref[0, hd, QK_HEAD:QK_PAD, :] = jnp.where(brow == 0, -bound, 0.0).astype(BF16)

    knope = _dot(ckv, w_uk_ref[...])
    kcos = kcos_ref[...]
    ksin = ksin_ref[...]
    one_lane = lax.broadcasted_iota(I32, (tm, QK_PAD), 1) == QK_HEAD
    for hd in range(MLA_HEADS):
        kh = knope[:, hd * QK_PAD:(hd + 1) * QK_PAD] + kpe
        r = lax.rsqrt(jnp.sum(kh * kh, axis=-1, keepdims=True) * (1.0 / QK_HEAD) + EPS)
        kb = jnp.where(one_lane, 1.0, (kh * kcos + kpe_sw * ksin) * r).astype(BF16)
        for j in range(n_kt):
            k_ref[0, hd, j] = kb[j * tk:(j + 1) * tk]

    vT = _dot_nt(w_uvT_ref[...], ckv)
    row = lax.broadcasted_iota(I32, (V_ROWS - V_HEAD, tk), 0)
    tail = jnp.where(row == 0, 1.0, 0.0).astype(BF16)
    for hd in range(MLA_HEADS):
        vb = vT[hd * V_HEAD:(hd + 1) * V_HEAD, :].astype(BF16)
        for j in range(n_kt):
            vT_ref[0, hd, j, 0:V_HEAD, :] = vb[:, j * tk:(j + 1) * tk]
            vT_ref[0, hd, j, V_HEAD:V_ROWS, :] = tail


def _mla_prep(x3, g1, wts, tabs, tm, tk):
    B, S, _ = x3.shape
    nb, per = S // tm, tm // tk
    full = lambda a: pl.BlockSpec(a.shape, lambda b, i: (0,) * a.ndim)
    in_specs = [pl.BlockSpec((None, tm, D_MODEL), lambda b, i: (b, i, 0)), full(g1)]
    in_specs += [full(w) for w in wts]
    in_specs += [pl.BlockSpec((tm, LANES), lambda b, i: (i, 0)),
                 pl.BlockSpec((tm, LANES), lambda b, i: (i, 0)),
                 pl.BlockSpec((HALF_ROPE, tm), lambda b, i: (0, i)),
                 pl.BlockSpec((HALF_ROPE, tm), lambda b, i: (0, i))]
    out_shape = [jax.ShapeDtypeStruct((B, MLA_HEADS, QK_PAD, S), BF16),
                 jax.ShapeDtypeStruct((B, MLA_HEADS, S // tk, tk, QK_PAD), BF16),
                 jax.ShapeDtypeStruct((B, MLA_HEADS, S // tk, V_ROWS, tk), BF16)]
    out_specs = [pl.BlockSpec((1, MLA_HEADS, QK_PAD, tm), lambda b, i: (b, 0, 0, i)),
                 pl.BlockSpec((1, MLA_HEADS, per, tk, QK_PAD), lambda b, i: (b, 0, i, 0, 0)),
                 pl.BlockSpec((1, MLA_HEADS, per, V_ROWS, tk), lambda b, i: (b, 0, i, 0, 0))]
    return pl.pallas_call(
        _mla_prep_kernel, grid=(B, nb), in_specs=in_specs, out_specs=out_specs, out_shape=out_shape,
        compiler_params=_cparams(("arbitrary", "arbitrary")), name="mla_prep",
    )(x3, g1, *wts, *tabs)


def _attn_kernel(qT_ref, k_ref, vT_ref, o_ref, *, n_kt):
    tq = qT_ref.shape[3]
    qTs = [qT_ref[0, hh] for hh in range(HEADS_PER_STEP)]

    def body(kt, carry):
        out = []
        for hh in range(HEADS_PER_STEP):
            m, acc = carry[hh]
            sT = _dot(k_ref[0, hh, kt], qTs[hh])
            m_new = jnp.maximum(m, jnp.max(sT, axis=0, keepdims=True))
            alpha = jnp.exp2(m - m_new)
            pT = jnp.exp2(sT - m_new).astype(BF16)
            acc = alpha * acc + _dot(vT_ref[0, hh, kt], pT)
            out.append((m_new, acc))
        return tuple(out)

    init = tuple((jnp.full((1, tq), -jnp.inf, F32), jnp.zeros((V_ROWS, tq), F32))
                 for _ in range(HEADS_PER_STEP))
    res = lax.fori_loop(0, n_kt, body, init, unroll=2)
    outs = [acc[:V_HEAD] / acc[V_HEAD:V_HEAD + 1] for _, acc in res]
    o_ref[0] = jnp.concatenate(outs, axis=0).T.astype(o_ref.dtype)


def _attn_bounded_kernel(qT_ref, k_ref, vT_ref, o_ref, *, n_kt):
    tq = qT_ref.shape[3]
    n_qc = tq // QCHUNK
    chains = [(hh, qc) for hh in range(HEADS_PER_STEP) for qc in range(n_qc)]
    qTs = [qT_ref[0, hh, :, qc * QCHUNK:(qc + 1) * QCHUNK] for hh, qc in chains]

    def scores(kt):
        return [_dot(k_ref[0, hh, kt], qTs[i]) for i, (hh, qc) in enumerate(chains)]

    def accumulate(accs, kt, sTs):
        return tuple(accs[i] + _dot(vT_ref[0, hh, kt], jnp.exp2(sTs[i]).astype(BF16))
                     for i, (hh, qc) in enumerate(chains))

    def body(j, accs):
        kt = j * per_trip
        sTs = scores(kt)
        accs = list(accs)
        for u in range(1, per_trip):
            for i, (hh, qc) in enumerate(chains):
                nxt_i = _dot(k_ref[0, hh, kt + u], qTs[i])
                accs[i] = accs[i] + _dot(vT_ref[0, hh, kt + u - 1], jnp.exp2(sTs[i]).astype(BF16))
                sTs[i] = nxt_i
        return accumulate(tuple(accs), kt + per_trip - 1, sTs)

    per_trip = min(KT_PER_TRIP, n_kt)
    assert n_kt % per_trip == 0
    accs = lax.fori_loop(0, n_kt // per_trip, body,
                         tuple(jnp.zeros((V_ROWS, QCHUNK), F32) for _ in chains))
    outs = [acc[:V_HEAD] / acc[V_HEAD:V_HEAD + 1] for acc in accs]
    heads = [jnp.concatenate(outs[hh * n_qc:(hh + 1) * n_qc], axis=1) for hh in range(HEADS_PER_STEP)]
    o_ref[0] = jnp.concatenate(heads, axis=0).T.astype(o_ref.dtype)


def _attention(qT, k, vT, tq, bounded):
    B, H, _, S = qT.shape
    n_kt, tk = k.shape[2], k.shape[3]
    hp = H // HEADS_PER_STEP
    return pl.pallas_call(
        functools.partial(_attn_bounded_kernel if bounded else _attn_kernel, n_kt=n_kt),
        grid=(B, hp, S // tq),
        in_specs=[pl.BlockSpec((1, HEADS_PER_STEP, QK_PAD, tq), lambda b, h, i: (b, h, 0, i)),
                  pl.BlockSpec((1, HEADS_PER_STEP, n_kt, tk, QK_PAD), lambda b, h, i: (b, h, 0, 0, 0)),
                  pl.BlockSpec((1, HEADS_PER_STEP, n_kt, V_ROWS, tk), lambda b, h, i: (b, h, 0, 0, 0))],
        out_specs=pl.BlockSpec((1, tq, HEADS_PER_STEP * V_HEAD), lambda b, h, i: (b, i, h)),
        out_shape=jax.ShapeDtypeStruct((B, S, H * V_HEAD), BF16),
        compiler_params=_cparams(("arbitrary", "arbitrary", "arbitrary")),
        name="mla_attention_bounded" if bounded else "mla_attention",
    )(qT, k, vT)


def _router_epilogue(x1, g2_ref, wr_cat_ref, wr_hi_ref, x1_ref, h2_ref, aff_ref):
    x1_ref[...] = x1
    h2 = _rms_rows(x1, g2_ref[...])
    h2_hi = h2.astype(BF16)
    h2_lo = (h2 - h2_hi.astype(F32)).astype(BF16)
    h2_ref[...] = h2_hi
    both = _dot(h2_hi, wr_cat_ref[...])
    logits = both[:, :N_EXPERTS] + both[:, N_EXPERTS:] + _dot(h2_lo, wr_hi_ref[...])
    ex = jnp.exp(logits - jnp.max(logits, axis=-1, keepdims=True))
    aff_ref[...] = ex / jnp.sum(ex, axis=-1, keepdims=True)


def _outproj_kernel(o_ref, w_ref, x_ref, g2_ref, wr_cat_ref, wr_hi_ref, x1_ref, h2_ref, aff_ref):
    x1 = x_ref[...] + _dot(o_ref[...], w_ref[...])
    _router_epilogue(x1, g2_ref, wr_cat_ref, wr_hi_ref, x1_ref, h2_ref, aff_ref)


def _row_specs(n, tm):
    out_shape = [jax.ShapeDtypeStruct((n, D_MODEL), F32),
                 jax.ShapeDtypeStruct((n, D_MODEL), BF16),
                 jax.ShapeDtypeStruct((n, N_EXPERTS), F32)]
    out_specs = [pl.BlockSpec((tm, D_MODEL), lambda i: (i, 0)),
                 pl.BlockSpec((tm, D_MODEL), lambda i: (i, 0)),
                 pl.BlockSpec((tm, N_EXPERTS), lambda i: (i, 0))]
    return out_shape, out_specs


def _outproj(o, w_out, x, g2, wr_cat, wr_hi, tm):
    n = x.shape[0]
    full = lambda a: pl.BlockSpec(a.shape, lambda i: (0,) * a.ndim)
    out_shape, out_specs = _row_specs(n, tm)
    return pl.pallas_call(
        _outproj_kernel, grid=(n // tm,),
        in_specs=[pl.BlockSpec((tm, D_MODEL), lambda i: (i, 0)), full(w_out),
                  pl.BlockSpec((tm, D_MODEL), lambda i: (i, 0)), full(g2), full(wr_cat), full(wr_hi)],
        out_specs=out_specs, out_shape=out_shape,
        compiler_params=_cparams(("arbitrary",)), name="mla_outproj_router",
    )(o, w_out, x, g2, wr_cat, wr_hi)


def _sgu_kernel(x_ref, g1_ref, w_in_ref, lng_ref, lnb_ref, ws_ref, bs_ref, w_out_ref,
                g2_ref, wr_cat_ref, wr_hi_ref, x1_ref, h2_ref, aff_ref):
    x = x_ref[...]
    tm = x.shape[0]
    h = _rms_rows(x, g1_ref[...]).astype(BF16)
    z = _dot(h, w_in_ref[...])
    z = 0.5 * z * (1.0 + lax.erf(z * np.float32(2.0 ** -0.5)))
    u = z[:, :SGU_HALF]
    v = z[:, SGU_HALF:]
    mu = jnp.mean(v, axis=-1, keepdims=True)
    vc = v - mu
    v = vc * lax.rsqrt(jnp.mean(vc * vc, axis=-1, keepdims=True) + EPS) * lng_ref[...] + lnb_ref[...]
    vb = v.astype(BF16)
    bs = bs_ref[...]
    n_chunks = tm // CHUNK
    mixed = []
    for g in range(SGU_GROUPS):
        blk = jnp.concatenate([vb[c * CHUNK:(c + 1) * CHUNK, g * LANES:(g + 1) * LANES]
                               for c in range(n_chunks)], axis=1)
        mixed.append(_dot(ws_ref[g], blk))
    sv = jnp.concatenate(
        [jnp.concatenate([mixed[g][:, c * LANES:(c + 1) * LANES] for g in range(SGU_GROUPS)], axis=1) + bs
         for c in range(n_chunks)], axis=0)
    gated = (u * sv).astype(BF16)
    x1 = x + _dot(gated, w_out_ref[...])
    _router_epilogue(x1, g2_ref, wr_cat_ref, wr_hi_ref, x1_ref, h2_ref, aff_ref)


def _sgu(x, g1, w_in, lng, lnb, ws, bs_full, w_out, g2, wr_cat, wr_hi, tm):
    n = x.shape[0]
    full = lambda a: pl.BlockSpec(a.shape, lambda i: (0,) * a.ndim)
    ops = (g1, w_in, lng, lnb, ws, bs_full, w_out, g2, wr_cat, wr_hi)
    out_shape, out_specs = _row_specs(n, tm)
    return pl.pallas_call(
        _sgu_kernel, grid=(n // tm,),
        in_specs=[pl.BlockSpec((tm, D_MODEL), lambda i: (i, 0))] + [full(a) for a in ops],
        out_specs=out_specs, out_shape=out_shape,
        compiler_params=_cparams(("arbitrary",)), name="sgu_router",
    )(x, *ops)


def _select_kernel(aff_ref, lp_ref, start_ref, cnt_ref, lp_pair_ref, *, nt, cap):
    a = aff_ref[...]
    bits = pltpu.bitcast(a, I32)

    def body(i, cur):
        trial = cur | jnp.left_shift(jnp.int32(1), 30 - i)
        ge = jnp.where(bits >= trial, 1.0, 0.0)
        cnt = jnp.sum(jnp.sum(ge, axis=1, keepdims=True), axis=2, keepdims=True)
        return jnp.where(cnt >= cap, trial, cur)

    thr = lax.fori_loop(0, 31, body, jnp.zeros((N_EXPERTS, 1, 1), I32))
    gtf = jnp.where(bits > thr, 1.0, 0.0)
    eqf = jnp.where(bits == thr, 1.0, 0.0)
    c_gt = jnp.sum(jnp.sum(gtf, axis=1, keepdims=True), axis=2, keepdims=True)
    need = cap - c_gt

    r = lax.broadcasted_iota(I32, (COMB_TILE, COMB_TILE), 0)
    c = lax.broadcasted_iota(I32, (COMB_TILE, COMB_TILE), 1)
    incl_u = jnp.where(r <= c, 1.0, 0.0).astype(BF16)
    ones_u = jnp.ones((COMB_TILE, LANES), BF16)
    rt = lax.broadcasted_iota(I32, (nt, nt), 0)
    ct = lax.broadcasted_iota(I32, (nt, nt), 1)
    excl_l = jnp.where(rt > ct, 1.0, 0.0).astype(BF16)

    def tile_prefix(maskf):
        m2 = maskf.reshape(N_EXPERTS * nt, COMB_TILE).astype(BF16)
        incl = _dot(m2, incl_u)
        cnt = _dot(m2, ones_u)
        cnt_b = cnt.astype(BF16)
        starts = [_dot(excl_l, cnt_b[e * nt:(e + 1) * nt]) for e in range(N_EXPERTS)]
        return incl, jnp.concatenate(starts, axis=0), cnt

    eq_incl, eq_start, _ = tile_prefix(eqf)
    tie_rank = (eq_start + eq_incl).reshape(N_EXPERTS, nt, COMB_TILE) - eqf
    selm = gtf + eqf * jnp.where(tie_rank < need, 1.0, 0.0)
    incl, start, cnt = tile_prefix(selm)
    sel2 = selm.reshape(N_EXPERTS * nt, COMB_TILE) > 0.5
    lp_ref[...] = jnp.where(sel2, incl - 1.0, float(NOT_SELECTED)).astype(I32)
    start_ref[...] = start.astype(I32)
    cnt_ref[...] = cnt.astype(I32)
    odd = (lax.broadcasted_iota(I32, (N_EXPERTS * nt, COMB_TILE), 0) & 1) == 1
    before = jnp.where(odd, jnp.concatenate([jnp.zeros((1, COMB_TILE), F32), cnt[:-1]], axis=0), 0.0)
    lp_pair_ref[...] = jnp.where(sel2, incl - 1.0 + before, float(NOT_SELECTED)).astype(I32)


def _select(aff_t3, cap):
    e, nt, _ = aff_t3.shape
    return pl.pallas_call(
        functools.partial(_select_kernel, nt=nt, cap=cap),
        out_shape=[jax.ShapeDtypeStruct((e * nt, COMB_TILE), I32)] * 4,
        compiler_params=pltpu.CompilerParams(vmem_limit_bytes=VMEM_LIMIT), name="moe_select",
    )(aff_t3)


def _round_down(v):
    return (v >> ROW_ALIGN_SHIFT) << ROW_ALIGN_SHIFT


def _by_slab_length(short, short_rows, long_rows, fn):
    @pl.when(short)
    def _():
        fn(short_rows)

    @pl.when(jnp.logical_not(short))
    def _():
        fn(long_rows)


def _dispatch_kernel(start_sm, cnt_sm, short_sm, x_ref, lp_ref, xg_hbm, stage, carry, sems, *, nt, cap):
    step = pl.program_id(0)

    def copy(s, e, row0, rows):
        dst = xg_hbm.at[e, pl.ds(pl.multiple_of(row0, ROW_ALIGN), rows), :]
        return pltpu.make_async_copy(stage.at[s, e, pl.ds(0, rows)], dst, sems.at[s, e])

    @pl.when(step == 0)
    def _():
        carry[...] = jnp.zeros(carry.shape, carry.dtype)
        stage[1] = jnp.zeros(stage.shape[1:], stage.dtype)
        for e in range(N_EXPERTS):
            copy(1, e, cap, DISP_LONG).start()
        for e in range(N_EXPERTS):
            copy(1, e, cap, DISP_LONG).wait()

    rio = lax.broadcasted_iota(I32, (ROW_ALIGN, D_MODEL), 0)
    by_length = functools.partial(_by_slab_length, short_rows=DISP_SHORT, long_rows=DISP_LONG)

    for slot in range(2):
        t = step * 2 + slot
        x = x_ref[pl.ds(slot * DISP_TILE, DISP_TILE), :]
        shifts = [start_sm[e * nt + t] - _round_down(start_sm[e * nt + t]) for e in range(N_EXPERTS)]

        def onehot_t(e, rows, slot=slot, shifts=shifts):
            jio = lax.broadcasted_iota(I32, (rows, DISP_TILE), 0)
            return jnp.where(lp_ref[slot, e:e + 1, :] + shifts[e] == jio, 1.0, 0.0).astype(BF16)

        def place(e, slab, rows, slot=slot, shifts=shifts, t=t):
            head = jnp.where(rio < shifts[e], carry[e].astype(F32), slab[0:ROW_ALIGN])
            stage[slot, e, 0:ROW_ALIGN] = head.astype(BF16)
            stage[slot, e, ROW_ALIGN:rows] = slab[ROW_ALIGN:].astype(BF16)
            nxt = pl.multiple_of(_round_down(shifts[e] + cnt_sm[e * nt + t]), ROW_ALIGN)
            carry[e] = stage[slot, e, pl.ds(nxt, ROW_ALIGN)]

        def fill(rows, x=x, onehot_t=onehot_t, place=place):
            if rows == DISP_SHORT:
                slabs = _dot(jnp.concatenate([onehot_t(e, rows) for e in range(N_EXPERTS)], axis=0), x)
                for e in range(N_EXPERTS):
                    place(e, slabs[e * rows:(e + 1) * rows], rows)
            else:
                for e in range(N_EXPERTS):
                    place(e, _dot(onehot_t(e, rows), x), rows)

        short_now = short_sm[t] == 1
        by_length(short_now, fn=fill)

        def hand_over(rows, t=t, slot=slot):
            def per_expert(prev_rows):
                for e in range(N_EXPERTS):
                    if prev_rows is not None:
                        copy(1 - slot, e, 0, prev_rows).wait()
                    copy(slot, e, _round_down(start_sm[e * nt + t]), rows).start()

            if slot == 0:
                pl.when(step == 0)(lambda: per_expert(None))
                pl.when(step > 0)(lambda: by_length(short_sm[t - 1] == 1, fn=per_expert))
            else:
                by_length(short_sm[t - 1] == 1, fn=per_expert)

        by_length(short_now, fn=hand_over)

    @pl.when(step == nt // 2 - 1)
    def _():
        by_length(short_sm[nt - 1] == 1, fn=lambda rows: [copy(1, e, 0, rows).wait() for e in range(N_EXPERTS)])


def _dispatch(start_flat, cnt_flat, short_flat, h2, lp3, cap):
    n = h2.shape[0]
    nt = n // DISP_TILE
    assert nt % 2 == 0
    grid_spec = pltpu.PrefetchScalarGridSpec(
        num_scalar_prefetch=3, grid=(nt // 2,),
        in_specs=[pl.BlockSpec((2 * DISP_TILE, D_MODEL), lambda t, *_: (t, 0)),
                  pl.BlockSpec((2, N_EXPERTS, DISP_TILE), lambda t, *_: (t, 0, 0))],
        out_specs=pl.BlockSpec(memory_space=pl.ANY),
        scratch_shapes=[pltpu.VMEM((2, N_EXPERTS, DISP_LONG, D_MODEL), BF16),
                        pltpu.VMEM((N_EXPERTS, ROW_ALIGN, D_MODEL), BF16),
                        pltpu.SemaphoreType.DMA((2, N_EXPERTS))])
    return pl.pallas_call(
        functools.partial(_dispatch_kernel, nt=nt, cap=cap), grid_spec=grid_spec,
        out_shape=jax.ShapeDtypeStruct((N_EXPERTS, cap + DISP_LONG, D_MODEL), BF16),
        compiler_params=_cparams(("arbitrary",)), name="moe_dispatch",
    )(start_flat, cnt_flat, short_flat, h2, lp3)


def _ffn_kernel(xg_ref, wg_ref, wu_ref, wd_ref, yg_ref, wg_b, wu_b, wd_b):
    @pl.when(pl.program_id(1) == 0)
    def _():
        wg_b[...] = wg_ref[0, 0].astype(BF16)
        wu_b[...] = wu_ref[0, 0].astype(BF16)
        wd_b[...] = wd_ref[0, 0].astype(BF16)

    x = xg_ref[0]
    g = _dot(x, wg_b[...])
    u = _dot(x, wu_b[...])
    hid = (g * jax.nn.sigmoid(g) * u).astype(BF16)
    y = _dot(hid, wd_b[...])
    yg_ref[0] = y.astype(BF16)


def _ffn(xg, wg, wu, wd, layer, cap, rt):
    wspec = pl.BlockSpec((1, 1, D_MODEL, EXPERT_FF), lambda e, r: (layer, e, 0, 0))
    return pl.pallas_call(
        _ffn_kernel, grid=(N_EXPERTS, cap // rt),
        in_specs=[pl.BlockSpec((1, rt, D_MODEL), lambda e, r: (e, r, 0)), wspec, wspec,
                  pl.BlockSpec((1, 1, EXPERT_FF, D_MODEL), lambda e, r: (layer, e, 0, 0))],
        out_specs=pl.BlockSpec((1, rt, D_MODEL), lambda e, r: (e, r, 0)),
        out_shape=jax.ShapeDtypeStruct((N_EXPERTS, cap, D_MODEL), BF16),
        scratch_shapes=[pltpu.VMEM((D_MODEL, EXPERT_FF), BF16), pltpu.VMEM((D_MODEL, EXPERT_FF), BF16),
                        pltpu.VMEM((EXPERT_FF, D_MODEL), BF16)],
        compiler_params=_cparams(("arbitrary", "arbitrary")), name="moe_ffn",
    )(xg, wg, wu, wd)


def _combine_kernel(start_sm, short_sm, x_ref, lpt_ref, aff_ref, yg_hbm, o_ref, slabs, sems, *, nt, cap):
    t = pl.program_id(0)
    slot = t % 2
    n_steps = nt // COMB_PER_STEP

    def row0(e, tt, rows):
        return jnp.minimum(_round_down(start_sm[e * nt + tt]), cap - rows)

    def copy(s, j, e, r0, rows):
        src = yg_hbm.at[e, pl.ds(pl.multiple_of(r0, ROW_ALIGN), rows), :]
        return pltpu.make_async_copy(src, slabs.at[s, j, e, pl.ds(0, rows)], sems.at[s, j])

    by_length = functools.partial(_by_slab_length, short_rows=COMB_SHORT, long_rows=COMB_LONG)

    def fetch(s, step):
        for j in range(COMB_PER_STEP):
            tt = step * COMB_PER_STEP + j
            by_length(short_sm[tt] == 1,
                      fn=lambda rows, j=j, tt=tt: [copy(s, j, e, row0(e, tt, rows), rows).start()
                                                   for e in range(N_EXPERTS)])

    @pl.when(t == 0)
    def _():
        fetch(0, 0)

    @pl.when(t + 1 < n_steps)
    def _():
        fetch(1 - slot, t + 1)

    def expand_short(j, tt, rows):
        tok = pl.ds(j * COMB_TILE, COMB_TILE)
        for e in range(N_EXPERTS):
            copy(slot, j, e, 0, rows).wait()
        stacked = jnp.concatenate([slabs[slot, j, e, 0:rows] for e in range(N_EXPERTS)], axis=0)
        width = N_EXPERTS * rows
        first = lax.broadcasted_iota(I32, (N_EXPERTS, width), 0) * rows
        lane2 = lax.broadcasted_iota(I32, (N_EXPERTS, width), 1)
        spread = jnp.where((lane2 >= first) & (lane2 < first + rows), 1.0, 0.0).astype(BF16)
        g = aff_ref[tok, :]
        g_hi = g.astype(BF16)
        g_lo = (g - g_hi.astype(F32)).astype(BF16)
        cols = jnp.concatenate([lpt_ref[tok, :].astype(F32).astype(BF16), g_hi, g_lo], axis=0)
        wide = _dot(cols, spread)
        lane = lax.broadcasted_iota(I32, (1, width), 1)
        target = lane
        for e in range(N_EXPERTS):
            delta = start_sm[e * nt + tt] - row0(e, tt, rows)
            target = jnp.where((lane >= e * rows) & (lane < (e + 1) * rows), lane - (e * rows + delta), target)
        hit = wide[:COMB_TILE] == target.astype(F32)
        w = jnp.concatenate([jnp.where(hit, wide[COMB_TILE:2 * COMB_TILE], 0.0).astype(BF16),
                             jnp.where(hit, wide[2 * COMB_TILE:], 0.0).astype(BF16)], axis=0)
        both = _dot(w, stacked)
        o_ref[tok, :] = x_ref[tok, :] + both[:COMB_TILE] + both[COMB_TILE:]

    def expand_long(j, tt, rows):
        tok = pl.ds(j * COMB_TILE, COMB_TILE)
        for e in range(N_EXPERTS):
            copy(slot, j, e, 0, rows).wait()
        acc = x_ref[tok, :]
        lane = lax.broadcasted_iota(I32, (COMB_TILE, rows), 1)
        for e in range(N_EXPERTS):
            delta = start_sm[e * nt + tt] - row0(e, tt, rows)
            onehot = jnp.where(lpt_ref[tok, e:e + 1] + delta == lane, 1.0, 0.0).astype(BF16)
            acc = acc + aff_ref[tok, e:e + 1] * _dot(onehot, slabs[slot, j, e])
        o_ref[tok, :] = acc

    for j in range(COMB_PER_STEP):
        tt = t * COMB_PER_STEP + j
        by_length(short_sm[tt] == 1,
                  fn=lambda rows, j=j, tt=tt: expand_short(j, tt, rows) if rows == COMB_SHORT
                  else expand_long(j, tt, rows))


def _combine(start_flat, short_flat, x1, lp_t, aff, yg, cap):
    n = x1.shape[0]
    nt = n // COMB_TILE
    rows = COMB_TILE * COMB_PER_STEP
    grid_spec = pltpu.PrefetchScalarGridSpec(
        num_scalar_prefetch=2, grid=(nt // COMB_PER_STEP,),
        in_specs=[pl.BlockSpec((rows, D_MODEL), lambda t, s, c: (t, 0)),
                  pl.BlockSpec((rows, N_EXPERTS), lambda t, s, c: (t, 0)),
                  pl.BlockSpec((rows, N_EXPERTS), lambda t, s, c: (t, 0)),
                  pl.BlockSpec(memory_space=pl.ANY)],
        out_specs=pl.BlockSpec((rows, D_MODEL), lambda t, s, c: (t, 0)),
        scratch_shapes=[pltpu.VMEM((2, COMB_PER_STEP, N_EXPERTS, COMB_LONG, D_MODEL), BF16),
                        pltpu.SemaphoreType.DMA((2, COMB_PER_STEP))])
    return pl.pallas_call(
        functools.partial(_combine_kernel, nt=nt, cap=cap), grid_spec=grid_spec,
        out_shape=jax.ShapeDtypeStruct((n, D_MODEL), F32),
        compiler_params=_cparams(("arbitrary",)), name="moe_combine",
    )(start_flat, short_flat, x1, lp_t, aff, yg)


def _moe(x1, h2, aff, wg, wu, wd, layer, ffn_rows):
    n = x1.shape[0]
    nt, ntd = n // COMB_TILE, n // DISP_TILE
    cap = max(1, CAPACITY_FACTOR * n // N_EXPERTS)
    assert DISP_TILE == 2 * COMB_TILE and cap >= DISP_LONG and cap % ROW_ALIGN == 0
    lp, start, cnt, lp_pair = _select(aff.T.reshape(N_EXPERTS, nt, COMB_TILE), cap)
    start_c, cnt_c = start[:, 0].reshape(N_EXPERTS, nt), cnt[:, 0].reshape(N_EXPERTS, nt)
    start_d, cnt_d = start_c[:, ::2], cnt_c[:, ::2] + cnt_c[:, 1::2]

    def short_flags(st, ct, short_rows):
        return (jnp.max((st & (ROW_ALIGN - 1)) + ct, axis=0) <= short_rows - ROW_ALIGN).astype(I32)

    lp_t = lp.reshape(N_EXPERTS, n).T
    lp_tiles = lp_pair.reshape(N_EXPERTS, ntd, DISP_TILE).transpose(1, 0, 2)
    xg = _dispatch(start_d.reshape(-1), cnt_d.reshape(-1), short_flags(start_d, cnt_d, DISP_SHORT), h2, lp_tiles, cap)
    yg = _ffn(xg, wg, wu, wd, layer, cap, min(ffn_rows, cap))
    return _combine(start_c.reshape(-1), short_flags(start_c, cnt_c, COMB_SHORT), x1, lp_t, aff, yg, cap)


def _rope_tables(S):
    pos = jnp.arange(S, dtype=F32)
    inv_freq = 1.0 / (ROPE_THETA ** (jnp.arange(0, ROPE_DIM, 2, dtype=F32) / ROPE_DIM))
    ang = pos[:, None] * inv_freq[None, :]
    return jnp.cos(ang), jnp.sin(ang)


def _prep_mla_weights(w_in, gq, gkv, w_uq, w_ukv, qn_g, kn_g):
    pad = lambda a, lo, hi: jnp.pad(a, ((0, 0), (lo, hi)))
    pe = w_in[:, Q_LORA + KV_LORA:]
    pe_sw = jnp.concatenate([pe[:, HALF_ROPE:], pe[:, :HALF_ROPE]], axis=1)
    w_in_p = jnp.concatenate([w_in[:, :Q_LORA + KV_LORA], pad(pe, NOPE_DIM, QK_PAD - QK_HEAD),
                              pad(pe_sw, NOPE_DIM, QK_PAD - QK_HEAD)], axis=1).astype(BF16)
    w_uqT = w_uq.T.astype(BF16)
    ukv = w_ukv.reshape(KV_LORA, MLA_HEADS, NOPE_DIM + V_HEAD)
    w_uk = jnp.pad(ukv[:, :, :NOPE_DIM], ((0, 0), (0, 0), (0, QK_PAD - NOPE_DIM)))
    w_uk = w_uk.reshape(KV_LORA, MLA_HEADS * QK_PAD).astype(BF16)
    w_uvT = ukv[:, :, NOPE_DIM:].reshape(KV_LORA, MLA_HEADS * V_HEAD).T.astype(BF16)
    gqc = (qn_g * np.float32(QK_HEAD ** -0.5 * np.log2(np.e))).reshape(QK_HEAD, 1)
    kb = (jnp.max(jnp.abs(kn_g)) * np.float32(QK_HEAD ** 0.5 * BOUND_MARGIN)).reshape(1, 1)
    return (w_in_p, gq.reshape(1, Q_LORA), gkv.reshape(1, KV_LORA), w_uqT, w_uk, w_uvT, gqc, kb)


def _score_bound_log2(qn_g, kn_g):
    return (jnp.max(jnp.abs(qn_g)) * jnp.max(jnp.abs(kn_g))
            * np.float32(QK_HEAD * QK_HEAD ** -0.5 * np.log2(np.e) * BOUND_MARGIN * BOUND_MARGIN))


def _prep_k_tables(cos, sin, kn_g):
    S = cos.shape[0]
    z = lambda w: jnp.zeros((S, w), F32)
    cos_l = jnp.concatenate([jnp.ones((S, NOPE_DIM), F32), cos, cos, z(QK_PAD - QK_HEAD)], axis=1)
    sin_l = jnp.concatenate([z(NOPE_DIM), -sin, sin, z(QK_PAD - QK_HEAD)], axis=1)
    g = jnp.pad(kn_g, (0, QK_PAD - QK_HEAD))
    g_sw = jnp.concatenate([jnp.zeros((NOPE_DIM,), F32), kn_g[NOPE_DIM + HALF_ROPE:],
                            kn_g[NOPE_DIM:NOPE_DIM + HALF_ROPE], jnp.zeros((QK_PAD - QK_HEAD,), F32)])
    return cos_l * g[None, :], sin_l * g_sw[None, :]


def _router_weights(w_r):
    hi = w_r.astype(BF16)
    lo = (w_r - hi.astype(F32)).astype(BF16)
    return jnp.concatenate([hi, lo], axis=1), hi


def _trunk(x3, p, tiles):
    B, S, _ = x3.shape
    n = B * S
    cos, sin = _rope_tables(S)
    mw = _prep_mla_weights(p['mla_w_in'][0], p['mla_q_lora_g'][0], p['mla_kv_lora_g'][0], p['mla_w_uq'][0],
                           p['mla_w_ukv'][0], p['mla_q_norm_g'][0], p['mla_k_norm_g'][0])
    kcos, ksin = _prep_k_tables(cos, sin, p['mla_k_norm_g'][0])
    qT, k, vT = _mla_prep(x3, p['norm1_g'][0].reshape(1, D_MODEL), mw, (kcos, ksin, cos.T, sin.T), tiles['tm'], tiles['tk'])
    fits = 2.0 * _score_bound_log2(p['mla_q_norm_g'][0], p['mla_k_norm_g'][0]) <= MAX_SHIFTED_SCORE_RANGE
    o = lax.cond(fits, lambda: _attention(qT, k, vT, tiles['tq'], True),
                 lambda: _attention(qT, k, vT, tiles['tq'], False)).reshape(n, D_MODEL)
    x = x3.reshape(n, D_MODEL)
    wr_cat, wr_hi = _router_weights(p['moe_w_router'][0])
    x1, h2, aff = _outproj(o, p['mla_w_out'][0].astype(BF16), x, p['norm2_g'][0].reshape(1, D_MODEL),
                           wr_cat, wr_hi, tiles['tm_out'])
    x = _moe(x1, h2, aff, p['moe_w_gate'], p['moe_w_up'], p['moe_w_down'], 0, tiles['ffn_rows'])
    wr_cat, wr_hi = _router_weights(p['moe_w_router'][1])
    bs_full = jnp.repeat(p['sgu_b_s'][0].T, LANES, axis=1)
    x1, h2, aff = _sgu(x, p['norm1_g'][1].reshape(1, D_MODEL), p['sgu_w_in'][0].astype(BF16),
                       p['sgu_ln_g'][0].reshape(1, SGU_HALF), p['sgu_ln_b'][0].reshape(1, SGU_HALF),
                       p['sgu_w_s'][0].astype(BF16), bs_full, p['sgu_w_out'][0].astype(BF16),
                       p['norm2_g'][1].reshape(1, D_MODEL), wr_cat, wr_hi, tiles['tm_sgu'])
    x = _moe(x1, h2, aff, p['moe_w_gate'], p['moe_w_up'], p['moe_w_down'], 1, tiles['ffn_rows'])
    return x.reshape(B, S, D_MODEL)


TILES = dict(tk=256, tq=1024, tm=512, tm_out=1024, tm_sgu=1024, ffn_rows=1024)


def kernel(x_prompt, x_sample, norm1_g, norm2_g, mla_w_in, mla_q_lora_g, mla_kv_lora_g, mla_w_uq, mla_w_ukv, mla_q_norm_g, mla_k_norm_g, mla_w_out, sgu_w_in, sgu_ln_g, sgu_ln_b, sgu_w_s, sgu_b_s, sgu_w_out, moe_w_router, moe_w_gate, moe_w_up, moe_w_down):
    p = dict(norm1_g=norm1_g, norm2_g=norm2_g, mla_w_in=mla_w_in, mla_q_lora_g=mla_q_lora_g,
             mla_kv_lora_g=mla_kv_lora_g, mla_w_uq=mla_w_uq, mla_w_ukv=mla_w_ukv, mla_q_norm_g=mla_q_norm_g,
             mla_k_norm_g=mla_k_norm_g, mla_w_out=mla_w_out, sgu_w_in=sgu_w_in, sgu_ln_g=sgu_ln_g,
             sgu_ln_b=sgu_ln_b, sgu_w_s=sgu_w_s, sgu_b_s=sgu_b_s, sgu_w_out=sgu_w_out, moe_w_router=moe_w_router,
             moe_w_gate=moe_w_gate, moe_w_up=moe_w_up, moe_w_down=moe_w_down)
    return (_trunk(x_prompt, p, TILES), _trunk(x_sample, p, TILES))
```

```python
import functools

import jax
import jax.numpy as jnp
import numpy as np
from jax import lax
from jax.experimental import pallas as pl
from jax.experimental.pallas import tpu as pltpu

F32 = jnp.float32
BF16 = jnp.bfloat16
I32 = jnp.int32

D_MODEL = 1024
MLA_HEADS = 16
Q_LORA = 256
KV_LORA = 128
NOPE_DIM = 64
ROPE_DIM = 32
HALF_ROPE = ROPE_DIM // 2
QK_HEAD = NOPE_DIM + ROPE_DIM
V_HEAD = 64
ROPE_THETA = 10000.0
CHUNK = 128
SGU_HALF = D_MODEL
SGU_GROUPS = 8
N_EXPERTS = 16
EXPERT_FF = 1024
CAPACITY_FACTOR = 2
EPS = 1e-6

LANES = 128
QK_PAD = 128
V_ROWS = 80
COMB_TILE = 128
DISP_TILE = 256
COMB_PER_STEP = 2
COMB_SLOTS = 3
ROW_ALIGN = 16
ROW_ALIGN_SHIFT = 4
COMB_LONG, COMB_SHORT = COMB_TILE + ROW_ALIGN, 64
DISP_LONG, DISP_SHORT = DISP_TILE + ROW_ALIGN, 80
NOT_SELECTED = -1024
BOUND_MARGIN = 1.0 + 2.0 ** -6
MAX_SHIFTED_SCORE_RANGE = 100.0
HEADS_PER_STEP = 2
KT_PER_TRIP = 32
QCHUNK = 256

VMEM_LIMIT = 52 * 1024 * 1024


def _cparams(sem):
    return pltpu.CompilerParams(dimension_semantics=sem, vmem_limit_bytes=VMEM_LIMIT)


def _dot(a, b):
    return jnp.dot(a, b, preferred_element_type=F32)


def _dot_nt(a, b):
    return lax.dot_general(a, b, (((1,), (1,)), ((), ())), preferred_element_type=F32)


def _rms_rows(x, g):
    return x * lax.rsqrt(jnp.mean(x * x, axis=-1, keepdims=True) + EPS) * g


def _mla_prep_kernel(x_ref, g1_ref, w_in_ref, gq_ref, gkv_ref, w_uqT_ref, w_uk_ref, w_uvT_ref,
                     gqc_ref, kb_ref, kcos_ref, ksin_ref, cosT_ref, sinT_ref,
                     qT_ref, k_ref, vT_ref):
    x = x_ref[...]
    h = _rms_rows(x, g1_ref[...]).astype(BF16)
    proj = _dot(h, w_in_ref[...])
    cq = _rms_rows(proj[:, :Q_LORA], gq_ref[...]).astype(BF16)
    ckv = _rms_rows(proj[:, Q_LORA:Q_LORA + KV_LORA], gkv_ref[...]).astype(BF16)
    kpe = proj[:, 384:512]
    kpe_sw = proj[:, 512:640]
    tm = x.shape[0]
    n_kt, tk = k_ref.shape[2], k_ref.shape[3]

    qT = _dot_nt(w_uqT_ref[...], cq)
    cosT = cosT_ref[...]
    sinT = sinT_ref[...]
    gqc = gqc_ref[...]
    kbound = kb_ref[...]
    brow = lax.broadcasted_iota(I32, (QK_PAD - QK_HEAD, tm), 0)
    for hd in range(MLA_HEADS):
        qh = qT[hd * QK_HEAD:(hd + 1) * QK_HEAD, :]
        r = lax.rsqrt(jnp.sum(qh * qh, axis=0, keepdims=True) * (1.0 / QK_HEAD) + EPS)
        qn = qh * gqc * r
        x1 = qn[NOPE_DIM:NOPE_DIM + HALF_ROPE]
        x2 = qn[NOPE_DIM + HALF_ROPE:QK_HEAD]
        qb = jnp.concatenate([qn[:NOPE_DIM], x1 * cosT - x2 * sinT, x2 * cosT + x1 * sinT], axis=0).astype(BF16)
        qT_ref[0, hd, 0:QK_HEAD, :] = qb
        qf = qb.astype(F32)
        bound = jnp.sqrt(jnp.sum(qf * qf, axis=0, keepdims=True)) * kbound
        qT_ref[0, hd, QK_HEAD:QK_PAD, :] = jnp.where(brow == 0, -bound, 0.0).astype(BF16)

    knope = _dot(ckv, w_uk_ref[...])
    kcos = kcos_ref[...]
    ksin = ksin_ref[...]
    one_lane = lax.broadcasted_iota(I32, (tm, QK_PAD), 1) == QK_HEAD
    for hd in range(MLA_HEADS):
        kh = knope[:, hd * QK_PAD:(hd + 1) * QK_PAD] + kpe
        r = lax.rsqrt(jnp.sum(kh * kh, axis=-1, keepdims=True) * (1.0 / QK_HEAD) + EPS)
        kb = jnp.where(one_lane, 1.0, (kh * kcos + kpe_sw * ksin) * r).astype(BF16)
        for j in range(n_kt):
            k_ref[0, hd, j] = kb[j * tk:(j + 1) * tk]

    vT = _dot_nt(w_uvT_ref[...], ckv)
    row = lax.broadcasted_iota(I32, (V_ROWS - V_HEAD, tk), 0)
    tail = jnp.where(row == 0, 1.0, 0.0).astype(BF16)
    for hd in range(MLA_HEADS):
        vb = vT[hd * V_HEAD:(hd + 1) * V_HEAD, :].astype(BF16)
        for j in range(n_kt):
            vT_ref[0, hd, j, 0:V_HEAD, :] = vb[:, j * tk:(j + 1) * tk]
            vT_ref[0, hd, j, V_HEAD:V_ROWS, :] = tail


def _mla_prep(x3, g1, wts, tabs, tm, tk):
    B, S, _ = x3.shape
    nb, per = S // tm, tm // tk
    full = lambda a: pl.BlockSpec(a.shape, lambda b, i: (0,) * a.ndim)
    in_specs = [pl.BlockSpec((None, tm, D_MODEL), lambda b, i: (b, i, 0)), full(g1)]
    in_specs += [full(w) for w in wts]
    in_specs += [pl.BlockSpec((tm, LANES), lambda b, i: (i, 0)),
                 pl.BlockSpec((tm, LANES), lambda b, i: (i, 0)),
                 pl.BlockSpec((HALF_ROPE, tm), lambda b, i: (0, i)),
                 pl.BlockSpec((HALF_ROPE, tm), lambda b, i: (0, i))]
    out_shape = [jax.ShapeDtypeStruct((B, MLA_HEADS, QK_PAD, S), BF16),
                 jax.ShapeDtypeStruct((B, MLA_HEADS, S // tk, tk, QK_PAD), BF16),
                 jax.ShapeDtypeStruct((B, MLA_HEADS, S // tk, V_ROWS, tk), BF16)]
    out_specs = [pl.BlockSpec((1, MLA_HEADS, QK_PAD, tm), lambda b, i: (b, 0, 0, i)),
                 pl.BlockSpec((1, MLA_HEADS, per, tk, QK_PAD), lambda b, i: (b, 0, i, 0, 0)),
                 pl.BlockSpec((1, MLA_HEADS, per, V_ROWS, tk), lambda b, i: (b, 0, i, 0, 0))]
    return pl.pallas_call(
        _mla_prep_kernel, grid=(B, nb), in_specs=in_specs, out_specs=out_specs, out_shape=out_shape,
        compiler_params=_cparams(("arbitrary", "arbitrary")), name="mla_prep",
    )(x3, g1, *wts, *tabs)


def _attn_kernel(qT_ref, k_ref, vT_ref, o_ref, *, n_kt):
    tq = qT_ref.shape[3]
    qTs = [qT_ref[0, hh] for hh in range(HEADS_PER_STEP)]

    def body(kt, carry):
        out = []
        for hh in range(HEADS_PER_STEP):
            m, acc = carry[hh]
            sT = _dot(k_ref[0, hh, kt], qTs[hh])
            m_new = jnp.maximum(m, jnp.max(sT, axis=0, keepdims=True))
            alpha = jnp.exp2(m - m_new)
            pT = jnp.exp2(sT - m_new).astype(BF16)
            acc = alpha * acc + _dot(vT_ref[0, hh, kt], pT)
            out.append((m_new, acc))
        return tuple(out)

    init = tuple((jnp.full((1, tq), -jnp.inf, F32), jnp.zeros((V_ROWS, tq), F32))
                 for _ in range(HEADS_PER_STEP))
    res = lax.fori_loop(0, n_kt, body, init, unroll=2)
    outs = [acc[:V_HEAD] / acc[V_HEAD:V_HEAD + 1] for _, acc in res]
    o_ref[0] = jnp.concatenate(outs, axis=0).T.astype(o_ref.dtype)


def _attn_bounded_kernel(qT_ref, k_ref, vT_ref, o_ref, *, n_kt):
    tq = qT_ref.shape[3]
    n_qc = tq // QCHUNK
    chains = [(hh, qc) for hh in range(HEADS_PER_STEP) for qc in range(n_qc)]
    qTs = [qT_ref[0, hh, :, qc * QCHUNK:(qc + 1) * QCHUNK] for hh, qc in chains]

    def scores(kt):
        return [_dot(k_ref[0, hh, kt], qTs[i]) for i, (hh, qc) in enumerate(chains)]

    def accumulate(accs, kt, sTs):
        return tuple(accs[i] + _dot(vT_ref[0, hh, kt], jnp.exp2(sTs[i]).astype(BF16))
                     for i, (hh, qc) in enumerate(chains))

    def body(j, accs):
        kt = j * per_trip
        sTs = scores(kt)
        accs = list(accs)
        for u in range(1, per_trip):
            for i, (hh, qc) in enumerate(chains):
                nxt_i = _dot(k_ref[0, hh, kt + u], qTs[i])
                accs[i] = accs[i] + _dot(vT_ref[0, hh, kt + u - 1], jnp.exp2(sTs[i]).astype(BF16))
                sTs[i] = nxt_i
        return accumulate(tuple(accs), kt + per_trip - 1, sTs)

    per_trip = min(KT_PER_TRIP, n_kt)
    assert n_kt % per_trip == 0
    accs = lax.fori_loop(0, n_kt // per_trip, body,
                         tuple(jnp.zeros((V_ROWS, QCHUNK), F32) for _ in chains))
    outs = [acc[:V_HEAD] / acc[V_HEAD:V_HEAD + 1] for acc in accs]
    heads = [jnp.concatenate(outs[hh * n_qc:(hh + 1) * n_qc], axis=1) for hh in range(HEADS_PER_STEP)]
    o_ref[0] = jnp.concatenate(heads, axis=0).T.astype(o_ref.dtype)


def _attention(qT, k, vT, tq, bounded):
    B, H, _, S = qT.shape
    n_kt, tk = k.shape[2], k.shape[3]
    hp = H // HEADS_PER_STEP
    return pl.pallas_call(
        functools.partial(_attn_bounded_kernel if bounded else _attn_kernel, n_kt=n_kt),
        grid=(B, hp, S // tq),
        in_specs=[pl.BlockSpec((1, HEADS_PER_STEP, QK_PAD, tq), lambda b, h, i: (b, h, 0, i)),
                  pl.BlockSpec((1, HEADS_PER_STEP, n_kt, tk, QK_PAD), lambda b, h, i: (b, h, 0, 0, 0)),
                  pl.BlockSpec((1, HEADS_PER_STEP, n_kt, V_ROWS, tk), lambda b, h, i: (b, h, 0, 0, 0))],
        out_specs=pl.BlockSpec((1, tq, HEADS_PER_STEP * V_HEAD), lambda b, h, i: (b, i, h)),
        out_shape=jax.ShapeDtypeStruct((B, S, H * V_HEAD), BF16),
        compiler_params=_cparams(("arbitrary", "arbitrary", "arbitrary")),
        name="mla_attention_bounded" if bounded else "mla_attention",
    )(qT, k, vT)


def _router_epilogue(x1, g2_ref, wr_cat_ref, wr_hi_ref, x1_ref, h2_ref, aff_ref):
    x1_ref[...] = x1
    h2 = _rms_rows(x1, g2_ref[...])
    h2_hi = h2.astype(BF16)
    h2_lo = (h2 - h2_hi.astype(F32)).astype(BF16)
    h2_ref[...] = h2_hi
    both = _dot(h2_hi, wr_cat_ref[...])
    logits = both[:, :N_EXPERTS] + both[:, N_EXPERTS:] + _dot(h2_lo, wr_hi_ref[...])
    ex = jnp.exp(logits - jnp.max(logits, axis=-1, keepdims=True))
    aff_ref[...] = ex / jnp.sum(ex, axis=-1, keepdims=True)


def _outproj_kernel(o_ref, w_ref, x_ref, g2_ref, wr_cat_ref, wr_hi_ref, x1_ref, h2_ref, aff_ref):
    x1 = x_ref[...] + _dot(o_ref[...], w_ref[...])
    _router_epilogue(x1, g2_ref, wr_cat_ref, wr_hi_ref, x1_ref, h2_ref, aff_ref)


def _row_specs(n, tm):
    out_shape = [jax.ShapeDtypeStruct((n, D_MODEL), F32),
                 jax.ShapeDtypeStruct((n, D_MODEL), BF16),
                 jax.ShapeDtypeStruct((n, N_EXPERTS), F32)]
    out_specs = [pl.BlockSpec((tm, D_MODEL), lambda i: (i, 0)),
                 pl.BlockSpec((tm, D_MODEL), lambda i: (i, 0)),
                 pl.BlockSpec((tm, N_EXPERTS), lambda i: (i, 0))]
    return out_shape, out_specs


def _outproj(o, w_out, x, g2, wr_cat, wr_hi, tm):
    n = x.shape[0]
    full = lambda a: pl.BlockSpec(a.shape, lambda i: (0,) * a.ndim)
    out_shape, out_specs = _row_specs(n, tm)
    return pl.pallas_call(
        _outproj_kernel, grid=(n // tm,),
        in_specs=[pl.BlockSpec((tm, D_MODEL), lambda i: (i, 0)), full(w_out),
                  pl.BlockSpec((tm, D_MODEL), lambda i: (i, 0)), full(g2), full(wr_cat), full(wr_hi)],
        out_specs=out_specs, out_shape=out_shape,
        compiler_params=_cparams(("arbitrary",)), name="mla_outproj_router",
    )(o, w_out, x, g2, wr_cat, wr_hi)


def _sgu_kernel(x_ref, g1_ref, w_in_ref, lng_ref, lnb_ref, ws_ref, bs_ref, w_out_ref,
                g2_ref, wr_cat_ref, wr_hi_ref, x1_ref, h2_ref, aff_ref):
    x = x_ref[...]
    tm = x.shape[0]
    h = _rms_rows(x, g1_ref[...]).astype(BF16)
    z = _dot(h, w_in_ref[...])
    z = 0.5 * z * (1.0 + lax.erf(z * np.float32(2.0 ** -0.5)))
    u = z[:, :SGU_HALF]
    v = z[:, SGU_HALF:]
    mu = jnp.mean(v, axis=-1, keepdims=True)
    vc = v - mu
    v = vc * lax.rsqrt(jnp.mean(vc * vc, axis=-1, keepdims=True) + EPS) * lng_ref[...] + lnb_ref[...]
    vb = v.astype(BF16)
    bs = bs_ref[...]
    n_chunks = tm // CHUNK
    mixed = []
    for g in range(SGU_GROUPS):
        blk = jnp.concatenate([vb[c * CHUNK:(c + 1) * CHUNK, g * LANES:(g + 1) * LANES]
                               for c in range(n_chunks)], axis=1)
        mixed.append(_dot(ws_ref[g], blk))
    sv = jnp.concatenate(
        [jnp.concatenate([mixed[g][:, c * LANES:(c + 1) * LANES] for g in range(SGU_GROUPS)], axis=1) + bs
         for c in range(n_chunks)], axis=0)
    gated = (u * sv).astype(BF16)
    x1 = x + _dot(gated, w_out_ref[...])
    _router_epilogue(x1, g2_ref, wr_cat_ref, wr_hi_ref, x1_ref, h2_ref, aff_ref)


def _sgu(x, g1, w_in, lng, lnb, ws, bs_full, w_out, g2, wr_cat, wr_hi, tm):
    n = x.shape[0]
    full = lambda a: pl.BlockSpec(a.shape, lambda i: (0,) * a.ndim)
    ops = (g1, w_in, lng, lnb, ws, bs_full, w_out, g2, wr_cat, wr_hi)
    out_shape, out_specs = _row_specs(n, tm)
    return pl.pallas_call(
        _sgu_kernel, grid=(n // tm,),
        in_specs=[pl.BlockSpec((tm, D_MODEL), lambda i: (i, 0))] + [full(a) for a in ops],
        out_specs=out_specs, out_shape=out_shape,
        compiler_params=_cparams(("arbitrary",)), name="sgu_router",
    )(x, *ops)


def _select_kernel(aff_ref, lp_ref, start_ref, cnt_ref, lp_pair_ref, *, nt, cap):
    a = aff_ref[...]
    bits = pltpu.bitcast(a, I32)

    def body(i, cur):
        trial = cur | jnp.left_shift(jnp.int32(1), 30 - i)
        ge = jnp.where(bits >= trial, 1.0, 0.0)
        cnt = jnp.sum(jnp.sum(ge, axis=1, keepdims=True), axis=2, keepdims=True)
        return jnp.where(cnt >= cap, trial, cur)

    thr = lax.fori_loop(0, 31, body, jnp.zeros((N_EXPERTS, 1, 1), I32))
    gtf = jnp.where(bits > thr, 1.0, 0.0)
    eqf = jnp.where(bits == thr, 1.0, 0.0)
    c_gt = jnp.sum(jnp.sum(gtf, axis=1, keepdims=True), axis=2, keepdims=True)
    need = cap - c_gt

    r = lax.broadcasted_iota(I32, (COMB_TILE, COMB_TILE), 0)
    c = lax.broadcasted_iota(I32, (COMB_TILE, COMB_TILE), 1)
    incl_u = jnp.where(r <= c, 1.0, 0.0).astype(BF16)
    ones_u = jnp.ones((COMB_TILE, LANES), BF16)
    rt = lax.broadcasted_iota(I32, (nt, nt), 0)
    ct = lax.broadcasted_iota(I32, (nt, nt), 1)
    excl_l = jnp.where(rt > ct, 1.0, 0.0).astype(BF16)

    def tile_prefix(maskf):
        m2 = maskf.reshape(N_EXPERTS * nt, COMB_TILE).astype(BF16)
        incl = _dot(m2, incl_u)
        cnt = _dot(m2, ones_u)
        cnt_b = cnt.astype(BF16)
        starts = [_dot(excl_l, cnt_b[e * nt:(e + 1) * nt]) for e in range(N_EXPERTS)]
        return incl, jnp.concatenate(starts, axis=0), cnt

    eq_incl, eq_start, _ = tile_prefix(eqf)
    tie_rank = (eq_start + eq_incl).reshape(N_EXPERTS, nt, COMB_TILE) - eqf
    selm = gtf + eqf * jnp.where(tie_rank < need, 1.0, 0.0)
    incl, start, cnt = tile_prefix(selm)
    sel2 = selm.reshape(N_EXPERTS * nt, COMB_TILE) > 0.5
    lp_ref[...] = jnp.where(sel2, incl - 1.0, float(NOT_SELECTED)).astype(I32)
    start_ref[...] = start.astype(I32)
    cnt_ref[...] = cnt.astype(I32)
    odd = (lax.broadcasted_iota(I32, (N_EXPERTS * nt, COMB_TILE), 0) & 1) == 1
    before = jnp.where(odd, jnp.concatenate([jnp.zeros((1, COMB_TILE), F32), cnt[:-1]], axis=0), 0.0)
    lp_pair_ref[...] = jnp.where(sel2, incl - 1.0 + before, float(NOT_SELECTED)).astype(I32)


def _select(aff_t3, cap):
    e, nt, _ = aff_t3.shape
    return pl.pallas_call(
        functools.partial(_select_kernel, nt=nt, cap=cap),
        out_shape=[jax.ShapeDtypeStruct((e * nt, COMB_TILE), I32)] * 4,
        compiler_params=pltpu.CompilerParams(vmem_limit_bytes=VMEM_LIMIT), name="moe_select",
    )(aff_t3)


def _round_down(v):
    return (v >> ROW_ALIGN_SHIFT) << ROW_ALIGN_SHIFT


def _by_slab_length(short, short_rows, long_rows, fn):
    @pl.when(short)
    def _():
        fn(short_rows)

    @pl.when(jnp.logical_not(short))
    def _():
        fn(long_rows)


def _dispatch_kernel(start_sm, cnt_sm, short_sm, x_ref, lp_ref, xg_hbm, stage, carry, sems, *, nt, cap):
    step = pl.program_id(0)

    def copy(s, e, row0, rows):
        dst = xg_hbm.at[e, pl.ds(pl.multiple_of(row0, ROW_ALIGN), rows), :]
        return pltpu.make_async_copy(stage.at[s, e, pl.ds(0, rows)], dst, sems.at[s, e])

    @pl.when(step == 0)
    def _():
        carry[...] = jnp.zeros(carry.shape, carry.dtype)
        stage[1] = jnp.zeros(stage.shape[1:], stage.dtype)
        for e in range(N_EXPERTS):
            copy(1, e, cap, DISP_LONG).start()
        for e in range(N_EXPERTS):
            copy(1, e, cap, DISP_LONG).wait()

    rio = lax.broadcasted_iota(I32, (ROW_ALIGN, D_MODEL), 0)
    by_length = functools.partial(_by_slab_length, short_rows=DISP_SHORT, long_rows=DISP_LONG)

    for slot in range(2):
        t = step * 2 + slot
        x = x_ref[pl.ds(slot * DISP_TILE, DISP_TILE), :]
        shifts = [start_sm[e * nt + t] - _round_down(start_sm[e * nt + t]) for e in range(N_EXPERTS)]

        def onehot_t(e, rows, slot=slot, shifts=shifts):
            jio = lax.broadcasted_iota(I32, (rows, DISP_TILE), 0)
            return jnp.where(lp_ref[slot, e:e + 1, :] + shifts[e] == jio, 1.0, 0.0).astype(BF16)

        def place(e, slab, rows, slot=slot, shifts=shifts, t=t):
            head = jnp.where(rio < shifts[e], carry[e].astype(F32), slab[0:ROW_ALIGN])
            stage[slot, e, 0:ROW_ALIGN] = head.astype(BF16)
            stage[slot, e, ROW_ALIGN:rows] = slab[ROW_ALIGN:].astype(BF16)
            nxt = pl.multiple_of(_round_down(shifts[e] + cnt_sm[e * nt + t]), ROW_ALIGN)
            carry[e] = stage[slot, e, pl.ds(nxt, ROW_ALIGN)]

        def fill(rows, x=x, onehot_t=onehot_t, place=place):
            if rows == DISP_SHORT:
                slabs = _dot(jnp.concatenate([onehot_t(e, rows) for e in range(N_EXPERTS)], axis=0), x)
                for e in range(N_EXPERTS):
                    place(e, slabs[e * rows:(e + 1) * rows], rows)
            else:
                for e in range(N_EXPERTS):
                    place(e, _dot(onehot_t(e, rows), x), rows)

        short_now = short_sm[t] == 1
        by_length(short_now, fn=fill)

        def hand_over(rows, t=t, slot=slot):
            def per_expert(prev_rows):
                for e in range(N_EXPERTS):
                    if prev_rows is not None:
                        copy(1 - slot, e, 0, prev_rows).wait()
                    copy(slot, e, _round_down(start_sm[e * nt + t]), rows).start()

            if slot == 0:
                pl.when(step == 0)(lambda: per_expert(None))
                pl.when(step > 0)(lambda: by_length(short_sm[t - 1] == 1, fn=per_expert))
            else:
                by_length(short_sm[t - 1] == 1, fn=per_expert)

        by_length(short_now, fn=hand_over)

    @pl.when(step == nt // 2 - 1)
    def _():
        by_length(short_sm[nt - 1] == 1, fn=lambda rows: [copy(1, e, 0, rows).wait() for e in range(N_EXPERTS)])


def _dispatch(start_flat, cnt_flat, short_flat, h2, lp3, cap):
    n = h2.shape[0]
    nt = n // DISP_TILE
    assert nt % 2 == 0
    grid_spec = pltpu.PrefetchScalarGridSpec(
        num_scalar_prefetch=3, grid=(nt // 2,),
        in_specs=[pl.BlockSpec((2 * DISP_TILE, D_MODEL), lambda t, *_: (t, 0)),
                  pl.BlockSpec((2, N_EXPERTS, DISP_TILE), lambda t, *_: (t, 0, 0))],
        out_specs=pl.BlockSpec(memory_space=pl.ANY),
        scratch_shapes=[pltpu.VMEM((2, N_EXPERTS, DISP_LONG, D_MODEL), BF16),
                        pltpu.VMEM((N_EXPERTS, ROW_ALIGN, D_MODEL), BF16),
                        pltpu.SemaphoreType.DMA((2, N_EXPERTS))])
    return pl.pallas_call(
        functools.partial(_dispatch_kernel, nt=nt, cap=cap), grid_spec=grid_spec,
        out_shape=jax.ShapeDtypeStruct((N_EXPERTS, cap + DISP_LONG, D_MODEL), BF16),
        compiler_params=_cparams(("arbitrary",)), name="moe_dispatch",
    )(start_flat, cnt_flat, short_flat, h2, lp3)


def _ffn_kernel(xg_ref, wg_ref, wu_ref, wd_ref, yg_ref, wg_b, wu_b, wd_b):
    @pl.when(pl.program_id(1) == 0)
    def _():
        wg_b[...] = wg_ref[0, 0].astype(BF16)
        wu_b[...] = wu_ref[0, 0].astype(BF16)
        wd_b[...] = wd_ref[0, 0].astype(BF16)

    x = xg_ref[0]
    g = _dot(x, wg_b[...])
    u = _dot(x, wu_b[...])
    hid = (g * jax.nn.sigmoid(g) * u).astype(BF16)
    y = _dot(hid, wd_b[...])
    yg_ref[0] = y.astype(BF16)


def _ffn(xg, wg, wu, wd, layer, cap, rt):
    wspec = pl.BlockSpec((1, 1, D_MODEL, EXPERT_FF), lambda e, r: (layer, e, 0, 0))
    return pl.pallas_call(
        _ffn_kernel, grid=(N_EXPERTS, cap // rt),
        in_specs=[pl.BlockSpec((1, rt, D_MODEL), lambda e, r: (e, r, 0)), wspec, wspec,
                  pl.BlockSpec((1, 1, EXPERT_FF, D_MODEL), lambda e, r: (layer, e, 0, 0))],
        out_specs=pl.BlockSpec((1, rt, D_MODEL), lambda e, r: (e, r, 0)),
        out_shape=jax.ShapeDtypeStruct((N_EXPERTS, cap, D_MODEL), BF16),
        scratch_shapes=[pltpu.VMEM((D_MODEL, EXPERT_FF), BF16), pltpu.VMEM((D_MODEL, EXPERT_FF), BF16),
                        pltpu.VMEM((EXPERT_FF, D_MODEL), BF16)],
        compiler_params=_cparams(("arbitrary", "arbitrary")), name="moe_ffn",
    )(xg, wg, wu, wd)


def _combine_kernel(start_sm, short_sm, x_ref, lpt_ref, aff_ref, yg_hbm, o_ref, slabs, sems, *, nt, cap):
    t = pl.program_id(0)
    slot = t % COMB_SLOTS
    n_steps = nt // COMB_PER_STEP

    def row0(e, tt, rows):
        return jnp.minimum(_round_down(start_sm[e * nt + tt]), cap - rows)

    def copy(s, j, e, r0, rows):
        src = yg_hbm.at[e, pl.ds(pl.multiple_of(r0, ROW_ALIGN), rows), :]
        return pltpu.make_async_copy(src, slabs.at[s, j, e, pl.ds(0, rows)], sems.at[s, j])

    by_length = functools.partial(_by_slab_length, short_rows=COMB_SHORT, long_rows=COMB_LONG)

    def fetch(s, step):
        for j in range(COMB_PER_STEP):
            tt = step * COMB_PER_STEP + j
            by_length(short_sm[tt] == 1,
                      fn=lambda rows, j=j, tt=tt: [copy(s, j, e, row0(e, tt, rows), rows).start()
                                                   for e in range(N_EXPERTS)])

    ahead = COMB_SLOTS - 1

    @pl.when(t == 0)
    def _():
        for s in range(min(ahead, n_steps)):
            fetch(s, s)

    @pl.when(t + ahead < n_steps)
    def _():
        fetch((t + ahead) % COMB_SLOTS, t + ahead)

    def expand_short(j, tt, rows):
        tok = pl.ds(j * COMB_TILE, COMB_TILE)
        for e in range(N_EXPERTS):
            copy(slot, j, e, 0, rows).wait()
        stacked = jnp.concatenate([slabs[slot, j, e, 0:rows] for e in range(N_EXPERTS)], axis=0)
        width = N_EXPERTS * rows
        first = lax.broadcasted_iota(I32, (N_EXPERTS, width), 0) * rows
        lane2 = lax.broadcasted_iota(I32, (N_EXPERTS, width), 1)
        spread = jnp.where((lane2 >= first) & (lane2 < first + rows), 1.0, 0.0).astype(BF16)
        g = aff_ref[tok, :]
        g_hi = g.astype(BF16)
        g_lo = (g - g_hi.astype(F32)).astype(BF16)
        cols = jnp.concatenate([lpt_ref[tok, :].astype(F32).astype(BF16), g_hi, g_lo], axis=0)
        wide = _dot(cols, spread)
        lane = lax.broadcasted_iota(I32, (1, width), 1)
        target = lane
        for e in range(N_EXPERTS):
            delta = start_sm[e * nt + tt] - row0(e, tt, rows)
            target = jnp.where((lane >= e * rows) & (lane < (e + 1) * rows), lane - (e * rows + delta), target)
        hit = wide[:COMB_TILE] == target.astype(F32)
        w = jnp.concatenate([jnp.where(hit, wide[COMB_TILE:2 * COMB_TILE], 0.0).astype(BF16),
                             jnp.where(hit, wide[2 * COMB_TILE:], 0.0).astype(BF16)], axis=0)
        both = _dot(w, stacked)
        o_ref[tok, :] = x_ref[tok, :] + both[:COMB_TILE] + both[COMB_TILE:]

    def expand_long(j, tt, rows):
        tok = pl.ds(j * COMB_TILE, COMB_TILE)
        for e in range(N_EXPERTS):
            copy(slot, j, e, 0, rows).wait()
        acc = x_ref[tok, :]
        lane = lax.broadcasted_iota(I32, (COMB_TILE, rows), 1)
        for e in range(N_EXPERTS):
            delta = start_sm[e * nt + tt] - row0(e, tt, rows)
            onehot = jnp.where(lpt_ref[tok, e:e + 1] + delta == lane, 1.0, 0.0).astype(BF16)
            acc = acc + aff_ref[tok, e:e + 1] * _dot(onehot, slabs[slot, j, e])
        o_ref[tok, :] = acc

    for j in range(COMB_PER_STEP):
        tt = t * COMB_PER_STEP + j
        by_length(short_sm[tt] == 1,
                  fn=lambda rows, j=j, tt=tt: expand_short(j, tt, rows) if rows == COMB_SHORT
                  else expand_long(j, tt, rows))


def _combine(start_flat, short_flat, x1, lp_t, aff, yg, cap):
    n = x1.shape[0]
    nt = n // COMB_TILE
    rows = COMB_TILE * COMB_PER_STEP
    grid_spec = pltpu.PrefetchScalarGridSpec(
        num_scalar_prefetch=2, grid=(nt // COMB_PER_STEP,),
        in_specs=[pl.BlockSpec((rows, D_MODEL), lambda t, s, c: (t, 0)),
                  pl.BlockSpec((rows, N_EXPERTS), lambda t, s, c: (t, 0)),
                  pl.BlockSpec((rows, N_EXPERTS), lambda t, s, c: (t, 0)),
                  pl.BlockSpec(memory_space=pl.ANY)],
        out_specs=pl.BlockSpec((rows, D_MODEL), lambda t, s, c: (t, 0)),
        scratch_shapes=[pltpu.VMEM((COMB_SLOTS, COMB_PER_STEP, N_EXPERTS, COMB_LONG, D_MODEL), BF16),
                        pltpu.SemaphoreType.DMA((COMB_SLOTS, COMB_PER_STEP))])
    return pl.pallas_call(
        functools.partial(_combine_kernel, nt=nt, cap=cap), grid_spec=grid_spec,
        out_shape=jax.ShapeDtypeStruct((n, D_MODEL), F32),
        compiler_params=_cparams(("arbitrary",)), name="moe_combine",
    )(start_flat, short_flat, x1, lp_t, aff, yg)


def _moe(x1, h2, aff, wg, wu, wd, layer, ffn_rows):
    n = x1.shape[0]
    nt, ntd = n // COMB_TILE, n // DISP_TILE
    cap = max(1, CAPACITY_FACTOR * n // N_EXPERTS)
    assert DISP_TILE == 2 * COMB_TILE and cap >= DISP_LONG and cap % ROW_ALIGN == 0
    lp, start, cnt, lp_pair = _select(aff.T.reshape(N_EXPERTS, nt, COMB_TILE), cap)
    start_c, cnt_c = start[:, 0].reshape(N_EXPERTS, nt), cnt[:, 0].reshape(N_EXPERTS, nt)
    start_d, cnt_d = start_c[:, ::2], cnt_c[:, ::2] + cnt_c[:, 1::2]

    def short_flags(st, ct, short_rows):
        return (jnp.max((st & (ROW_ALIGN - 1)) + ct, axis=0) <= short_rows - ROW_ALIGN).astype(I32)

    lp_t = lp.reshape(N_EXPERTS, n).T
    lp_tiles = lp_pair.reshape(N_EXPERTS, ntd, DISP_TILE).transpose(1, 0, 2)
    xg = _dispatch(start_d.reshape(-1), cnt_d.reshape(-1), short_flags(start_d, cnt_d, DISP_SHORT), h2, lp_tiles, cap)
    yg = _ffn(xg, wg, wu, wd, layer, cap, min(ffn_rows, cap))
    return _combine(start_c.reshape(-1), short_flags(start_c, cnt_c, COMB_SHORT), x1, lp_t, aff, yg, cap)


def _rope_tables(S):
    pos = np.arange(S, dtype=np.float64)
    inv_freq = 1.0 / (ROPE_THETA ** (np.arange(0, ROPE_DIM, 2, dtype=np.float64) / ROPE_DIM))
    ang = pos[:, None] * inv_freq[None, :]
    return jnp.asarray(np.cos(ang), F32), jnp.asarray(np.sin(ang), F32)


def _prep_mla_weights(w_in, gq, gkv, w_uq, w_ukv, qn_g, kn_g):
    pad = lambda a, lo, hi: jnp.pad(a, ((0, 0), (lo, hi)))
    pe = w_in[:, Q_LORA + KV_LORA:]
    pe_sw = jnp.concatenate([pe[:, HALF_ROPE:], pe[:, :HALF_ROPE]], axis=1)
    w_in_p = jnp.concatenate([w_in[:, :Q_LORA + KV_LORA], pad(pe, NOPE_DIM, QK_PAD - QK_HEAD),
                              pad(pe_sw, NOPE_DIM, QK_PAD - QK_HEAD)], axis=1).astype(BF16)
    w_uqT = w_uq.T.astype(BF16)
    ukv = w_ukv.reshape(KV_LORA, MLA_HEADS, NOPE_DIM + V_HEAD)
    w_uk = jnp.pad(ukv[:, :, :NOPE_DIM], ((0, 0), (0, 0), (0, QK_PAD - NOPE_DIM)))
    w_uk = w_uk.reshape(KV_LORA, MLA_HEADS * QK_PAD).astype(BF16)
    w_uvT = ukv[:, :, NOPE_DIM:].reshape(KV_LORA, MLA_HEADS * V_HEAD).T.astype(BF16)
    gqc = (qn_g * np.float32(QK_HEAD ** -0.5 * np.log2(np.e))).reshape(QK_HEAD, 1)
    kb = (jnp.max(jnp.abs(kn_g)) * np.float32(QK_HEAD ** 0.5 * BOUND_MARGIN)).reshape(1, 1)
    return (w_in_p, gq.reshape(1, Q_LORA), gkv.reshape(1, KV_LORA), w_uqT, w_uk, w_uvT, gqc, kb)


def _score_bound_log2(qn_g, kn_g):
    return (jnp.max(jnp.abs(qn_g)) * jnp.max(jnp.abs(kn_g))
            * np.float32(QK_HEAD * QK_HEAD ** -0.5 * np.log2(np.e) * BOUND_MARGIN * BOUND_MARGIN))


def _prep_k_tables(cos, sin, kn_g):
    S = cos.shape[0]
    z = lambda w: jnp.zeros((S, w), F32)
    cos_l = jnp.concatenate([jnp.ones((S, NOPE_DIM), F32), cos, cos, z(QK_PAD - QK_HEAD)], axis=1)
    sin_l = jnp.concatenate([z(NOPE_DIM), -sin, sin, z(QK_PAD - QK_HEAD)], axis=1)
    g = jnp.pad(kn_g, (0, QK_PAD - QK_HEAD))
    g_sw = jnp.concatenate([jnp.zeros((NOPE_DIM,), F32), kn_g[NOPE_DIM + HALF_ROPE:],
                            kn_g[NOPE_DIM:NOPE_DIM + HALF_ROPE], jnp.zeros((QK_PAD - QK_HEAD,), F32)])
    return cos_l * g[None, :], sin_l * g_sw[None, :]


def _router_weights(w_r):
    hi = w_r.astype(BF16)
    lo = (w_r - hi.astype(F32)).astype(BF16)
    return jnp.concatenate([hi, lo], axis=1), hi


def _trunk(x3, p, tiles):
    B, S, _ = x3.shape
    n = B * S
    cos, sin = _rope_tables(S)
    mw = _prep_mla_weights(p['mla_w_in'][0], p['mla_q_lora_g'][0], p['mla_kv_lora_g'][0], p['mla_w_uq'][0],
                           p['mla_w_ukv'][0], p['mla_q_norm_g'][0], p['mla_k_norm_g'][0])
    kcos, ksin = _prep_k_tables(cos, sin, p['mla_k_norm_g'][0])
    qT, k, vT = _mla_prep(x3, p['norm1_g'][0].reshape(1, D_MODEL), mw, (kcos, ksin, cos.T, sin.T), tiles['tm'], tiles['tk'])
    fits = 2.0 * _score_bound_log2(p['mla_q_norm_g'][0], p['mla_k_norm_g'][0]) <= MAX_SHIFTED_SCORE_RANGE
    o = lax.cond(fits, lambda: _attention(qT, k, vT, tiles['tq'], True),
                 lambda: _attention(qT, k, vT, tiles['tq'], False)).reshape(n, D_MODEL)
    x = x3.reshape(n, D_MODEL)
    wr_cat, wr_hi = _router_weights(p['moe_w_router'][0])
    x1, h2, aff = _outproj(o, p['mla_w_out'][0].astype(BF16), x, p['norm2_g'][0].reshape(1, D_MODEL),
                           wr_cat, wr_hi, tiles['tm_out'])
    x = _moe(x1, h2, aff, p['moe_w_gate'], p['moe_w_up'], p['moe_w_down'], 0, tiles['ffn_rows'])
    wr_cat, wr_hi = _router_weights(p['moe_w_router'][1])
    bs_full = jnp.repeat(p['sgu_b_s'][0].T, LANES, axis=1)
    x1, h2, aff = _sgu(x, p['norm1_g'][1].reshape(1, D_MODEL), p['sgu_w_in'][0].astype(BF16),
                       p['sgu_ln_g'][0].reshape(1, SGU_HALF), p['sgu_ln_b'][0].reshape(1, SGU_HALF),
                       p['sgu_w_s'][0].astype(BF16), bs_full, p['sgu_w_out'][0].astype(BF16),
                       p['norm2_g'][1].reshape(1, D_MODEL), wr_cat, wr_hi, tiles['tm_sgu'])
    x = _moe(x1, h2, aff, p['moe_w_gate'], p['moe_w_up'], p['moe_w_down'], 1, tiles['ffn_rows'])
    return x.reshape(B, S, D_MODEL)


TILES = dict(tk=256, tq=1024, tm=512, tm_out=1024, tm_sgu=1024, ffn_rows=1024)


def kernel(x_prompt, x_sample, norm1_g, norm2_g, mla_w_in, mla_q_lora_g, mla_kv_lora_g, mla_w_uq, mla_w_ukv, mla_q_norm_g, mla_k_norm_g, mla_w_out, sgu_w_in, sgu_ln_g, sgu_ln_b, sgu_w_s, sgu_b_s, sgu_w_out, moe_w_router, moe_w_gate, moe_w_up, moe_w_down):
    p = dict(norm1_g=norm1_g, norm2_g=norm2_g, mla_w_in=mla_w_in, mla_q_lora_g=mla_q_lora_g,
             mla_kv_lora_g=mla_kv_lora_g, mla_w_uq=mla_w_uq, mla_w_ukv=mla_w_ukv, mla_q_norm_g=mla_q_norm_g,
             mla_k_norm_g=mla_k_norm_g, mla_w_out=mla_w_out, sgu_w_in=sgu_w_in, sgu_ln_g=sgu_ln_g,
             sgu_ln_b=sgu_ln_b, sgu_w_s=sgu_w_s, sgu_b_s=sgu_b_s, sgu_w_out=sgu_w_out, moe_w_router=moe_w_router,
             moe_w_gate=moe_w_gate, moe_w_up=moe_w_up, moe_w_down=moe_w_down)
    return (_trunk(x_prompt, p, TILES), _trunk(x_sample, p, TILES))
```

```python
import functools

import jax
import jax.numpy as jnp
import numpy as np
from jax import lax
from jax.experimental import pallas as pl
from jax.experimental.pallas import tpu as pltpu

F32 = jnp.float32
BF16 = jnp.bfloat16
I32 = jnp.int32

D_MODEL = 1024
MLA_HEADS = 16
Q_LORA = 256
KV_LORA = 128
NOPE_DIM = 64
ROPE_DIM = 32
HALF_ROPE = ROPE_DIM // 2
QK_HEAD = NOPE_DIM + ROPE_DIM
V_HEAD = 64
ROPE_THETA = 10000.0
CHUNK = 128
SGU_HALF = D_MODEL
SGU_GROUPS = 8
N_EXPERTS = 16
EXPERT_FF = 1024
CAPACITY_FACTOR = 2
EPS = 1e-6

LANES = 128
QK_PAD = 128
V_ROWS = 80
COMB_TILE = 128
DISP_TILE = 256
COMB_PER_STEP = 2
COMB_SLOTS = 3
ROW_ALIGN = 16
ROW_ALIGN_SHIFT = 4
COMB_LONG, COMB_SHORT = COMB_TILE + ROW_ALIGN, 64
DISP_LONG, DISP_SHORT = DISP_TILE + ROW_ALIGN, 80
NOT_SELECTED = -1024
BOUND_MARGIN = 1.0 + 2.0 ** -6
MAX_SHIFTED_SCORE_RANGE = 100.0
HEADS_PER_STEP = 2
KT_PER_TRIP = 32
QCHUNK = 256

VMEM_LIMIT = 52 * 1024 * 1024


def _cparams(sem):
    return pltpu.CompilerParams(dimension_semantics=sem, vmem_limit_bytes=VMEM_LIMIT)


def _dot(a, b):
    return jnp.dot(a, b, preferred_element_type=F32)


def _dot_nt(a, b):
    return lax.dot_general(a, b, (((1,), (1,)), ((), ())), preferred_element_type=F32)


def _rms_rows(x, g):
    return x * lax.rsqrt(jnp.mean(x * x, axis=-1, keepdims=True) + EPS) * g


def _mla_prep_kernel(x_ref, g1_ref, w_in_ref, gq_ref, gkv_ref, w_uqT_ref, w_uk_ref, w_uvT_ref,
                     gqc_ref, kb_ref, kcos_ref, ksin_ref, cosT_ref, sinT_ref,
                     qT_ref, k_ref, vT_ref):
    x = x_ref[...]
    h = _rms_rows(x, g1_ref[...]).astype(BF16)
    proj = _dot(h, w_in_ref[...])
    cq = _rms_rows(proj[:, :Q_LORA], gq_ref[...]).astype(BF16)
    ckv = _rms_rows(proj[:, Q_LORA:Q_LORA + KV_LORA], gkv_ref[...]).astype(BF16)
    kpe = proj[:, 384:512]
    kpe_sw = proj[:, 512:640]
    tm = x.shape[0]
    n_kt, tk = k_ref.shape[2], k_ref.shape[3]

    qT = _dot_nt(w_uqT_ref[...], cq)
    cosT = cosT_ref[...]
    sinT = sinT_ref[...]
    gqc = gqc_ref[...]
    kbound = kb_ref[...]
    brow = lax.broadcasted_iota(I32, (QK_PAD - QK_HEAD, tm), 0)
    for hd in range(MLA_HEADS):
        qh = qT[hd * QK_HEAD:(hd + 1) * QK_HEAD, :]
        r = lax.rsqrt(jnp.sum(qh * qh, axis=0, keepdims=True) * (1.0 / QK_HEAD) + EPS)
        qn = qh * gqc * r
        x1 = qn[NOPE_DIM:NOPE_DIM + HALF_ROPE]
        x2 = qn[NOPE_DIM + HALF_ROPE:QK_HEAD]
        qb = jnp.concatenate([qn[:NOPE_DIM], x1 * cosT - x2 * sinT, x2 * cosT + x1 * sinT], axis=0).astype(BF16)
        qT_ref[0, hd, 0:QK_HEAD, :] = qb
        qf = qb.astype(F32)
        bound = jnp.sqrt(jnp.sum(qf * qf, axis=0, keepdims=True)) * kbound
        qT_ref[0, hd, QK_HEAD:QK_PAD, :] = jnp.where(brow == 0, -bound, 0.0).astype(BF16)

    knope = _dot(ckv, w_uk_ref[...])
    kcos = kcos_ref[...]
    ksin = ksin_ref[...]
    one_lane = lax.broadcasted_iota(I32, (tm, QK_PAD), 1) == QK_HEAD
    for hd in range(MLA_HEADS):
        kh = knope[:, hd * QK_PAD:(hd + 1) * QK_PAD] + kpe
        r = lax.rsqrt(jnp.sum(kh * kh, axis=-1, keepdims=True) * (1.0 / QK_HEAD) + EPS)
        kb = jnp.where(one_lane, 1.0, (kh * kcos + kpe_sw * ksin) * r).astype(BF16)
        for j in range(n_kt):
            k_ref[0, hd, j] = kb[j * tk:(j + 1) * tk]

    vT = _dot_nt(w_uvT_ref[...], ckv)
    row = lax.broadcasted_iota(I32, (V_ROWS - V_HEAD, tk), 0)
    tail = jnp.where(row == 0, 1.0, 0.0).astype(BF16)
    for hd in range(MLA_HEADS):
        vb = vT[hd * V_HEAD:(hd + 1) * V_HEAD, :].astype(BF16)
        for j in range(n_kt):
            vT_ref[0, hd, j, 0:V_HEAD, :] = vb[:, j * tk:(j + 1) * tk]
            vT_ref[0, hd, j, V_HEAD:V_ROWS, :] = tail


def _mla_prep(x3, g1, wts, tabs, tm, tk):
    B, S, _ = x3.shape
    nb, per = S // tm, tm // tk
    full = lambda a: pl.BlockSpec(a.shape, lambda b, i: (0,) * a.ndim)
    in_specs = [pl.BlockSpec((None, tm, D_MODEL), lambda b, i: (b, i, 0)), full(g1)]
    in_specs += [full(w) for w in wts]
    in_specs += [pl.BlockSpec((tm, LANES), lambda b, i: (i, 0)),
                 pl.BlockSpec((tm, LANES), lambda b, i: (i, 0)),
                 pl.BlockSpec((HALF_ROPE, tm), lambda b, i: (0, i)),
                 pl.BlockSpec((HALF_ROPE, tm), lambda b, i: (0, i))]
    out_shape = [jax.ShapeDtypeStruct((B, MLA_HEADS, QK_PAD, S), BF16),
                 jax.ShapeDtypeStruct((B, MLA_HEADS, S // tk, tk, QK_PAD), BF16),
                 jax.ShapeDtypeStruct((B, MLA_HEADS, S // tk, V_ROWS, tk), BF16)]
    out_specs = [pl.BlockSpec((1, MLA_HEADS, QK_PAD, tm), lambda b, i: (b, 0, 0, i)),
                 pl.BlockSpec((1, MLA_HEADS, per, tk, QK_PAD), lambda b, i: (b, 0, i, 0, 0)),
                 pl.BlockSpec((1, MLA_HEADS, per, V_ROWS, tk), lambda b, i: (b, 0, i, 0, 0))]
    return pl.pallas_call(
        _mla_prep_kernel, grid=(B, nb), in_specs=in_specs, out_specs=out_specs, out_shape=out_shape,
        compiler_params=_cparams(("arbitrary", "arbitrary")), name="mla_prep",
    )(x3, g1, *wts, *tabs)


def _attn_kernel(qT_ref, k_ref, vT_ref, o_ref, *, n_kt):
    tq = qT_ref.shape[3]
    qTs = [qT_ref[0, hh] for hh in range(HEADS_PER_STEP)]

    def body(kt, carry):
        out = []
        for hh in range(HEADS_PER_STEP):
            m, acc = carry[hh]
            sT = _dot(k_ref[0, hh, kt], qTs[hh])
            m_new = jnp.maximum(m, jnp.max(sT, axis=0, keepdims=True))
            alpha = jnp.exp2(m - m_new)
            pT = jnp.exp2(sT - m_new).astype(BF16)
            acc = alpha * acc + _dot(vT_ref[0, hh, kt], pT)
            out.append((m_new, acc))
        return tuple(out)

    init = tuple((jnp.full((1, tq), -jnp.inf, F32), jnp.zeros((V_ROWS, tq), F32))
                 for _ in range(HEADS_PER_STEP))
    res = lax.fori_loop(0, n_kt, body, init, unroll=2)
    outs = [acc[:V_HEAD] / acc[V_HEAD:V_HEAD + 1] for _, acc in res]
    o_ref[0] = jnp.concatenate(outs, axis=0).T.astype(o_ref.dtype)


def _attn_bounded_kernel(qT_ref, k_ref, vT_ref, o_ref, *, n_kt):
    tq = qT_ref.shape[3]
    n_qc = tq // QCHUNK
    chains = [(hh, qc) for hh in range(HEADS_PER_STEP) for qc in range(n_qc)]
    qTs = [qT_ref[0, hh, :, qc * QCHUNK:(qc + 1) * QCHUNK] for hh, qc in chains]

    def scores(kt):
        return [_dot(k_ref[0, hh, kt], qTs[i]) for i, (hh, qc) in enumerate(chains)]

    def accumulate(accs, kt, sTs):
        return tuple(accs[i] + _dot(vT_ref[0, hh, kt], jnp.exp2(sTs[i]).astype(BF16))
                     for i, (hh, qc) in enumerate(chains))

    def body(j, accs):
        kt = j * per_trip
        sTs = scores(kt)
        accs = list(accs)
        for u in range(1, per_trip):
            for i, (hh, qc) in enumerate(chains):
                nxt_i = _dot(k_ref[0, hh, kt + u], qTs[i])
                accs[i] = accs[i] + _dot(vT_ref[0, hh, kt + u - 1], jnp.exp2(sTs[i]).astype(BF16))
                sTs[i] = nxt_i
        return accumulate(tuple(accs), kt + per_trip - 1, sTs)

    per_trip = min(KT_PER_TRIP, n_kt)
    assert n_kt % per_trip == 0
    accs = lax.fori_loop(0, n_kt // per_trip, body,
                         tuple(jnp.zeros((V_ROWS, QCHUNK), F32) for _ in chains))
    outs = [acc[:V_HEAD] / acc[V_HEAD:V_HEAD + 1] for acc in accs]
    heads = [jnp.concatenate(outs[hh * n_qc:(hh + 1) * n_qc], axis=1) for hh in range(HEADS_PER_STEP)]
    o_ref[0] = jnp.concatenate(heads, axis=0).T.astype(o_ref.dtype)


def _attention(qT, k, vT, tq, bounded):
    B, H, _, S = qT.shape
    n_kt, tk = k.shape[2], k.shape[3]
    hp = H // HEADS_PER_STEP
    return pl.pallas_call(
        functools.partial(_attn_bounded_kernel if bounded else _attn_kernel, n_kt=n_kt),
        grid=(B, hp, S // tq),
        in_specs=[pl.BlockSpec((1, HEADS_PER_STEP, QK_PAD, tq), lambda b, h, i: (b, h, 0, i)),
                  pl.BlockSpec((1, HEADS_PER_STEP, n_kt, tk, QK_PAD), lambda b, h, i: (b, h, 0, 0, 0)),
                  pl.BlockSpec((1, HEADS_PER_STEP, n_kt, V_ROWS, tk), lambda b, h, i: (b, h, 0, 0, 0))],
        out_specs=pl.BlockSpec((1, tq, HEADS_PER_STEP * V_HEAD), lambda b, h, i: (b, i, h)),
        out_shape=jax.ShapeDtypeStruct((B, S, H * V_HEAD), BF16),
        compiler_params=_cparams(("arbitrary", "arbitrary", "arbitrary")),
        name="mla_attention_bounded" if bounded else "mla_attention",
    )(qT, k, vT)


def _router_epilogue(x1, g2_ref, wr_cat_ref, wr_hi_ref, x1_ref, h2_ref, aff_ref):
    x1_ref[...] = x1
    h2 = _rms_rows(x1, g2_ref[...])
    h2_hi = h2.astype(BF16)
    h2_lo = (h2 - h2_hi.astype(F32)).astype(BF16)
    h2_ref[...] = h2_hi
    both = _dot(h2_hi, wr_cat_ref[...])
    logits = both[:, :N_EXPERTS] + both[:, N_EXPERTS:] + _dot(h2_lo, wr_hi_ref[...])
    ex = jnp.exp(logits - jnp.max(logits, axis=-1, keepdims=True))
    aff_ref[...] = ex / jnp.sum(ex, axis=-1, keepdims=True)


def _outproj_kernel(o_ref, w_ref, x_ref, g2_ref, wr_cat_ref, wr_hi_ref, x1_ref, h2_ref, aff_ref):
    x1 = x_ref[...] + _dot(o_ref[...], w_ref[...])
    _router_epilogue(x1, g2_ref, wr_cat_ref, wr_hi_ref, x1_ref, h2_ref, aff_ref)


def _row_specs(n, tm):
    out_shape = [jax.ShapeDtypeStruct((n, D_MODEL), F32),
                 jax.ShapeDtypeStruct((n, D_MODEL), BF16),
                 jax.ShapeDtypeStruct((n, N_EXPERTS), F32)]
    out_specs = [pl.BlockSpec((tm, D_MODEL), lambda i: (i, 0)),
                 pl.BlockSpec((tm, D_MODEL), lambda i: (i, 0)),
                 pl.BlockSpec((tm, N_EXPERTS), lambda i: (i, 0))]
    return out_shape, out_specs


def _outproj(o, w_out, x, g2, wr_cat, wr_hi, tm):
    n = x.shape[0]
    full = lambda a: pl.BlockSpec(a.shape, lambda i: (0,) * a.ndim)
    out_shape, out_specs = _row_specs(n, tm)
    return pl.pallas_call(
        _outproj_kernel, grid=(n // tm,),
        in_specs=[pl.BlockSpec((tm, D_MODEL), lambda i: (i, 0)), full(w_out),
                  pl.BlockSpec((tm, D_MODEL), lambda i: (i, 0)), full(g2), full(wr_cat), full(wr_hi)],
        out_specs=out_specs, out_shape=out_shape,
        compiler_params=_cparams(("arbitrary",)), name="mla_outproj_router",
    )(o, w_out, x, g2, wr_cat, wr_hi)


def _sgu_kernel(x_ref, g1_ref, w_in_ref, lng_ref, lnb_ref, ws_ref, bs_ref, w_out_ref,
                g2_ref, wr_cat_ref, wr_hi_ref, x1_ref, h2_ref, aff_ref):
    x = x_ref[...]
    tm = x.shape[0]
    h = _rms_rows(x, g1_ref[...]).astype(BF16)
    z = _dot(h, w_in_ref[...])
    z = 0.5 * z * (1.0 + lax.erf(z * np.float32(2.0 ** -0.5)))
    u = z[:, :SGU_HALF]
    v = z[:, SGU_HALF:]
    mu = jnp.mean(v, axis=-1, keepdims=True)
    vc = v - mu
    v = vc * lax.rsqrt(jnp.mean(vc * vc, axis=-1, keepdims=True) + EPS) * lng_ref[...] + lnb_ref[...]
    vb = v.astype(BF16)
    bs = bs_ref[...]
    n_chunks = tm // CHUNK
    mixed = []
    for g in range(SGU_GROUPS):
        blk = jnp.concatenate([vb[c * CHUNK:(c + 1) * CHUNK, g * LANES:(g + 1) * LANES]
                               for c in range(n_chunks)], axis=1)
        mixed.append(_dot(ws_ref[g], blk))
    sv = jnp.concatenate(
        [jnp.concatenate([mixed[g][:, c * LANES:(c + 1) * LANES] for g in range(SGU_GROUPS)], axis=1) + bs
         for c in range(n_chunks)], axis=0)
    gated = (u * sv).astype(BF16)
    x1 = x + _dot(gated, w_out_ref[...])
    _router_epilogue(x1, g2_ref, wr_cat_ref, wr_hi_ref, x1_ref, h2_ref, aff_ref)


def _sgu(x, g1, w_in, lng, lnb, ws, bs_full, w_out, g2, wr_cat, wr_hi, tm):
    n = x.shape[0]
    full = lambda a: pl.BlockSpec(a.shape, lambda i: (0,) * a.ndim)
    ops = (g1, w_in, lng, lnb, ws, bs_full, w_out, g2, wr_cat, wr_hi)
    out_shape, out_specs = _row_specs(n, tm)
    return pl.pallas_call(
        _sgu_kernel, grid=(n // tm,),
        in_specs=[pl.BlockSpec((tm, D_MODEL), lambda i: (i, 0))] + [full(a) for a in ops],
        out_specs=out_specs, out_shape=out_shape,
        compiler_params=_cparams(("arbitrary",)), name="sgu_router",
    )(x, *ops)


def _select_kernel(aff_ref, lp_ref, start_ref, cnt_ref, lp_pair_ref, *, nt, cap):
    a = aff_ref[...]
    bits = pltpu.bitcast(a, I32)

    def body(i, cur):
        trial = cur | jnp.left_shift(jnp.int32(1), 30 - i)
        ge = jnp.where(bits >= trial, 1.0, 0.0)
        cnt = jnp.sum(jnp.sum(ge, axis=1, keepdims=True), axis=2, keepdims=True)
        return jnp.where(cnt >= cap, trial, cur)

    thr = lax.fori_loop(0, 31, body, jnp.zeros((N_EXPERTS, 1, 1), I32))
    gtf = jnp.where(bits > thr, 1.0, 0.0)
    eqf = jnp.where(bits == thr, 1.0, 0.0)
    c_gt = jnp.sum(jnp.sum(gtf, axis=1, keepdims=True), axis=2, keepdims=True)
    need = cap - c_gt

    r = lax.broadcasted_iota(I32, (COMB_TILE, COMB_TILE), 0)
    c = lax.broadcasted_iota(I32, (COMB_TILE, COMB_TILE), 1)
    incl_u = jnp.where(r <= c, 1.0, 0.0).astype(BF16)
    ones_u = jnp.ones((COMB_TILE, LANES), BF16)
    rt = lax.broadcasted_iota(I32, (nt, nt), 0)
    ct = lax.broadcasted_iota(I32, (nt, nt), 1)
    excl_l = jnp.where(rt > ct, 1.0, 0.0).astype(BF16)

    def tile_prefix(maskf):
        m2 = maskf.reshape(N_EXPERTS * nt, COMB_TILE).astype(BF16)
        incl = _dot(m2, incl_u)
        cnt = _dot(m2, ones_u)
        cnt_b = cnt.astype(BF16)
        starts = [_dot(excl_l, cnt_b[e * nt:(e + 1) * nt]) for e in range(N_EXPERTS)]
        return incl, jnp.concatenate(starts, axis=0), cnt

    eq_incl, eq_start, _ = tile_prefix(eqf)
    tie_rank = (eq_start + eq_incl).reshape(N_EXPERTS, nt, COMB_TILE) - eqf
    selm = gtf + eqf * jnp.where(tie_rank < need, 1.0, 0.0)
    incl, start, cnt = tile_prefix(selm)
    sel2 = selm.reshape(N_EXPERTS * nt, COMB_TILE) > 0.5
    lp_ref[...] = jnp.where(sel2, incl - 1.0, float(NOT_SELECTED)).astype(I32)
    start_ref[...] = start.astype(I32)
    cnt_ref[...] = cnt.astype(I32)
    odd = (lax.broadcasted_iota(I32, (N_EXPERTS * nt, COMB_TILE), 0) & 1) == 1
    before = jnp.where(odd, jnp.concatenate([jnp.zeros((1, COMB_TILE), F32), cnt[:-1]], axis=0), 0.0)
    lp_pair_ref[...] = jnp.where(sel2, incl - 1.0 + before, float(NOT_SELECTED)).astype(I32)


def _select(aff_t3, cap):
    e, nt, _ = aff_t3.shape
    return pl.pallas_call(
        functools.partial(_select_kernel, nt=nt, cap=cap),
        out_shape=[jax.ShapeDtypeStruct((e * nt, COMB_TILE), I32)] * 4,
        compiler_params=pltpu.CompilerParams(vmem_limit_bytes=VMEM_LIMIT), name="moe_select",
    )(aff_t3)


def _round_down(v):
    return (v >> ROW_ALIGN_SHIFT) << ROW_ALIGN_SHIFT


def _by_slab_length(short, short_rows, long_rows, fn):
    @pl.when(short)
    def _():
        fn(short_rows)

    @pl.when(jnp.logical_not(short))
    def _():
        fn(long_rows)


def _dispatch_kernel(start_sm, cnt_sm, short_sm, x_ref, lp_ref, xg_hbm, stage, carry, sems, *, nt, cap):
    step = pl.program_id(0)

    def copy(s, e, row0, rows):
        dst = xg_hbm.at[e, pl.ds(pl.multiple_of(row0, ROW_ALIGN), rows), :]
        return pltpu.make_async_copy(stage.at[s, e, pl.ds(0, rows)], dst, sems.at[s, e])

    @pl.when(step == 0)
    def _():
        carry[...] = jnp.zeros(carry.shape, carry.dtype)
        stage[1] = jnp.zeros(stage.shape[1:], stage.dtype)
        for e in range(N_EXPERTS):
            copy(1, e, cap, DISP_LONG).start()
        for e in range(N_EXPERTS):
            copy(1, e, cap, DISP_LONG).wait()

    rio = lax.broadcasted_iota(I32, (ROW_ALIGN, D_MODEL), 0)
    by_length = functools.partial(_by_slab_length, short_rows=DISP_SHORT, long_rows=DISP_LONG)

    for slot in range(2):
        t = step * 2 + slot
        x = x_ref[pl.ds(slot * DISP_TILE, DISP_TILE), :]
        shifts = [start_sm[e * nt + t] - _round_down(start_sm[e * nt + t]) for e in range(N_EXPERTS)]

        def onehot_t(e, rows, slot=slot, shifts=shifts):
            jio = lax.broadcasted_iota(I32, (rows, DISP_TILE), 0)
            return jnp.where(lp_ref[slot, e:e + 1, :] + shifts[e] == jio, 1.0, 0.0).astype(BF16)

        def place(e, slab, rows, slot=slot, shifts=shifts, t=t):
            head = jnp.where(rio < shifts[e], carry[e].astype(F32), slab[0:ROW_ALIGN])
            stage[slot, e, 0:ROW_ALIGN] = head.astype(BF16)
            stage[slot, e, ROW_ALIGN:rows] = slab[ROW_ALIGN:].astype(BF16)
            nxt = pl.multiple_of(_round_down(shifts[e] + cnt_sm[e * nt + t]), ROW_ALIGN)
            carry[e] = stage[slot, e, pl.ds(nxt, ROW_ALIGN)]

        def fill(rows, x=x, onehot_t=onehot_t, place=place):
            if rows == DISP_SHORT:
                slabs = _dot(jnp.concatenate([onehot_t(e, rows) for e in range(N_EXPERTS)], axis=0), x)
                for e in range(N_EXPERTS):
                    place(e, slabs[e * rows:(e + 1) * rows], rows)
            else:
                for e in range(N_EXPERTS):
                    place(e, _dot(onehot_t(e, rows), x), rows)

        short_now = short_sm[t] == 1
        by_length(short_now, fn=fill)

        def hand_over(rows, t=t, slot=slot):
            def per_expert(prev_rows):
                for e in range(N_EXPERTS):
                    if prev_rows is not None:
                        copy(1 - slot, e, 0, prev_rows).wait()
                    copy(slot, e, _round_down(start_sm[e * nt + t]), rows).start(priority=e % 2)

            if slot == 0:
                pl.when(step == 0)(lambda: per_expert(None))
                pl.when(step > 0)(lambda: by_length(short_sm[t - 1] == 1, fn=per_expert))
            else:
                by_length(short_sm[t - 1] == 1, fn=per_expert)

        by_length(short_now, fn=hand_over)

    @pl.when(step == nt // 2 - 1)
    def _():
        by_length(short_sm[nt - 1] == 1, fn=lambda rows: [copy(1, e, 0, rows).wait() for e in range(N_EXPERTS)])


def _dispatch(start_flat, cnt_flat, short_flat, h2, lp3, cap):
    n = h2.shape[0]
    nt = n // DISP_TILE
    assert nt % 2 == 0
    grid_spec = pltpu.PrefetchScalarGridSpec(
        num_scalar_prefetch=3, grid=(nt // 2,),
        in_specs=[pl.BlockSpec((2 * DISP_TILE, D_MODEL), lambda t, *_: (t, 0)),
                  pl.BlockSpec((2, N_EXPERTS, DISP_TILE), lambda t, *_: (t, 0, 0))],
        out_specs=pl.BlockSpec(memory_space=pl.ANY),
        scratch_shapes=[pltpu.VMEM((2, N_EXPERTS, DISP_LONG, D_MODEL), BF16),
                        pltpu.VMEM((N_EXPERTS, ROW_ALIGN, D_MODEL), BF16),
                        pltpu.SemaphoreType.DMA((2, N_EXPERTS))])
    return pl.pallas_call(
        functools.partial(_dispatch_kernel, nt=nt, cap=cap), grid_spec=grid_spec,
        out_shape=jax.ShapeDtypeStruct((N_EXPERTS, cap + DISP_LONG, D_MODEL), BF16),
        compiler_params=_cparams(("arbitrary",)), name="moe_dispatch",
    )(start_flat, cnt_flat, short_flat, h2, lp3)


def _ffn_kernel(xg_ref, wg_ref, wu_ref, wd_ref, yg_ref, wg_b, wu_b, wd_b):
    @pl.when(pl.program_id(1) == 0)
    def _():
        wg_b[...] = wg_ref[0, 0].astype(BF16)
        wu_b[...] = wu_ref[0, 0].astype(BF16)
        wd_b[...] = wd_ref[0, 0].astype(BF16)

    x = xg_ref[0]
    g = _dot(x, wg_b[...])
    u = _dot(x, wu_b[...])
    hid = (g * jax.nn.sigmoid(g) * u).astype(BF16)
    y = _dot(hid, wd_b[...])
    yg_ref[0] = y.astype(BF16)


def _ffn(xg, wg, wu, wd, layer, cap, rt):
    wspec = pl.BlockSpec((1, 1, D_MODEL, EXPERT_FF), lambda e, r: (layer, e, 0, 0))
    return pl.pallas_call(
        _ffn_kernel, grid=(N_EXPERTS, cap // rt),
        in_specs=[pl.BlockSpec((1, rt, D_MODEL), lambda e, r: (e, r, 0)), wspec, wspec,
                  pl.BlockSpec((1, 1, EXPERT_FF, D_MODEL), lambda e, r: (layer, e, 0, 0))],
        out_specs=pl.BlockSpec((1, rt, D_MODEL), lambda e, r: (e, r, 0)),
        out_shape=jax.ShapeDtypeStruct((N_EXPERTS, cap, D_MODEL), BF16),
        scratch_shapes=[pltpu.VMEM((D_MODEL, EXPERT_FF), BF16), pltpu.VMEM((D_MODEL, EXPERT_FF), BF16),
                        pltpu.VMEM((EXPERT_FF, D_MODEL), BF16)],
        compiler_params=_cparams(("arbitrary", "arbitrary")), name="moe_ffn",
    )(xg, wg, wu, wd)


def _combine_kernel(start_sm, short_sm, x_ref, lpt_ref, aff_ref, yg_hbm, o_ref, slabs, sems, *, nt, cap):
    t = pl.program_id(0)
    slot = t % COMB_SLOTS
    n_steps = nt // COMB_PER_STEP

    def row0(e, tt, rows):
        return jnp.minimum(_round_down(start_sm[e * nt + tt]), cap - rows)

    def copy(s, j, e, r0, rows):
        src = yg_hbm.at[e, pl.ds(pl.multiple_of(r0, ROW_ALIGN), rows), :]
        return pltpu.make_async_copy(src, slabs.at[s, j, e, pl.ds(0, rows)], sems.at[s, j])

    by_length = functools.partial(_by_slab_length, short_rows=COMB_SHORT, long_rows=COMB_LONG)

    def fetch(s, step):
        for j in range(COMB_PER_STEP):
            tt = step * COMB_PER_STEP + j
            by_length(short_sm[tt] == 1,
                      fn=lambda rows, j=j, tt=tt: [copy(s, j, e, row0(e, tt, rows), rows).start(priority=e % 2)
                                                   for e in range(N_EXPERTS)])

    ahead = COMB_SLOTS - 1

    @pl.when(t == 0)
    def _():
        for s in range(min(ahead, n_steps)):
            fetch(s, s)

    @pl.when(t + ahead < n_steps)
    def _():
        fetch((t + ahead) % COMB_SLOTS, t + ahead)

    def expand_short(j, tt, rows):
        tok = pl.ds(j * COMB_TILE, COMB_TILE)
        for e in range(N_EXPERTS):
            copy(slot, j, e, 0, rows).wait()
        stacked = jnp.concatenate([slabs[slot, j, e, 0:rows] for e in range(N_EXPERTS)], axis=0)
        width = N_EXPERTS * rows
        first = lax.broadcasted_iota(I32, (N_EXPERTS, width), 0) * rows
        lane2 = lax.broadcasted_iota(I32, (N_EXPERTS, width), 1)
        spread = jnp.where((lane2 >= first) & (lane2 < first + rows), 1.0, 0.0).astype(BF16)
        g = aff_ref[tok, :]
        g_hi = g.astype(BF16)
        g_lo = (g - g_hi.astype(F32)).astype(BF16)
        cols = jnp.concatenate([lpt_ref[tok, :].astype(F32).astype(BF16), g_hi, g_lo], axis=0)
        wide = _dot(cols, spread)
        lane = lax.broadcasted_iota(I32, (1, width), 1)
        target = lane
        for e in range(N_EXPERTS):
            delta = start_sm[e * nt + tt] - row0(e, tt, rows)
            target = jnp.where((lane >= e * rows) & (lane < (e + 1) * rows), lane - (e * rows + delta), target)
        hit = wide[:COMB_TILE] == target.astype(F32)
        w = jnp.concatenate([jnp.where(hit, wide[COMB_TILE:2 * COMB_TILE], 0.0).astype(BF16),
                             jnp.where(hit, wide[2 * COMB_TILE:], 0.0).astype(BF16)], axis=0)
        both = _dot(w, stacked)
        o_ref[tok, :] = x_ref[tok, :] + both[:COMB_TILE] + both[COMB_TILE:]

    def expand_long(j, tt, rows):
        tok = pl.ds(j * COMB_TILE, COMB_TILE)
        for e in range(N_EXPERTS):
            copy(slot, j, e, 0, rows).wait()
        acc = x_ref[tok, :]
        lane = lax.broadcasted_iota(I32, (COMB_TILE, rows), 1)
        for e in range(N_EXPERTS):
            delta = start_sm[e * nt + tt] - row0(e, tt, rows)
            onehot = jnp.where(lpt_ref[tok, e:e + 1] + delta == lane, 1.0, 0.0).astype(BF16)
            acc = acc + aff_ref[tok, e:e + 1] * _dot(onehot, slabs[slot, j, e])
        o_ref[tok, :] = acc

    for j in range(COMB_PER_STEP):
        tt = t * COMB_PER_STEP + j
        by_length(short_sm[tt] == 1,
                  fn=lambda rows, j=j, tt=tt: expand_short(j, tt, rows) if rows == COMB_SHORT
                  else expand_long(j, tt, rows))


def _combine(start_flat, short_flat, x1, lp_t, aff, yg, cap):
    n = x1.shape[0]
    nt = n // COMB_TILE
    rows = COMB_TILE * COMB_PER_STEP
    grid_spec = pltpu.PrefetchScalarGridSpec(
        num_scalar_prefetch=2, grid=(nt // COMB_PER_STEP,),
        in_specs=[pl.BlockSpec((rows, D_MODEL), lambda t, s, c: (t, 0)),
                  pl.BlockSpec((rows, N_EXPERTS), lambda t, s, c: (t, 0)),
                  pl.BlockSpec((rows, N_EXPERTS), lambda t, s, c: (t, 0)),
                  pl.BlockSpec(memory_space=pl.ANY)],
        out_specs=pl.BlockSpec((rows, D_MODEL), lambda t, s, c: (t, 0)),
        scratch_shapes=[pltpu.VMEM((COMB_SLOTS, COMB_PER_STEP, N_EXPERTS, COMB_LONG, D_MODEL), BF16),
                        pltpu.SemaphoreType.DMA((COMB_SLOTS, COMB_PER_STEP))])
    return pl.pallas_call(
        functools.partial(_combine_kernel, nt=nt, cap=cap), grid_spec=grid_spec,
        out_shape=jax.ShapeDtypeStruct((n, D_MODEL), F32),
        compiler_params=_cparams(("arbitrary",)), name="moe_combine",
    )(start_flat, short_flat, x1, lp_t, aff, yg)


def _moe(x1, h2, aff, wg, wu, wd, layer, ffn_rows):
    n = x1.shape[0]
    nt, ntd = n // COMB_TILE, n // DISP_TILE
    cap = max(1, CAPACITY_FACTOR * n // N_EXPERTS)
    assert DISP_TILE == 2 * COMB_TILE and cap >= DISP_LONG and cap % ROW_ALIGN == 0
    lp, start, cnt, lp_pair = _select(aff.T.reshape(N_EXPERTS, nt, COMB_TILE), cap)
    start_c, cnt_c = start[:, 0].reshape(N_EXPERTS, nt), cnt[:, 0].reshape(N_EXPERTS, nt)
    start_d, cnt_d = start_c[:, ::2], cnt_c[:, ::2] + cnt_c[:, 1::2]

    def short_flags(st, ct, short_rows):
        return (jnp.max((st & (ROW_ALIGN - 1)) + ct, axis=0) <= short_rows - ROW_ALIGN).astype(I32)

    lp_t = lp.reshape(N_EXPERTS, n).T
    lp_tiles = lp_pair.reshape(N_EXPERTS, ntd, DISP_TILE).transpose(1, 0, 2)
    xg = _dispatch(start_d.reshape(-1), cnt_d.reshape(-1), short_flags(start_d, cnt_d, DISP_SHORT), h2, lp_tiles, cap)
    yg = _ffn(xg, wg, wu, wd, layer, cap, min(ffn_rows, cap))
    return _combine(start_c.reshape(-1), short_flags(start_c, cnt_c, COMB_SHORT), x1, lp_t, aff, yg, cap)


def _rope_tables(S):
    pos = np.arange(S, dtype=np.float64)
    inv_freq = 1.0 / (ROPE_THETA ** (np.arange(0, ROPE_DIM, 2, dtype=np.float64) / ROPE_DIM))
    ang = pos[:, None] * inv_freq[None, :]
    return jnp.asarray(np.cos(ang), F32), jnp.asarray(np.sin(ang), F32)


def _prep_mla_weights(w_in, gq, gkv, w_uq, w_ukv, qn_g, kn_g):
    pad = lambda a, lo, hi: jnp.pad(a, ((0, 0), (lo, hi)))
    pe = w_in[:, Q_LORA + KV_LORA:]
    pe_sw = jnp.concatenate([pe[:, HALF_ROPE:], pe[:, :HALF_ROPE]], axis=1)
    w_in_p = jnp.concatenate([w_in[:, :Q_LORA + KV_LORA], pad(pe, NOPE_DIM, QK_PAD - QK_HEAD),
                              pad(pe_sw, NOPE_DIM, QK_PAD - QK_HEAD)], axis=1).astype(BF16)
    w_uqT = w_uq.T.astype(BF16)
    ukv = w_ukv.reshape(KV_LORA, MLA_HEADS, NOPE_DIM + V_HEAD)
    w_uk = jnp.pad(ukv[:, :, :NOPE_DIM], ((0, 0), (0, 0), (0, QK_PAD - NOPE_DIM)))
    w_uk = w_uk.reshape(KV_LORA, MLA_HEADS * QK_PAD).astype(BF16)
    w_uvT = ukv[:, :, NOPE_DIM:].reshape(KV_LORA, MLA_HEADS * V_HEAD).T.astype(BF16)
    gqc = (qn_g * np.float32(QK_HEAD ** -0.5 * np.log2(np.e))).reshape(QK_HEAD, 1)
    kb = (jnp.max(jnp.abs(kn_g)) * np.float32(QK_HEAD ** 0.5 * BOUND_MARGIN)).reshape(1, 1)
    return (w_in_p, gq.reshape(1, Q_LORA), gkv.reshape(1, KV_LORA), w_uqT, w_uk, w_uvT, gqc, kb)


def _score_bound_log2(qn_g, kn_g):
    return (jnp.max(jnp.abs(qn_g)) * jnp.max(jnp.abs(kn_g))
            * np.float32(QK_HEAD * QK_HEAD ** -0.5 * np.log2(np.e) * BOUND_MARGIN * BOUND_MARGIN))


def _prep_k_tables(cos, sin, kn_g):
    S = cos.shape[0]
    z = lambda w: jnp.zeros((S, w), F32)
    cos_l = jnp.concatenate([jnp.ones((S, NOPE_DIM), F32), cos, cos, z(QK_PAD - QK_HEAD)], axis=1)
    sin_l = jnp.concatenate([z(NOPE_DIM), -sin, sin, z(QK_PAD - QK_HEAD)], axis=1)
    g = jnp.pad(kn_g, (0, QK_PAD - QK_HEAD))
    g_sw = jnp.concatenate([jnp.zeros((NOPE_DIM,), F32), kn_g[NOPE_DIM + HALF_ROPE:],
                            kn_g[NOPE_DIM:NOPE_DIM + HALF_ROPE], jnp.zeros((QK_PAD - QK_HEAD,), F32)])
    return cos_l * g[None, :], sin_l * g_sw[None, :]


def _router_weights(w_r):
    hi = w_r.astype(BF16)
    lo = (w_r - hi.astype(F32)).astype(BF16)
    return jnp.concatenate([hi, lo], axis=1), hi


def _trunk(x3, p, tiles):
    B, S, _ = x3.shape
    n = B * S
    cos, sin = _rope_tables(S)
    mw = _prep_mla_weights(p['mla_w_in'][0], p['mla_q_lora_g'][0], p['mla_kv_lora_g'][0], p['mla_w_uq'][0],
                           p['mla_w_ukv'][0], p['mla_q_norm_g'][0], p['mla_k_norm_g'][0])
    kcos, ksin = _prep_k_tables(cos, sin, p['mla_k_norm_g'][0])
    qT, k, vT = _mla_prep(x3, p['norm1_g'][0].reshape(1, D_MODEL), mw, (kcos, ksin, cos.T, sin.T), tiles['tm'], tiles['tk'])
    fits = 2.0 * _score_bound_log2(p['mla_q_norm_g'][0], p['mla_k_norm_g'][0]) <= MAX_SHIFTED_SCORE_RANGE
    o = lax.cond(fits, lambda: _attention(qT, k, vT, tiles['tq'], True),
                 lambda: _attention(qT, k, vT, tiles['tq'], False)).reshape(n, D_MODEL)
    x = x3.reshape(n, D_MODEL)
    wr_cat, wr_hi = _router_weights(p['moe_w_router'][0])
    x1, h2, aff = _outproj(o, p['mla_w_out'][0].astype(BF16), x, p['norm2_g'][0].reshape(1, D_MODEL),
                           wr_cat, wr_hi, tiles['tm_out'])
    x = _moe(x1, h2, aff, p['moe_w_gate'], p['moe_w_up'], p['moe_w_down'], 0, tiles['ffn_rows'])
    wr_cat, wr_hi = _router_weights(p['moe_w_router'][1])
    bs_full = jnp.repeat(p['sgu_b_s'][0].T, LANES, axis=1)
    x1, h2, aff = _sgu(x, p['norm1_g'][1].reshape(1, D_MODEL), p['sgu_w_in'][0].astype(BF16),
                       p['sgu_ln_g'][0].reshape(1, SGU_HALF), p['sgu_ln_b'][0].reshape(1, SGU_HALF),
                       p['sgu_w_s'][0].astype(BF16), bs_full, p['sgu_w_out'][0].astype(BF16),
                       p['norm2_g'][1].reshape(1, D_MODEL), wr_cat, wr_hi, tiles['tm_sgu'])
    x = _moe(x1, h2, aff, p['moe_w_gate'], p['moe_w_up'], p['moe_w_down'], 1, tiles['ffn_rows'])
    return x.reshape(B, S, D_MODEL)


TILES = dict(tk=256, tq=1024, tm=512, tm_out=1024, tm_sgu=1024, ffn_rows=1024)


def kernel(x_prompt, x_sample, norm1_g, norm2_g, mla_w_in, mla_q_lora_g, mla_kv_lora_g, mla_w_uq, mla_w_ukv, mla_q_norm_g, mla_k_norm_g, mla_w_out, sgu_w_in, sgu_ln_g, sgu_ln_b, sgu_w_s, sgu_b_s, sgu_w_out, moe_w_router, moe_w_gate, moe_w_up, moe_w_down):
    p = dict(norm1_g=norm1_g, norm2_g=norm2_g, mla_w_in=mla_w_in, mla_q_lora_g=mla_q_lora_g,
             mla_kv_lora_g=mla_kv_lora_g, mla_w_uq=mla_w_uq, mla_w_ukv=mla_w_ukv, mla_q_norm_g=mla_q_norm_g,
             mla_k_norm_g=mla_k_norm_g, mla_w_out=mla_w_out, sgu_w_in=sgu_w_in, sgu_ln_g=sgu_ln_g,
             sgu_ln_b=sgu_ln_b, sgu_w_s=sgu_w_s, sgu_b_s=sgu_b_s, sgu_w_out=sgu_w_out, moe_w_router=moe_w_router,
             moe_w_gate=moe_w_gate, moe_w_up=moe_w_up, moe_w_down=moe_w_down)
    return (_trunk(x_prompt, p, TILES), _trunk(x_sample, p, TILES))
```
